```python
import jax, jax.numpy as jnp
from jax import lax
import numpy as np

D_MODEL = 1024
BATCH = 4
SEQ = 4096
DEPTH = 1
DEC_BATCH = 32
DEC_SEQ = 8
PAST_LEN = 16384
PAGE_SIZE = 128

N_MEM = 256
D_CONV = 512
CONV_WIDTH = 31
N_HEADS = 8
HEAD_DIM = 64
D_ATT = N_HEADS * HEAD_DIM
N_IDX_HEADS = 8
IDX_DIM = 64
TOPK_MAX = 256
MEM_HEADS = 4
MEM_HEAD_DIM = 128
D_MEM = MEM_HEADS * MEM_HEAD_DIM
N_BRANCH = 3
D_FF = -(-8 * D_MODEL // (3 * 256)) * 256
Q_BLOCK = 128
EPS = 1e-6
NEG = -1e30
SPLITS = [2 * D_CONV, D_ATT, D_ATT, D_ATT, N_IDX_HEADS * IDX_DIM, IDX_DIM, N_IDX_HEADS, D_MEM, N_BRANCH * D_MODEL]
D_IN = sum(SPLITS)

kernel_name = 'hybrid_conv_dsa_mem_decoder_step'


def rmsnorm(x, g):
    xf = x.astype(jnp.float32)
    y = xf * lax.rsqrt(jnp.mean(xf * xf, axis=-1, keepdims=True) + EPS)
    return (y * g.astype(jnp.float32)).astype(x.dtype)


def layernorm(x, g, b):
    xf = x.astype(jnp.float32)
    mu = jnp.mean(xf, axis=-1, keepdims=True)
    xc = xf - mu
    y = xc * lax.rsqrt(jnp.mean(xc * xc, axis=-1, keepdims=True) + EPS)
    return (y * g.astype(jnp.float32) + b.astype(jnp.float32)).astype(x.dtype)


def branch_inputs(x, g_mix, w_in):
    n, t = x.shape[:2]
    z = rmsnorm(x, g_mix) @ w_in
    offs = np.cumsum(SPLITS)[:-1].tolist()
    glu, q, k, v, qi, ki, wi, qm, gates = jnp.split(z, offs, axis=-1)
    ga, gb = jnp.split(glu, 2, axis=-1)
    u = ga * jax.nn.sigmoid(gb)
    q = q.reshape(n, t, N_HEADS, HEAD_DIM)
    k = k.reshape(n, t, N_HEADS, HEAD_DIM)
    v = v.reshape(n, t, N_HEADS, HEAD_DIM)
    qi = qi.reshape(n, t, N_IDX_HEADS, IDX_DIM)
    qm = qm.reshape(n, t, MEM_HEADS, MEM_HEAD_DIM)
    gates = jax.nn.sigmoid(gates).reshape(n, t, N_BRANCH, D_MODEL)
    return u, q, k, v, qi, ki, wi, qm, gates


def conv_branch(u_ext, w_dw, b_dw, g_ln, b_ln, w_out):
    c = lax.conv_general_dilated(u_ext, w_dw[:, None, :], window_strides=(1,), padding='VALID',
                                 dimension_numbers=('NWC', 'WIO', 'NWC'), feature_group_count=D_CONV)
    c = jax.nn.silu(layernorm(c + b_dw, g_ln, b_ln))
    return c @ w_out


def indexer_scores(qi, wi, ki):
    dots = jnp.einsum('nthd,nld->nthl', qi, ki).astype(jnp.float32) * (IDX_DIM ** -0.5)
    return jnp.einsum('nth,nthl->ntl', wi.astype(jnp.float32) * (N_IDX_HEADS ** -0.5), jax.nn.relu(dots))


def select_keys(scores, qpos, n_sel):
    L = scores.shape[-1]
    causal = jnp.arange(L)[None, None, :] <= qpos[None, :, None]
    _, idx = lax.top_k(jnp.where(causal, scores, NEG), n_sel)
    valid = idx <= qpos[None, :, None]
    return idx, valid


def sparse_attend(q, kg, vg, valid):
    s = jnp.einsum('nthd,ntkhd->nthk', q, kg).astype(jnp.float32) * (HEAD_DIM ** -0.5)
    s = jnp.where(valid[:, :, None, :], s, NEG)
    p = jax.nn.softmax(s, axis=-1).astype(vg.dtype)
    return jnp.einsum('nthk,ntkhd->nthd', p, vg)


_gather_rows = jax.vmap(lambda a, i: a[i])


def dsa_prompt(q, k, v, qi, ki, wi):
    b, s = q.shape[:2]
    n_sel = min(TOPK_MAX, s // 4)
    nb = s // Q_BLOCK

    def blocks(a):
        return jnp.swapaxes(a.reshape((b, nb, Q_BLOCK) + a.shape[2:]), 0, 1)

    def one_block(args):
        qb, qib, wib, bi = args
        qpos = bi * Q_BLOCK + jnp.arange(Q_BLOCK)
        idx, valid = select_keys(indexer_scores(qib, wib, ki), qpos, n_sel)
        return sparse_attend(qb, _gather_rows(k, idx), _gather_rows(v, idx), valid)

    out = lax.map(one_block, (blocks(q), blocks(qi), blocks(wi), jnp.arange(nb)))
    return jnp.swapaxes(out, 0, 1).reshape(q.shape)


def dsa_sample(q, k_new, v_new, qi, ki_new, wi, cache_k, cache_v, cache_idx_k, page_table):
    n, t = q.shape[:2]
    past = page_table.shape[1] * PAGE_SIZE
    n_sel = min(TOPK_MAX, (past + t) // 4)
    ki_past = cache_idx_k[page_table].reshape(n, past, IDX_DIM)
    ki_all = jnp.concatenate([ki_past, ki_new], axis=1)
    qpos = past + jnp.arange(t)
    idx, valid = select_keys(indexer_scores(qi, wi, ki_all), qpos, n_sel)
    in_past = (idx < past)[..., None, None]
    pidx = jnp.minimum(idx, past - 1)
    phys = _gather_rows(page_table, pidx // PAGE_SIZE)
    off = pidx % PAGE_SIZE
    nidx = jnp.clip(idx - past, 0, t - 1)
    kg = jnp.where(in_past, cache_k[phys, off], _gather_rows(k_new, nidx))
    vg = jnp.where(in_past, cache_v[phys, off], _gather_rows(v_new, nidx))
    return sparse_attend(q, kg, vg, valid)


def mem_kv(mem, g, w):
    n, m = mem.shape[:2]
    mk, mv = jnp.split(rmsnorm(mem, g) @ w, 2, axis=-1)
    return mk.reshape(n, m, MEM_HEADS, MEM_HEAD_DIM), mv.reshape(n, m, MEM_HEADS, MEM_HEAD_DIM)


def mem_attend(q, mk, mv):
    s = jnp.einsum('nthd,nmhd->nhtm', q, mk).astype(jnp.float32) * (MEM_HEAD_DIM ** -0.5)
    p = jax.nn.softmax(s, axis=-1).astype(mv.dtype)
    return jnp.einsum('nhtm,nmhd->nthd', p, mv)


def merge_and_ffn(x, oc, oa, om, gates, w_att_out, w_mem_out, w_out, g_ffn, w_ffn_in, w_ffn_out):
    n, t = x.shape[:2]
    oa = oa.reshape(n, t, D_ATT) @ w_att_out
    om = om.reshape(n, t, D_MEM) @ w_mem_out
    m = gates[:, :, 0] * oc + gates[:, :, 1] * oa + gates[:, :, 2] * om
    x = x + m @ w_out
    a, b = jnp.split(rmsnorm(x, g_ffn) @ w_ffn_in, 2, axis=-1)
    return x + (jax.nn.silu(a) * b) @ w_ffn_out


def setup_inputs(seed: int = 0) -> dict:
    key = jax.random.key(seed)
    ks = jax.random.split(key, 32)
    f32 = jnp.float32
    n_pages = PAST_LEN // PAGE_SIZE
    n_used = DEC_BATCH * n_pages
    n_phys = n_used + max(1, n_used // 4)

    def nrm(k, shape, scale=1.0):
        return jax.random.normal(k, shape, f32) * scale

    def gain(k, shape):
        return 1.0 + 0.05 * jax.random.normal(k, shape, f32)

    page_table = jax.random.permutation(ks[0], n_phys)[:n_used].reshape(DEC_BATCH, n_pages).astype(jnp.int32)
    return {
        'x_prompt': nrm(ks[1], (BATCH, SEQ, D_MODEL)),
        'x_sample': nrm(ks[2], (DEC_BATCH, DEC_SEQ, D_MODEL)),
        'mem_prompt': nrm(ks[3], (BATCH, N_MEM, D_MODEL)),
        'cache_conv': nrm(ks[4], (DEPTH, DEC_BATCH, CONV_WIDTH - 1, D_CONV), 0.5),
        'cache_k': nrm(ks[5], (DEPTH, n_phys, PAGE_SIZE, N_HEADS, HEAD_DIM)),
        'cache_v': nrm(ks[6], (DEPTH, n_phys, PAGE_SIZE, N_HEADS, HEAD_DIM)),
        'cache_idx_k': nrm(ks[7], (DEPTH, n_phys, PAGE_SIZE, IDX_DIM)),
        'cache_mem_k': nrm(ks[8], (DEPTH, DEC_BATCH, N_MEM, MEM_HEADS, MEM_HEAD_DIM)),
        'cache_mem_v': nrm(ks[9], (DEPTH, DEC_BATCH, N_MEM, MEM_HEADS, MEM_HEAD_DIM)),
        'page_table': page_table,
        'g_mix': gain(ks[10], (DEPTH, D_MODEL)),
        'w_in': nrm(ks[11], (DEPTH, D_MODEL, D_IN), D_MODEL ** -0.5),
        'w_conv_dw': nrm(ks[12], (DEPTH, CONV_WIDTH, D_CONV), CONV_WIDTH ** -0.5),
        'b_conv_dw': nrm(ks[13], (DEPTH, D_CONV), 0.02),
        'g_conv_ln': gain(ks[14], (DEPTH, D_CONV)),
        'b_conv_ln': nrm(ks[15], (DEPTH, D_CONV), 0.02),
        'w_conv_out': nrm(ks[16], (DEPTH, D_CONV, D_MODEL), D_CONV ** -0.5),
        'w_att_out': nrm(ks[17], (DEPTH, D_ATT, D_MODEL), D_ATT ** -0.5),
        'g_mem': gain(ks[18], (DEPTH, D_MODEL)),
        'w_mem_kv': nrm(ks[19], (DEPTH, D_MODEL, 2 * D_MEM), D_MODEL ** -0.5),
        'w_mem_out': nrm(ks[20], (DEPTH, D_MEM, D_MODEL), D_MEM ** -0.5),
        'w_out': nrm(ks[21], (DEPTH, D_MODEL, D_MODEL), D_MODEL ** -0.5),
        'g_ffn': gain(ks[22], (DEPTH, D_MODEL)),
        'w_ffn_in': nrm(ks[23], (DEPTH, D_MODEL, 2 * D_FF), D_MODEL ** -0.5),
        'w_ffn_out': nrm(ks[24], (DEPTH, D_FF, D_MODEL), D_FF ** -0.5),
        'g_final': gain(ks[25], (D_MODEL,)),
    }


def reference(x_prompt, x_sample, mem_prompt, cache_conv, cache_k, cache_v, cache_idx_k, cache_mem_k, cache_mem_v,
              page_table, g_mix, w_in, w_conv_dw, b_conv_dw, g_conv_ln, b_conv_ln, w_conv_out, w_att_out, g_mem,
              w_mem_kv, w_mem_out, w_out, g_ffn, w_ffn_in, w_ffn_out, g_final):
    xp, xs = x_prompt, x_sample
    b = xp.shape[0]
    conv_p, k_p, v_p, ik_p, mk_p, mv_p = [], [], [], [], [], []
    conv_s, k_s, v_s, ik_s = [], [], [], []
    for l in range(DEPTH):
        u, q, k, v, qi, ki, wi, qm, gates = branch_inputs(xp, g_mix[l], w_in[l])
        u_ext = jnp.concatenate([jnp.zeros((b, CONV_WIDTH - 1, D_CONV), u.dtype), u], axis=1)
        oc = conv_branch(u_ext, w_conv_dw[l], b_conv_dw[l], g_conv_ln[l], b_conv_ln[l], w_conv_out[l])
        oa = dsa_prompt(q, k, v, qi, ki, wi)
        mk, mv = mem_kv(mem_prompt, g_mem[l], w_mem_kv[l])
        om = mem_attend(qm, mk, mv)
        xp = merge_and_ffn(xp, oc, oa, om, gates, w_att_out[l], w_mem_out[l], w_out[l], g_ffn[l], w_ffn_in[l], w_ffn_out[l])
        conv_p.append(u_ext[:, -(CONV_WIDTH - 1):])
        k_p.append(k)
        v_p.append(v)
        ik_p.append(ki)
        mk_p.append(mk)
        mv_p.append(mv)
        u, q, k, v, qi, ki, wi, qm, gates = branch_inputs(xs, g_mix[l], w_in[l])
        u_ext = jnp.concatenate([cache_conv[l].astype(u.dtype), u], axis=1)
        oc = conv_branch(u_ext, w_conv_dw[l], b_conv_dw[l], g_conv_ln[l], b_conv_ln[l], w_conv_out[l])
        oa = dsa_sample(q, k, v, qi, ki, wi, cache_k[l], cache_v[l], cache_idx_k[l], page_table)
        om = mem_attend(qm, cache_mem_k[l], cache_mem_v[l])
        xs = merge_and_ffn(xs, oc, oa, om, gates, w_att_out[l], w_mem_out[l], w_out[l], g_ffn[l], w_ffn_in[l], w_ffn_out[l])
        conv_s.append(u_ext[:, -(CONV_WIDTH - 1):])
        k_s.append(k)
        v_s.append(v)
        ik_s.append(ki)
    y_prompt = rmsnorm(xp, g_final)
    y_sample = rmsnorm(xs, g_final)
    conv_state_prompt = jnp.stack(conv_p)
    k_prompt = jnp.stack(k_p)
    v_prompt = jnp.stack(v_p)
    idx_k_prompt = jnp.stack(ik_p)
    mem_k_prompt = jnp.stack(mk_p)
    mem_v_prompt = jnp.stack(mv_p)
    conv_state_sample = jnp.stack(conv_s)
    k_sample = jnp.stack(k_s)
    v_sample = jnp.stack(v_s)
    idx_k_sample = jnp.stack(ik_s)
    return (y_prompt, y_sample, conv_state_prompt, k_prompt, v_prompt, idx_k_prompt, mem_k_prompt, mem_v_prompt,
            conv_state_sample, k_sample, v_sample, idx_k_sample)
```

```python
import functools

import jax
import jax.numpy as jnp
import numpy as np
from jax import lax
from jax.experimental import pallas as pl
from jax.experimental.pallas import tpu as pltpu

EPS = 1e-6
NEG = -1e30
CONV_WIDTH = 31
N_HEADS = 8
HEAD_DIM = 64
N_IDX_HEADS = 8
IDX_DIM = 64
MEM_HEADS = 4
MEM_HEAD_DIM = 128
TOPK_MAX = 256
PAGE_SIZE = 128
LANES = 128
VMEM_LIMIT = 56 * 1024 * 1024
INT_MIN = -(2 ** 31)

BF16 = jnp.bfloat16
F32 = jnp.float32
I32 = jnp.int32


def _cparams(n_axes):
    return pltpu.CompilerParams(dimension_semantics=("arbitrary",) * n_axes, vmem_limit_bytes=VMEM_LIMIT)


def _const_spec(shape):
    zeros = (0,) * len(shape)
    return pl.BlockSpec(shape, lambda *_: zeros)


def _rmsnorm(x, g):
    return x * lax.rsqrt(jnp.mean(x * x, axis=-1, keepdims=True) + EPS) * g


def _dot(a, b):
    return jnp.dot(a, b, preferred_element_type=F32)


def _dot_nt(a, b):
    return lax.dot_general(a, b, (((1,), (1,)), ((), ())), preferred_element_type=F32)


def _inproj_common(x, g_ref, w_ref, u_ref, q_ref, qi_ref, wi_ref, qm_ref, h_ref, d_conv, d_att, d_idx, d_mem):
    h = _rmsnorm(x, g_ref[...]).astype(BF16)
    h_ref[...] = h
    o = 0
    glu = _dot(h, w_ref[:, o:o + 2 * d_conv]); o += 2 * d_conv
    u_ref[...] = glu[:, :d_conv] * jax.nn.sigmoid(glu[:, d_conv:])
    q_ref[...] = (_dot(h, w_ref[:, o:o + d_att]) * (HEAD_DIM ** -0.5)).astype(BF16); o += d_att
    qi_ref[...] = (_dot(h, w_ref[:, o:o + d_idx]) * (IDX_DIM ** -0.5)).astype(BF16); o += d_idx
    wi_ref[...] = _dot(h, w_ref[:, o:o + LANES])[:, :N_IDX_HEADS] * (N_IDX_HEADS ** -0.5); o += LANES
    qm_ref[...] = _dot(h, w_ref[:, o:o + d_mem]).astype(BF16)
    return h


def _inproj_prompt_kernel(x_ref, g_ref, w_ref, wkv_ref, u_ref, q_ref, qi_ref, wi_ref, qm_ref, h_ref,
                          kt_ref, vt_ref, ktb_ref, vtb_ref, kit_ref, kitb_ref, *, d_conv, d_att, d_idx, d_mem):
    h = _inproj_common(x_ref[0], g_ref, w_ref, u_ref, q_ref, qi_ref, wi_ref, qm_ref, h_ref, d_conv, d_att, d_idx,
                       d_mem)
    kt = _dot_nt(wkv_ref[0:d_att, :], h)
    kt_ref[0] = kt
    ktb_ref[0] = kt.astype(BF16)
    vt = _dot_nt(wkv_ref[d_att:2 * d_att, :], h)
    vt_ref[0] = vt
    vtb_ref[0] = vt.astype(BF16)
    kit = _dot_nt(wkv_ref[2 * d_att:2 * d_att + IDX_DIM, :], h)
    kit_ref[0] = kit
    kitb_ref[0] = kit.astype(BF16)


def _inproj_sample_kernel(x_ref, g_ref, w_ref, wkv_ref, u_ref, q_ref, qi_ref, wi_ref, qm_ref, h_ref,
                          k_ref, v_ref, ki_ref, *, d_conv, d_att, d_idx, d_mem):
    h = _inproj_common(x_ref[...], g_ref, w_ref, u_ref, q_ref, qi_ref, wi_ref, qm_ref, h_ref, d_conv, d_att, d_idx,
                       d_mem)
    k_ref[...] = _dot(h, wkv_ref[:, 0:d_att])
    v_ref[...] = _dot(h, wkv_ref[:, d_att:2 * d_att])
    ki_ref[...] = _dot(h, wkv_ref[:, 2 * d_att:2 * d_att + LANES])[:, :IDX_DIM]


def _inproj_out_types(d, d_conv, d_att, d_idx, d_mem):
    return [(d_conv, F32), (d_att, BF16), (d_idx, BF16), (N_IDX_HEADS, F32), (d_mem, BF16), (d, BF16)]


def _inproj_prompt(x, g, w, wkv_t, *, tm, d_conv, d_att, d_idx, d_mem):
    bsz, s, d = x.shape
    nt = s // tm
    row = lambda n: pl.BlockSpec((tm, n), lambda b, i: (b * nt + i, 0))
    col = lambda n: pl.BlockSpec((1, n, tm), lambda b, i: (b, 0, i))
    outs = _inproj_out_types(d, d_conv, d_att, d_idx, d_mem)
    outs_t = [(d_att, F32), (d_att, F32), (d_att, BF16), (d_att, BF16), (IDX_DIM, F32), (IDX_DIM, BF16)]
    return pl.pallas_call(
        functools.partial(_inproj_prompt_kernel, d_conv=d_conv, d_att=d_att, d_idx=d_idx, d_mem=d_mem),
        grid=(bsz, nt),
        in_specs=[pl.BlockSpec((1, tm, d), lambda b, i: (b, i, 0)), _const_spec((1, d)), _const_spec(w.shape),
                  _const_spec(wkv_t.shape)],
        out_specs=[row(n) for n, _ in outs] + [col(n) for n, _ in outs_t],
        out_shape=[jax.ShapeDtypeStruct((bsz * s, n), dt) for n, dt in outs]
                  + [jax.ShapeDtypeStruct((bsz, n, s), dt) for n, dt in outs_t],
        compiler_params=_cparams(2),
        name="inproj_prompt",
    )(x, g, w, wkv_t)


def _inproj_sample(x, g, w, wkv, *, d_conv, d_att, d_idx, d_mem):
    m, d = x.shape
    outs = _inproj_out_types(d, d_conv, d_att, d_idx, d_mem) + [(d_att, F32), (d_att, F32), (IDX_DIM, F32)]
    return pl.pallas_call(
        functools.partial(_inproj_sample_kernel, d_conv=d_conv, d_att=d_att, d_idx=d_idx, d_mem=d_mem),
        grid=(1,),
        in_specs=[_const_spec((m, d)), _const_spec((1, d)), _const_spec(w.shape), _const_spec(wkv.shape)],
        out_specs=[_const_spec((m, n)) for n, _ in outs],
        out_shape=[jax.ShapeDtypeStruct((m, n), dt) for n, dt in outs],
        compiler_params=_cparams(1),
        name="inproj_sample",
    )(x, g, w, wkv)


def _conv_rows(ext_ref, start, rows, w_ref, b_ref, g_ref, bl_ref):
    acc = w_ref[0:1, :] * ext_ref[pl.ds(start, rows), :]
    for j in range(1, CONV_WIDTH):
        acc = acc + w_ref[j:j + 1, :] * ext_ref[pl.ds(start + j, rows), :]
    c = acc + b_ref[...]
    mu = jnp.mean(c, axis=-1, keepdims=True)
    xc = c - mu
    y = xc * lax.rsqrt(jnp.mean(xc * xc, axis=-1, keepdims=True) + EPS) * g_ref[...] + bl_ref[...]
    return (y * jax.nn.sigmoid(y)).astype(BF16)


def _conv_prompt_kernel(prev_ref, cur_ref, w_ref, b_ref, g_ref, bl_ref, o_ref, ext_ref, *, halo, chunk):
    t = cur_ref.shape[1]

    @pl.when(pl.program_id(1) == 0)
    def _():
        ext_ref[0:halo, :] = jnp.zeros((halo, ext_ref.shape[1]), F32)

    @pl.when(pl.program_id(1) > 0)
    def _():
        ext_ref[0:halo, :] = prev_ref[0]

    ext_ref[halo:halo + t, :] = cur_ref[0]
    first = halo - (CONV_WIDTH - 1)
    for c in range(t // chunk):
        o_ref[0, c * chunk:(c + 1) * chunk, :] = _conv_rows(ext_ref, first + c * chunk, chunk, w_ref, b_ref, g_ref,
                                                            bl_ref)


def _conv_prompt(u, w, b, g, bl, *, t=512, halo=32, chunk=64):
    bsz, s, dc = u.shape
    assert s % t == 0 and t % halo == 0 and halo >= CONV_WIDTH - 1 and t % chunk == 0
    r = t // halo
    return pl.pallas_call(
        functools.partial(_conv_prompt_kernel, halo=halo, chunk=chunk),
        grid=(bsz, s // t),
        in_specs=[pl.BlockSpec((1, halo, dc), lambda bi, i: (bi, jnp.maximum(i * r - 1, 0), 0)),
                  pl.BlockSpec((1, t, dc), lambda bi, i: (bi, i, 0)),
                  _const_spec(w.shape), _const_spec((1, dc)), _const_spec((1, dc)), _const_spec((1, dc))],
        out_specs=pl.BlockSpec((1, t, dc), lambda bi, i: (bi, i, 0)),
        out_shape=jax.ShapeDtypeStruct((bsz, s, dc), BF16),
        scratch_shapes=[pltpu.VMEM((halo + t, dc), F32)],
        compiler_params=_cparams(2),
        name="conv_prompt",
    )(u, u, w, b, g, bl)


def _conv_sample_kernel(ext_ref, w_ref, b_ref, g_ref, bl_ref, o_ref):
    t = o_ref.shape[1]
    o_ref[0] = _conv_rows(ext_ref.at[0], 0, t, w_ref, b_ref, g_ref, bl_ref)


def _conv_sample(u_ext, w, b, g, bl):
    n, te, dc = u_ext.shape
    t = te - (CONV_WIDTH - 1)
    return pl.pallas_call(
        _conv_sample_kernel,
        grid=(n,),
        in_specs=[pl.BlockSpec((1, te, dc), lambda i: (i, 0, 0)),
                  _const_spec(w.shape), _const_spec((1, dc)), _const_spec((1, dc)), _const_spec((1, dc))],
        out_specs=pl.BlockSpec((1, t, dc), lambda i: (i, 0, 0)),
        out_shape=jax.ShapeDtypeStruct((n, t, dc), BF16),
        compiler_params=_cparams(1),
        name="conv_sample",
    )(u_ext, w, b, g, bl)


def _memkv_kernel(x_ref, g_ref, w_ref, k_ref, v_ref):
    h = _rmsnorm(x_ref[...], g_ref[...]).astype(BF16)
    kv = _dot(h, w_ref[...])
    d = k_ref.shape[1]
    k_ref[...] = kv[:, :d]
    v_ref[...] = kv[:, d:]


def _memkv(x, g, w, *, tm=256):
    m, d = x.shape
    dm = w.shape[1] // 2
    return pl.pallas_call(
        _memkv_kernel,
        grid=(m // tm,),
        in_specs=[pl.BlockSpec((tm, d), lambda i: (i, 0)), _const_spec((1, d)), _const_spec(w.shape)],
        out_specs=[pl.BlockSpec((tm, dm), lambda i: (i, 0))] * 2,
        out_shape=[jax.ShapeDtypeStruct((m, dm), F32)] * 2,
        compiler_params=_cparams(1),
        name="memkv",
    )(x, g, w)


def _memattn_kernel(q_ref, k_ref, v_ref, o_ref):
    for h in range(MEM_HEADS):
        sl = slice(h * MEM_HEAD_DIM, (h + 1) * MEM_HEAD_DIM)
        s = _dot_nt(q_ref[0, :, sl], k_ref[0, :, sl].astype(BF16)) * (MEM_HEAD_DIM ** -0.5)
        e = jnp.exp(s - jnp.max(s, axis=-1, keepdims=True))
        p = e / jnp.sum(e, axis=-1, keepdims=True)
        o_ref[0, :, sl] = _dot(p.astype(BF16), v_ref[0, :, sl].astype(BF16)).astype(BF16)


def _memattn(q, k, v, *, tq):
    n, t, dm = q.shape
    nm = k.shape[1]
    return pl.pallas_call(
        _memattn_kernel,
        grid=(n, t // tq),
        in_specs=[pl.BlockSpec((1, tq, dm), lambda i, j: (i, j, 0)),
                  pl.BlockSpec((1, nm, dm), lambda i, j: (i, 0, 0)),
                  pl.BlockSpec((1, nm, dm), lambda i, j: (i, 0, 0))],
        out_specs=pl.BlockSpec((1, tq, dm), lambda i, j: (i, j, 0)),
        out_shape=jax.ShapeDtypeStruct((n, t, dm), BF16),
        compiler_params=_cparams(2),
        name="memattn",
    )(q, k, v)


def _ordinal_to_float(o):
    return pltpu.bitcast(jnp.where(o >= 0, o, (-o) | INT_MIN), F32)


def _select_mask(score_ref, mask_ref, n_tiles, n_sel, outside, idx_bits, valid_fn, mask_dtype):
    _, rows, tk = score_ref.shape
    shp = (rows, LANES)
    fold = tk // LANES
    k_f = jnp.float32(n_sel)
    out_f = jnp.float32(1.0) * outside

    def lane_fold(x):
        acc = x[:, :LANES]
        for i in range(1, fold):
            acc = acc + x[:, i * LANES:(i + 1) * LANES]
        return acc

    def rep(x):
        return x if fold == 1 else jnp.concatenate([x] * fold, axis=1)

    def row_total(part):
        return jnp.broadcast_to(jnp.sum(part, axis=1, keepdims=True), shp)

    def count_ge(thr):
        thr_t = rep(thr)

        def body(j, part):
            return part + lane_fold(jnp.where(score_ref[j] >= thr_t, 1.0, 0.0))

        part = lax.fori_loop(0, n_tiles, body, jnp.zeros(shp, F32))
        return row_total(part) + jnp.where(thr <= NEG, out_f, 0.0)

    def bit_step(it, carry):
        u, cnt_u = carry
        bit = lax.shift_left(jnp.int32(1), 31 - it)
        cand = u | bit
        cnt = count_ge(_ordinal_to_float(cand ^ INT_MIN))
        ok = cnt >= k_f
        return jnp.where(ok, cand, u), jnp.where(ok, cnt, cnt_u)

    total = jnp.float32(1.0) * (n_tiles * tk) + out_f
    u, cnt_ge = lax.fori_loop(0, 32, bit_step, (jnp.zeros(shp, I32), jnp.broadcast_to(total, shp)))
    thr = _ordinal_to_float(u ^ INT_MIN)
    thr_t = rep(thr)

    surplus = jnp.max(jnp.where(cnt_ge > k_f, 1.0, 0.0)) > 0.0

    def tie_search():
        def count_gt_body(j, part):
            return part + lane_fold(jnp.where(score_ref[j] > thr_t, 1.0, 0.0))

        cnt_gt = row_total(lax.fori_loop(0, n_tiles, count_gt_body, jnp.zeros(shp, F32)))
        cnt_gt = cnt_gt + jnp.where(thr < NEG, out_f, 0.0)
        in_tiles = n_tiles * tk

        def idx_step(it, v):
            bit = lax.shift_left(jnp.int32(1), idx_bits - 1 - it)
            cand = v | bit
            cand_t = rep(cand)

            def body(j, part):
                idx = j * tk + lax.broadcasted_iota(I32, (rows, tk), 1)
                hit = jnp.where(score_ref[j] == thr_t, jnp.where(idx < cand_t, 1.0, 0.0), 0.0)
                return part + lane_fold(hit)

            ties_below = row_total(lax.fori_loop(0, n_tiles, body, jnp.zeros(shp, F32)))
            out_below = jnp.clip(cand - in_tiles, 0, outside).astype(F32)
            ties_below = ties_below + jnp.where(thr == NEG, out_below, 0.0)
            return jnp.where(cnt_gt + ties_below < k_f, cand, v)

        return lax.fori_loop(0, idx_bits, idx_step, jnp.zeros(shp, I32))

    jmax = lax.cond(surplus, tie_search, lambda: jnp.full(shp, 2 ** idx_bits - 1, I32))
    jmax_t = rep(jmax)

    def emit(j, _):
        sc = score_ref[j]
        idx = j * tk + lax.broadcasted_iota(I32, (rows, tk), 1)
        sel = jnp.where(sc > thr_t, 1.0, jnp.where(sc == thr_t, jnp.where(idx <= jmax_t, 1.0, 0.0), 0.0))
        mask_ref[j] = jnp.where(valid_fn(j, (rows, tk)), sel, 0.0).astype(mask_dtype)
        return 0

    lax.fori_loop(0, n_tiles, emit, 0)


def _sel_prompt_kernel(qi_ref, wi_ref, kit_ref, mask_ref, key_ref, *, n_sel):
    tq = qi_ref.shape[1]
    s = kit_ref.shape[2]
    n_all = s // tq
    qb = pl.program_id(1)
    n_tiles = qb + 1
    row = qb * tq + lax.broadcasted_iota(I32, (tq, tq), 0)

    def score_tile(j):
        kit_t = kit_ref[0, :, pl.ds(pl.multiple_of(j * tq, tq), tq)]
        acc = jnp.zeros((tq, tq), F32)
        for h in range(N_IDX_HEADS):
            d = _dot(qi_ref[0, :, h * IDX_DIM:(h + 1) * IDX_DIM], kit_t)
            acc = acc + wi_ref[0, :, h:h + 1] * jnp.maximum(d, 0.0)
        return acc

    def below(j, _):
        key_ref[j] = score_tile(j)
        return 0

    lax.fori_loop(0, qb, below, 0)
    col = qb * tq + lax.broadcasted_iota(I32, (tq, tq), 1)
    key_ref[qb] = jnp.where(col <= row, score_tile(qb), NEG)

    def valid(j, shape):
        return j * tq + lax.broadcasted_iota(I32, shape, 1) <= row

    idx_bits = max(1, int(s - 1).bit_length())
    _select_mask(key_ref, mask_ref.at[0, 0], n_tiles, n_sel, (n_all - n_tiles) * tq, idx_bits, valid, BF16)

    def clear(j, _):
        mask_ref[0, 0, j] = jnp.zeros((tq, tq), BF16)
        return 0

    lax.fori_loop(n_tiles, n_all, clear, 0)


def _sel_prompt(qi, wi, kitb, *, tq, n_sel):
    bsz, s, _ = qi.shape
    nq = s // tq
    return pl.pallas_call(
        functools.partial(_sel_prompt_kernel, n_sel=n_sel),
        grid=(bsz, nq),
        in_specs=[pl.BlockSpec((1, tq, qi.shape[2]), lambda b, i: (b, i, 0)),
                  pl.BlockSpec((1, tq, wi.shape[2]), lambda b, i: (b, i, 0)),
                  pl.BlockSpec((1, kitb.shape[1], s), lambda b, i: (b, 0, 0))],
        out_specs=pl.BlockSpec((1, 1, nq, tq, tq), lambda b, i: (b, i, 0, 0, 0)),
        out_shape=jax.ShapeDtypeStruct((bsz, nq, nq, tq, tq), BF16),
        scratch_shapes=[pltpu.VMEM((nq, tq, tq), F32)],
        compiler_params=_cparams(2),
        name="sel_prompt",
    )(qi, wi, kitb)


def _attn_prompt_kernel(q_ref, kt_ref, vt_ref, mask_ref, o_ref):
    tq = q_ref.shape[1]
    n_tiles = pl.program_id(1) + 1
    outs = []
    for h in range(N_HEADS):
        sl = slice(h * HEAD_DIM, (h + 1) * HEAD_DIM)
        q_h = q_ref[0, :, sl]

        def body(j, carry, sl=sl, q_h=q_h):
            m, l, acc = carry
            keys = pl.ds(pl.multiple_of(j * tq, tq), tq)
            sel = mask_ref[0, 0, j].astype(F32) > 0.0
            s = jnp.where(sel, _dot(q_h, kt_ref[0, sl, keys]), NEG)
            m_new = jnp.maximum(m, jnp.max(s, axis=-1, keepdims=True))
            alpha = jnp.exp(m - m_new)
            p = jnp.where(sel, jnp.exp(s - m_new), 0.0)
            l = alpha * l + jnp.sum(p, axis=-1, keepdims=True)
            acc = alpha * acc + _dot_nt(p.astype(BF16), vt_ref[0, sl, keys])
            return m_new, l, acc

        init = (jnp.full((tq, 1), NEG, F32), jnp.zeros((tq, 1), F32), jnp.zeros((tq, HEAD_DIM), F32))
        _, l, acc = lax.fori_loop(0, n_tiles, body, init)
        outs.append(acc / l)
    o_ref[0] = jnp.concatenate(outs, axis=1).astype(BF16)


def _attn_prompt(q, ktb, vtb, mask, *, tq):
    bsz, s, da = q.shape
    nq = s // tq
    return pl.pallas_call(
        _attn_prompt_kernel,
        grid=(bsz, nq),
        in_specs=[pl.BlockSpec((1, tq, da), lambda b, i: (b, i, 0)),
                  pl.BlockSpec((1, da, s), lambda b, i: (b, 0, 0)),
                  pl.BlockSpec((1, da, s), lambda b, i: (b, 0, 0)),
                  pl.BlockSpec((1, 1, nq, tq, tq), lambda b, i: (b, i, 0, 0, 0))],
        out_specs=pl.BlockSpec((1, tq, da), lambda b, i: (b, i, 0)),
        out_shape=jax.ShapeDtypeStruct((bsz, s, da), BF16),
        compiler_params=_cparams(2),
        name="attn_prompt",
    )(q, ktb, vtb, mask)


def _score_sample_kernel(pt_ref, qi_ref, wi_ref, kin_ref, *rest, pages):
    page_refs = rest[:pages]
    past_ref, new_ref = rest[pages:]
    t = new_ref.shape[1]
    qi = qi_ref[0]
    wi = wi_ref[0]

    def scores(dots):
        w = wi * jnp.maximum(dots, 0.0)
        acc = w[0:t]
        for h in range(1, N_IDX_HEADS):
            acc = acc + w[h * t:(h + 1) * t]
        return acc

    for i in range(pages):
        past_ref[0, :, i * PAGE_SIZE:(i + 1) * PAGE_SIZE] = scores(_dot(qi, page_refs[i][0].astype(BF16)))

    @pl.when(pl.program_id(1) == 0)
    def _():
        ext = jnp.concatenate([kin_ref[0], jnp.zeros((LANES - t, IDX_DIM), F32)], axis=0).astype(BF16)
        sc = scores(_dot_nt(qi, ext))
        qpos = lax.broadcasted_iota(I32, (t, LANES), 0)
        kpos = lax.broadcasted_iota(I32, (t, LANES), 1)
        new_ref[0] = jnp.where(kpos < t, jnp.where(kpos <= qpos, sc, NEG), -jnp.inf)


def _score_sample(page_table, qi_hm, wi_hm, ki_new, cache_idx_k, *, pages):
    n, n_pages = page_table.shape
    t = ki_new.shape[1]
    ht = qi_hm.shape[1]
    steps = n_pages // pages

    def page_spec(i):
        return pl.BlockSpec((1, IDX_DIM, PAGE_SIZE), lambda b, j, pt: (pt[b, j * pages + i], 0, 0))

    grid_spec = pltpu.PrefetchScalarGridSpec(
        num_scalar_prefetch=1,
        grid=(n, steps),
        in_specs=[pl.BlockSpec((1, ht, IDX_DIM), lambda b, j, pt: (b, 0, 0)),
                  pl.BlockSpec((1, ht, 1), lambda b, j, pt: (b, 0, 0)),
                  pl.BlockSpec((1, t, IDX_DIM), lambda b, j, pt: (b, 0, 0))]
                 + [page_spec(i) for i in range(pages)],
        out_specs=[pl.BlockSpec((1, t, pages * PAGE_SIZE), lambda b, j, pt: (b, 0, j)),
                   pl.BlockSpec((1, t, LANES), lambda b, j, pt: (b, 0, 0))],
    )
    return pl.pallas_call(
        functools.partial(_score_sample_kernel, pages=pages),
        grid_spec=grid_spec,
        out_shape=[jax.ShapeDtypeStruct((n, t, n_pages * PAGE_SIZE), F32),
                   jax.ShapeDtypeStruct((n, t, LANES), F32)],
        compiler_params=_cparams(2),
        name="score_sample",
    )(page_table, qi_hm, wi_hm, ki_new, *([cache_idx_k] * pages))


def _sel_sample_kernel(past_ref, new_ref, mpast_ref, mnew_ref, key_ref, mask_ref, *, n_sel, t, tile):
    rows = past_ref.shape[0]
    n_past = past_ref.shape[1] // tile
    for j in range(n_past):
        key_ref[j] = past_ref[:, j * tile:(j + 1) * tile]
    pad = jnp.full((rows, tile - LANES), -jnp.inf, F32)
    key_ref[n_past] = jnp.concatenate([new_ref[...], pad], axis=1)
    qpos = lax.rem(lax.broadcasted_iota(I32, (rows, tile), 0), t)

    def valid(j, shape):
        kpos = lax.broadcasted_iota(I32, shape, 1)
        return kpos <= qpos + jnp.minimum(n_past - j, 1) * tile

    idx_bits = int((n_past + 1) * tile - 1).bit_length()
    _select_mask(key_ref, mask_ref, n_past + 1, n_sel, 0, idx_bits, valid, F32)
    for j in range(n_past):
        mpast_ref[:, j * tile:(j + 1) * tile] = mask_ref[j]
    mnew_ref[...] = mask_ref[n_past][:, :LANES]


def _sel_sample(key_past, key_new, *, n_sel, t, rows=64, tile=2048):
    r, n_keys = key_past.shape
    n_t = n_keys // tile + 1
    return pl.pallas_call(
        functools.partial(_sel_sample_kernel, n_sel=n_sel, t=t, tile=tile),
        grid=(r // rows,),
        in_specs=[pl.BlockSpec((rows, n_keys), lambda i: (i, 0)), pl.BlockSpec((rows, LANES), lambda i: (i, 0))],
        out_specs=[pl.BlockSpec((rows, n_keys), lambda i: (i, 0)), pl.BlockSpec((rows, LANES), lambda i: (i, 0))],
        out_shape=[jax.ShapeDtypeStruct((r, n_keys), F32), jax.ShapeDtypeStruct((r, LANES), F32)],
        scratch_shapes=[pltpu.VMEM((n_t, rows, tile), F32), pltpu.VMEM((n_t, rows, tile), F32)],
        compiler_params=_cparams(1),
        name="sel_sample",
    )(key_past, key_new)


def _attn_sample_kernel(pt_ref, q_ref, kn_ref, vn_ref, mpast_ref, mnew_ref, *rest, pages):
    k_refs = rest[:pages]
    v_refs = rest[pages:2 * pages]
    o_ref, qbd_ref, m_ref, l_ref, acc_ref = rest[2 * pages:]
    t, da = q_ref.shape[1], q_ref.shape[2]
    ht = N_HEADS * t
    j = pl.program_id(1)

    @pl.when(j == 0)
    def _():
        q_rep = jnp.concatenate([q_ref[0].astype(F32)] * N_HEADS, axis=0)
        head_of_row = lax.broadcasted_iota(I32, (ht, da), 0) // t
        head_of_col = lax.broadcasted_iota(I32, (ht, da), 1) // HEAD_DIM
        qbd_ref[...] = jnp.where(head_of_row == head_of_col, q_rep, 0.0).astype(BF16)
        m_ref[...] = jnp.full(m_ref.shape, NEG, F32)
        l_ref[...] = jnp.zeros(l_ref.shape, F32)
        acc_ref[...] = jnp.zeros(acc_ref.shape, F32)

    def attend(k_page, v_page, mask_t, feature_major):
        sel = jnp.concatenate([mask_t] * N_HEADS, axis=0) > 0.0
        qk = _dot(qbd_ref[...], k_page) if feature_major else _dot_nt(qbd_ref[...], k_page)
        s = jnp.where(sel, qk, NEG)
        m_old = m_ref[...]
        m_new = jnp.maximum(m_old, jnp.max(s, axis=-1, keepdims=True))
        alpha = jnp.exp(m_old - m_new)
        p = jnp.where(sel, jnp.exp(s - m_new), 0.0)
        l_ref[...] = alpha * l_ref[...] + jnp.sum(p, axis=-1, keepdims=True)
        pv = _dot_nt(p.astype(BF16), v_page) if feature_major else _dot(p.astype(BF16), v_page)
        acc_ref[...] = alpha * acc_ref[...] + pv
        m_ref[...] = m_new

    for i in range(pages):
        attend(k_refs[i][0].astype(BF16), v_refs[i][0].astype(BF16),
               mpast_ref[0, :, i * PAGE_SIZE:(i + 1) * PAGE_SIZE], True)

    @pl.when(j == pl.num_programs(1) - 1)
    def _():
        zeros = jnp.zeros((PAGE_SIZE - t, da), F32)
        attend(jnp.concatenate([kn_ref[0], zeros], axis=0).astype(BF16),
               jnp.concatenate([vn_ref[0], zeros], axis=0).astype(BF16), mnew_ref[0], False)
        out = acc_ref[...] / l_ref[...]
        o_ref[0] = jnp.concatenate(
            [out[h * t:(h + 1) * t, h * HEAD_DIM:(h + 1) * HEAD_DIM] for h in range(N_HEADS)], axis=1).astype(BF16)


def _attn_sample(page_table, q, k_new, v_new, mask_past, mask_new, cache_k, cache_v, *, pages):
    n, n_pages = page_table.shape
    t, da = q.shape[1], q.shape[2]
    steps = n_pages // pages
    ht = N_HEADS * t

    def page_spec(i):
        return pl.BlockSpec((1, da, PAGE_SIZE), lambda b, j, pt: (pt[b, j * pages + i], 0, 0))

    per_seq = lambda shape: pl.BlockSpec((1,) + shape, lambda b, j, pt: (b, 0, 0))
    grid_spec = pltpu.PrefetchScalarGridSpec(
        num_scalar_prefetch=1,
        grid=(n, steps),
        in_specs=[per_seq((t, da)), per_seq((t, da)), per_seq((t, da)),
                  pl.BlockSpec((1, t, pages * PAGE_SIZE), lambda b, j, pt: (b, 0, j)), per_seq((t, LANES))]
                 + [page_spec(i) for i in range(pages)] + [page_spec(i) for i in range(pages)],
        out_specs=per_seq((t, da)),
        scratch_shapes=[pltpu.VMEM((ht, da), BF16), pltpu.VMEM((ht, 1), F32), pltpu.VMEM((ht, 1), F32),
                        pltpu.VMEM((ht, da), F32)],
    )
    return pl.pallas_call(
        functools.partial(_attn_sample_kernel, pages=pages),
        grid_spec=grid_spec,
        out_shape=jax.ShapeDtypeStruct((n, t, da), BF16),
        compiler_params=_cparams(2),
        name="attn_sample",
    )(page_table, q, k_new, v_new, mask_past, mask_new, *([cache_k] * pages), *([cache_v] * pages))


def _merge_kernel(x_ref, h_ref, c_ref, a_ref, m_ref, wg_ref, wc_ref, wa_ref, wm_ref, wo_ref, o_ref):
    d = x_ref.shape[1]
    gates = jax.nn.sigmoid(_dot(h_ref[...], wg_ref[...]))
    mix = (gates[:, :d] * _dot(c_ref[...], wc_ref[...])
           + gates[:, d:2 * d] * _dot(a_ref[...], wa_ref[...])
           + gates[:, 2 * d:] * _dot(m_ref[...], wm_ref[...]))
    o_ref[...] = x_ref[...] + _dot(mix.astype(BF16), wo_ref[...])


def _merge(x, h, c, a, mo, wg, wc, wa, wm, wo, *, tm):
    m, d = x.shape
    row = lambda n: pl.BlockSpec((tm, n), lambda i: (i, 0))
    return pl.pallas_call(
        _merge_kernel,
        grid=(m // tm,),
        in_specs=[row(d), row(d), row(c.shape[1]), row(a.shape[1]), row(mo.shape[1]),
                  _const_spec(wg.shape), _const_spec(wc.shape), _const_spec(wa.shape), _const_spec(wm.shape),
                  _const_spec(wo.shape)],
        out_specs=row(d),
        out_shape=jax.ShapeDtypeStruct((m, d), F32),
        compiler_params=_cparams(1),
        name="merge",
    )(x, h, c, a, mo, wg, wc, wa, wm, wo)


def _ffn_kernel(x_ref, g_ref, wi_ref, wo_ref, gf_ref, o_ref):
    x = x_ref[...]
    dff = wo_ref.shape[0]
    hn = _rmsnorm(x, g_ref[...]).astype(BF16)
    a = _dot(hn, wi_ref[:, :dff])
    b = _dot(hn, wi_ref[:, dff:])
    y = x + _dot((a * jax.nn.sigmoid(a) * b).astype(BF16), wo_ref[...])
    o_ref[...] = _rmsnorm(y, gf_ref[...])


def _ffn(x, g, w_in, w_out, g_final, *, tm):
    m, d = x.shape
    return pl.pallas_call(
        _ffn_kernel,
        grid=(m // tm,),
        in_specs=[pl.BlockSpec((tm, d), lambda i: (i, 0)), _const_spec((1, d)), _const_spec(w_in.shape),
                  _const_spec(w_out.shape), _const_spec((1, d))],
        out_specs=pl.BlockSpec((tm, d), lambda i: (i, 0)),
        out_shape=jax.ShapeDtypeStruct((m, d), F32),
        compiler_params=_cparams(1),
        name="ffn",
    )(x, g, w_in, w_out, g_final)


def kernel(x_prompt, x_sample, mem_prompt, cache_conv, cache_k, cache_v, cache_idx_k, cache_mem_k, cache_mem_v,
           page_table, g_mix, w_in, w_conv_dw, b_conv_dw, g_conv_ln, b_conv_ln, w_conv_out, w_att_out, g_mem,
           w_mem_kv, w_mem_out, w_out, g_ffn, w_ffn_in, w_ffn_out, g_final):
    depth = g_mix.shape[0]
    assert depth == 1
    bsz, seq, d = x_prompt.shape
    n_dec, t_dec, _ = x_sample.shape
    n_mem = mem_prompt.shape[1]
    d_conv = w_conv_dw.shape[2]
    d_att = N_HEADS * HEAD_DIM
    d_idx = N_IDX_HEADS * IDX_DIM
    d_mem = MEM_HEADS * MEM_HEAD_DIM
    n_phys = cache_k.shape[1]
    past = page_table.shape[1] * PAGE_SIZE
    l = 0

    splits = [2 * d_conv, d_att, d_att, d_att, d_idx, IDX_DIM, N_IDX_HEADS, d_mem, 3 * d]
    offs = np.concatenate([[0], np.cumsum(splits)])
    w = w_in[l]
    col = lambda i: w[:, offs[i]:offs[i + 1]]
    pad_to_tile = lambda a: jnp.pad(a, ((0, 0), (0, LANES - a.shape[1])))
    w_proj = jnp.concatenate([col(0), col(1), col(4), pad_to_tile(col(6)), col(7)], axis=1).astype(BF16)
    w_kv = jnp.concatenate([col(2), col(3), pad_to_tile(col(5))], axis=1).astype(BF16)
    w_kv_t = jnp.concatenate([col(2), col(3), col(5)], axis=1).T.astype(BF16)
    w_gates = col(8).astype(BF16)
    bf = lambda a: a[l].astype(BF16)
    row2 = lambda a: a.reshape(1, -1)
    wdw, bdw, gln, bln = w_conv_dw[l], row2(b_conv_dw[l]), row2(g_conv_ln[l]), row2(b_conv_ln[l])
    dims = dict(d_conv=d_conv, d_att=d_att, d_idx=d_idx, d_mem=d_mem)
    merge = functools.partial(_merge, wg=w_gates, wc=bf(w_conv_out), wa=bf(w_att_out), wm=bf(w_mem_out),
                              wo=bf(w_out))
    ffn = functools.partial(_ffn, g=row2(g_ffn[l]), w_in=bf(w_ffn_in), w_out=bf(w_ffn_out),
                            g_final=row2(g_final))

    mp = bsz * seq
    xp = x_prompt.reshape(mp, d)
    u, q, qi, wi, qm, h, kt, vt, ktb, vtb, kit, kitb = _inproj_prompt(x_prompt, row2(g_mix[l]), w_proj, w_kv_t,
                                                                      tm=512, **dims)
    per_seq = lambda a: a.reshape(bsz, seq, a.shape[-1])
    u3 = per_seq(u)
    c = _conv_prompt(u3, wdw, bdw, gln, bln)
    n_sel = min(TOPK_MAX, seq // 4)
    tq = 256
    mask = _sel_prompt(per_seq(qi), per_seq(wi), kitb, tq=tq, n_sel=n_sel)
    oa = _attn_prompt(per_seq(q), ktb, vtb, mask, tq=tq)
    mk, mv = _memkv(mem_prompt.reshape(bsz * n_mem, d), row2(g_mem[l]), bf(w_mem_kv))
    om = _memattn(per_seq(qm), mk.reshape(bsz, n_mem, d_mem), mv.reshape(bsz, n_mem, d_mem), tq=512)
    x1 = merge(xp, h, c.reshape(mp, d_conv), oa.reshape(mp, d_att), om.reshape(mp, d_mem), tm=512)
    y_prompt = ffn(x1, tm=256).reshape(bsz, seq, d)
    conv_state_prompt = u3[:, seq - (CONV_WIDTH - 1):][None]
    k_prompt = kt.reshape(bsz, N_HEADS, HEAD_DIM, seq).transpose(0, 3, 1, 2)[None]
    v_prompt = vt.reshape(bsz, N_HEADS, HEAD_DIM, seq).transpose(0, 3, 1, 2)[None]
    idx_k_prompt = kit.transpose(0, 2, 1)[None]
    mem_k_prompt = mk.reshape(1, bsz, n_mem, MEM_HEADS, MEM_HEAD_DIM)
    mem_v_prompt = mv.reshape(1, bsz, n_mem, MEM_HEADS, MEM_HEAD_DIM)

    ms = n_dec * t_dec
    xs = x_sample.reshape(ms, d)
    u, q, qi, wi, qm, h, k, v, ki = _inproj_sample(xs, row2(g_mix[l]), w_proj, w_kv, **dims)
    per_seq = lambda a: a.reshape(n_dec, t_dec, a.shape[-1])
    u_ext = jnp.concatenate([cache_conv[l], per_seq(u)], axis=1)
    c = _conv_sample(u_ext, wdw, bdw, gln, bln)
    n_sel = min(TOPK_MAX, (past + t_dec) // 4)
    qi_hm = per_seq(qi).reshape(n_dec, t_dec, N_IDX_HEADS, IDX_DIM).transpose(0, 2, 1, 3)
    qi_hm = qi_hm.reshape(n_dec, N_IDX_HEADS * t_dec, IDX_DIM)
    wi_hm = per_seq(wi).transpose(0, 2, 1).reshape(n_dec, N_IDX_HEADS * t_dec, 1)
    page_major = lambda a: jnp.moveaxis(a[l], 1, -1).reshape(n_phys, -1, PAGE_SIZE)
    key_past, key_new = _score_sample(page_table, qi_hm, wi_hm, per_seq(ki), page_major(cache_idx_k), pages=16)
    mask_past, mask_new = _sel_sample(key_past.reshape(ms, past), key_new.reshape(ms, LANES), n_sel=n_sel, t=t_dec)
    oa = _attn_sample(page_table, per_seq(q), per_seq(k), per_seq(v), mask_past.reshape(n_dec, t_dec, past),
                      mask_new.reshape(n_dec, t_dec, LANES), page_major(cache_k), page_major(cache_v), pages=4)
    om = _memattn(per_seq(qm), cache_mem_k[l].reshape(n_dec, n_mem, d_mem),
                  cache_mem_v[l].reshape(n_dec, n_mem, d_mem), tq=t_dec)
    x1 = merge(xs, h, c.reshape(ms, d_conv), oa.reshape(ms, d_att), om.reshape(ms, d_mem), tm=ms)
    y_sample = ffn(x1, tm=ms).reshape(n_dec, t_dec, d)
    conv_state_sample = u_ext[:, t_dec:][None]
    k_sample = k.reshape(1, n_dec, t_dec, N_HEADS, HEAD_DIM)
    v_sample = v.reshape(1, n_dec, t_dec, N_HEADS, HEAD_DIM)
    idx_k_sample = ki.reshape(1, n_dec, t_dec, IDX_DIM)

    return (y_prompt, y_sample, conv_state_prompt, k_prompt, v_prompt, idx_k_prompt, mem_k_prompt, mem_v_prompt,
            conv_state_sample, k_sample, v_sample, idx_k_sample)
```

```python
import functools

import jax
import jax.numpy as jnp
import numpy as np
from jax import lax
from jax.experimental import pallas as pl
from jax.experimental.pallas import tpu as pltpu

EPS = 1e-6
NEG = -1e30
CONV_WIDTH = 31
N_HEADS = 8
HEAD_DIM = 64
N_IDX_HEADS = 8
IDX_DIM = 64
MEM_HEADS = 4
MEM_HEAD_DIM = 128
TOPK_MAX = 256
PAGE_SIZE = 128
LANES = 128
SUBLANES = 8
VMEM_LIMIT = 56 * 1024 * 1024
INT_MIN = -(2 ** 31)

BF16 = jnp.bfloat16
F32 = jnp.float32
I32 = jnp.int32


def _cparams(n_axes):
    return pltpu.CompilerParams(dimension_semantics=("arbitrary",) * n_axes, vmem_limit_bytes=VMEM_LIMIT)


def _const_spec(shape):
    zeros = (0,) * len(shape)
    return pl.BlockSpec(shape, lambda *_: zeros)


def _rmsnorm(x, g):
    return x * lax.rsqrt(jnp.mean(x * x, axis=-1, keepdims=True) + EPS) * g


def _dot(a, b):
    return jnp.dot(a, b, preferred_element_type=F32)


def _dot_nt(a, b):
    return lax.dot_general(a, b, (((1,), (1,)), ((), ())), preferred_element_type=F32)


WI_ROWS = 16


def _glu_qm(h, w_ref, u_ref, qm_ref, d_conv, d_mem):
    o = 0
    glu = _dot(h, w_ref[:, o:o + 2 * d_conv]); o += 2 * d_conv
    u_ref[...] = glu[:, :d_conv] * jax.nn.sigmoid(glu[:, d_conv:])
    qm_ref[...] = _dot(h, w_ref[:, o:o + d_mem]).astype(BF16); o += d_mem
    return o


def _inproj_prompt_kernel(x_ref, g_ref, w_ref, wt_ref, u_ref, qm_ref, h_ref, kib_ref, kb_ref,
                          kt_ref, vt_ref, vtb_ref, kit_ref, qit_ref, wit_ref, qt_ref, *, d_conv, d_att, d_idx, d_mem):
    h = _rmsnorm(x_ref[0], g_ref[...]).astype(BF16)
    h_ref[...] = h
    o = _glu_qm(h, w_ref, u_ref, qm_ref, d_conv, d_mem)
    kib_ref[...] = _dot(h, w_ref[:, o:o + LANES])[:, :IDX_DIM].astype(BF16)
    r = 0
    kt = _dot_nt(wt_ref[r:r + d_att, :], h); r += d_att
    kt_ref[0] = kt
    kb_ref[...] = kt.T.astype(BF16)
    vt = _dot_nt(wt_ref[r:r + d_att, :], h); r += d_att
    vt_ref[0] = vt
    vtb_ref[0] = vt.astype(BF16)
    kit_ref[0] = _dot_nt(wt_ref[r:r + IDX_DIM, :], h); r += IDX_DIM
    wit_ref[0] = _dot_nt(wt_ref[r:r + WI_ROWS, :], h)[:N_IDX_HEADS] * (N_IDX_HEADS ** -0.5); r += WI_ROWS
    qit_ref[0] = (_dot_nt(wt_ref[r:r + d_idx, :], h) * (IDX_DIM ** -0.5)).astype(BF16); r += d_idx
    qt_ref[0] = (_dot_nt(wt_ref[r:r + d_att, :], h) * (HEAD_DIM ** -0.5)).astype(BF16)


def _inproj_prompt(x, g, w, wt, *, tm, d_conv, d_att, d_idx, d_mem):
    bsz, s, d = x.shape
    nt = s // tm
    row = lambda n: pl.BlockSpec((tm, n), lambda b, i: (b * nt + i, 0))
    col = lambda n: pl.BlockSpec((1, n, tm), lambda b, i: (b, 0, i))
    outs = [(d_conv, F32), (d_mem, BF16), (d, BF16), (IDX_DIM, BF16), (d_att, BF16)]
    outs_t = [(d_att, F32), (d_att, F32), (d_att, BF16), (IDX_DIM, F32), (d_idx, BF16), (N_IDX_HEADS, F32),
              (d_att, BF16)]
    return pl.pallas_call(
        functools.partial(_inproj_prompt_kernel, d_conv=d_conv, d_att=d_att, d_idx=d_idx, d_mem=d_mem),
        grid=(bsz, nt),
        in_specs=[pl.BlockSpec((1, tm, d), lambda b, i: (b, i, 0)), _const_spec((1, d)), _const_spec(w.shape),
                  _const_spec(wt.shape)],
        out_specs=[row(n) for n, _ in outs] + [col(n) for n, _ in outs_t],
        out_shape=[jax.ShapeDtypeStruct((bsz * s, n), dt) for n, dt in outs]
                  + [jax.ShapeDtypeStruct((bsz, n, s), dt) for n, dt in outs_t],
        compiler_params=_cparams(2),
        name="inproj_prompt",
    )(x, g, w, wt)


def _inproj_sample_kernel(x_ref, g_ref, w_ref, u_ref, q_ref, qm_ref, h_ref, qi_ref, wi_ref, k_ref, v_ref, ki_ref,
                          *, d_conv, d_att, d_idx, d_mem):
    h = _rmsnorm(x_ref[...], g_ref[...]).astype(BF16)
    h_ref[...] = h
    o = _glu_qm(h, w_ref, u_ref, qm_ref, d_conv, d_mem)
    ki_ref[...] = _dot(h, w_ref[:, o:o + LANES])[:, :IDX_DIM]; o += LANES
    q_ref[...] = (_dot(h, w_ref[:, o:o + d_att]) * (HEAD_DIM ** -0.5)).astype(BF16); o += d_att
    qi_ref[...] = (_dot(h, w_ref[:, o:o + d_idx]) * (IDX_DIM ** -0.5)).astype(BF16); o += d_idx
    wi_ref[...] = _dot(h, w_ref[:, o:o + LANES])[:, :N_IDX_HEADS] * (N_IDX_HEADS ** -0.5); o += LANES
    k_ref[...] = _dot(h, w_ref[:, o:o + d_att]); o += d_att
    v_ref[...] = _dot(h, w_ref[:, o:o + d_att])


def _inproj_sample(x, g, w, *, d_conv, d_att, d_idx, d_mem):
    m, d = x.shape
    outs = [(d_conv, F32), (d_att, BF16), (d_mem, BF16), (d, BF16), (d_idx, BF16), (N_IDX_HEADS, F32),
            (d_att, F32), (d_att, F32), (IDX_DIM, F32)]
    return pl.pallas_call(
        functools.partial(_inproj_sample_kernel, d_conv=d_conv, d_att=d_att, d_idx=d_idx, d_mem=d_mem),
        grid=(1,),
        in_specs=[_const_spec((m, d)), _const_spec((1, d)), _const_spec(w.shape)],
        out_specs=[_const_spec((m, n)) for n, _ in outs],
        out_shape=[jax.ShapeDtypeStruct((m, n), dt) for n, dt in outs],
        compiler_params=_cparams(1),
        name="inproj_sample",
    )(x, g, w)


def _conv_rows(ext_ref, start, rows, w_ref, b_ref, g_ref, bl_ref):
    acc = w_ref[0:1, :] * ext_ref[pl.ds(start, rows), :]
    for j in range(1, CONV_WIDTH):
        acc = acc + w_ref[j:j + 1, :] * ext_ref[pl.ds(start + j, rows), :]
    c = acc + b_ref[...]
    mu = jnp.mean(c, axis=-1, keepdims=True)
    xc = c - mu
    y = xc * lax.rsqrt(jnp.mean(xc * xc, axis=-1, keepdims=True) + EPS) * g_ref[...] + bl_ref[...]
    return (y * jax.nn.sigmoid(y)).astype(BF16)


def _conv_prompt_kernel(prev_ref, cur_ref, w_ref, b_ref, g_ref, bl_ref, o_ref, ext_ref, *, halo, chunk):
    t = cur_ref.shape[1]

    @pl.when(pl.program_id(1) == 0)
    def _():
        ext_ref[0:halo, :] = jnp.zeros((halo, ext_ref.shape[1]), F32)

    @pl.when(pl.program_id(1) > 0)
    def _():
        ext_ref[0:halo, :] = prev_ref[0]

    ext_ref[halo:halo + t, :] = cur_ref[0]
    first = halo - (CONV_WIDTH - 1)
    for c in range(t // chunk):
        o_ref[0, c * chunk:(c + 1) * chunk, :] = _conv_rows(ext_ref, first + c * chunk, chunk, w_ref, b_ref, g_ref,
                                                            bl_ref)


def _conv_prompt(u, w, b, g, bl, *, t=512, halo=32, chunk=64):
    bsz, s, dc = u.shape
    assert s % t == 0 and t % halo == 0 and halo >= CONV_WIDTH - 1 and t % chunk == 0
    r = t // halo
    return pl.pallas_call(
        functools.partial(_conv_prompt_kernel, halo=halo, chunk=chunk),
        grid=(bsz, s // t),
        in_specs=[pl.BlockSpec((1, halo, dc), lambda bi, i: (bi, jnp.maximum(i * r - 1, 0), 0)),
                  pl.BlockSpec((1, t, dc), lambda bi, i: (bi, i, 0)),
                  _const_spec(w.shape), _const_spec((1, dc)), _const_spec((1, dc)), _const_spec((1, dc))],
        out_specs=pl.BlockSpec((1, t, dc), lambda bi, i: (bi, i, 0)),
        out_shape=jax.ShapeDtypeStruct((bsz, s, dc), BF16),
        scratch_shapes=[pltpu.VMEM((halo + t, dc), F32)],
        compiler_params=_cparams(2),
        name="conv_prompt",
    )(u, u, w, b, g, bl)


def _conv_sample_kernel(ext_ref, w_ref, b_ref, g_ref, bl_ref, o_ref):
    t = o_ref.shape[1]
    o_ref[0] = _conv_rows(ext_ref.at[0], 0, t, w_ref, b_ref, g_ref, bl_ref)


def _conv_sample(u_ext, w, b, g, bl):
    n, te, dc = u_ext.shape
    t = te - (CONV_WIDTH - 1)
    return pl.pallas_call(
        _conv_sample_kernel,
        grid=(n,),
        in_specs=[pl.BlockSpec((1, te, dc), lambda i: (i, 0, 0)),
                  _const_spec(w.shape), _const_spec((1, dc)), _const_spec((1, dc)), _const_spec((1, dc))],
        out_specs=pl.BlockSpec((1, t, dc), lambda i: (i, 0, 0)),
        out_shape=jax.ShapeDtypeStruct((n, t, dc), BF16),
        compiler_params=_cparams(1),
        name="conv_sample",
    )(u_ext, w, b, g, bl)


def _memkv_kernel(x_ref, g_ref, w_ref, k_ref, v_ref):
    h = _rmsnorm(x_ref[...], g_ref[...]).astype(BF16)
    kv = _dot(h, w_ref[...])
    d = k_ref.shape[1]
    k_ref[...] = kv[:, :d]
    v_ref[...] = kv[:, d:]


def _memkv(x, g, w, *, tm=256):
    m, d = x.shape
    dm = w.shape[1] // 2
    return pl.pallas_call(
        _memkv_kernel,
        grid=(m // tm,),
        in_specs=[pl.BlockSpec((tm, d), lambda i: (i, 0)), _const_spec((1, d)), _const_spec(w.shape)],
        out_specs=[pl.BlockSpec((tm, dm), lambda i: (i, 0))] * 2,
        out_shape=[jax.ShapeDtypeStruct((m, dm), F32)] * 2,
        compiler_params=_cparams(1),
        name="memkv",
    )(x, g, w)


def _memattn_kernel(q_ref, k_ref, v_ref, o_ref):
    for h in range(MEM_HEADS):
        sl = slice(h * MEM_HEAD_DIM, (h + 1) * MEM_HEAD_DIM)
        s = _dot_nt(q_ref[0, :, sl], k_ref[0, :, sl].astype(BF16)) * (MEM_HEAD_DIM ** -0.5)
        e = jnp.exp(s - jnp.max(s, axis=-1, keepdims=True))
        p = e / jnp.sum(e, axis=-1, keepdims=True)
        o_ref[0, :, sl] = _dot(p.astype(BF16), v_ref[0, :, sl].astype(BF16)).astype(BF16)


def _memattn(q, k, v, *, tq):
    n, t, dm = q.shape
    nm = k.shape[1]
    return pl.pallas_call(
        _memattn_kernel,
        grid=(n, t // tq),
        in_specs=[pl.BlockSpec((1, tq, dm), lambda i, j: (i, j, 0)),
                  pl.BlockSpec((1, nm, dm), lambda i, j: (i, 0, 0)),
                  pl.BlockSpec((1, nm, dm), lambda i, j: (i, 0, 0))],
        out_specs=pl.BlockSpec((1, tq, dm), lambda i, j: (i, j, 0)),
        out_shape=jax.ShapeDtypeStruct((n, t, dm), BF16),
        compiler_params=_cparams(2),
        name="memattn",
    )(q, k, v)


def _ordinal_to_float(o):
    return pltpu.bitcast(jnp.where(o >= 0, o, (-o) | INT_MIN), F32)


def _select_mask(score_ref, key_axis, n_tiles, n_sel, outside, idx_bits, valid_fn, emit_fn):
    _, ta, tb = score_ref.shape
    tile = (ta, tb)
    unit = LANES if key_axis == 1 else 4 * SUBLANES
    tk = tile[key_axis]
    fold = tk // unit
    shp = (ta, LANES) if key_axis == 1 else (unit, tb)
    k_f = jnp.float32(n_sel)
    out_f = jnp.float32(1.0) * outside

    def key_fold(x):
        if key_axis == 0:
            return jnp.sum(x.reshape(fold, unit, tb), axis=0)
        acc = x[:, :LANES]
        for i in range(1, fold):
            acc = acc + x[:, i * LANES:(i + 1) * LANES]
        return acc

    def rep(x):
        return x if fold == 1 else pltpu.repeat(x, fold, axis=key_axis)

    def row_total(part):
        return jnp.broadcast_to(jnp.sum(part, axis=key_axis, keepdims=True), shp)

    def key_index(j):
        return j * tk + lax.broadcasted_iota(I32, tile, key_axis)

    def count_ge(thr):
        thr_t = rep(thr)

        def body(j, part):
            return part + key_fold(jnp.where(score_ref[j] >= thr_t, 1.0, 0.0))

        part = lax.fori_loop(0, n_tiles, body, jnp.zeros(shp, F32))
        return row_total(part) + jnp.where(thr <= NEG, out_f, 0.0)

    def bit_step(it, carry):
        u, cnt_u = carry
        bit = lax.shift_left(jnp.int32(1), 31 - it)
        cand = u | bit
        cnt = count_ge(_ordinal_to_float(cand ^ INT_MIN))
        ok = cnt >= k_f
        return jnp.where(ok, cand, u), jnp.where(ok, cnt, cnt_u)

    total = jnp.float32(1.0) * (n_tiles * tk) + out_f
    u, cnt_ge = lax.fori_loop(0, 32, bit_step, (jnp.zeros(shp, I32), jnp.broadcast_to(total, shp)))
    thr = _ordinal_to_float(u ^ INT_MIN)
    thr_t = rep(thr)

    surplus = jnp.max(jnp.where(cnt_ge > k_f, 1.0, 0.0)) > 0.0

    def tie_search():
        def count_gt_body(j, part):
            return part + key_fold(jnp.where(score_ref[j] > thr_t, 1.0, 0.0))

        cnt_gt = row_total(lax.fori_loop(0, n_tiles, count_gt_body, jnp.zeros(shp, F32)))
        cnt_gt = cnt_gt + jnp.where(thr < NEG, out_f, 0.0)
        in_tiles = n_tiles * tk

        def idx_step(it, v):
            bit = lax.shift_left(jnp.int32(1), idx_bits - 1 - it)
            cand = v | bit
            cand_t = rep(cand)

            def body(j, part):
                hit = jnp.where(score_ref[j] == thr_t, jnp.where(key_index(j) < cand_t, 1.0, 0.0), 0.0)
                return part + key_fold(hit)

            ties_below = row_total(lax.fori_loop(0, n_tiles, body, jnp.zeros(shp, F32)))
            out_below = jnp.clip(cand - in_tiles, 0, outside).astype(F32)
            ties_below = ties_below + jnp.where(thr == NEG, out_below, 0.0)
            return jnp.where(cnt_gt + ties_below < k_f, cand, v)

        return lax.fori_loop(0, idx_bits, idx_step, jnp.zeros(shp, I32))

    jmax = lax.cond(surplus, tie_search, lambda: jnp.full(shp, 2 ** idx_bits - 1, I32))
    jmax_t = rep(jmax)

    def emit(j, _):
        sc = score_ref[j]
        sel = jnp.where(sc > thr_t, 1.0, jnp.where(sc == thr_t, jnp.where(key_index(j) <= jmax_t, 1.0, 0.0), 0.0))
        emit_fn(j, jnp.where(valid_fn(j, tile), sel, 0.0))
        return 0

    lax.fori_loop(0, n_tiles, emit, 0)


def _sel_prompt_kernel(kib_ref, qit_ref, wit_ref, mask_ref, sc_ref, *, n_sel):
    tq = qit_ref.shape[2]
    s = kib_ref.shape[1]
    n_all = s // tq
    qb = pl.program_id(1)
    n_tiles = qb + 1
    qpos = qb * tq + lax.broadcasted_iota(I32, (tq, tq), 1)

    def score_tile(j):
        ki_t = kib_ref[0, pl.ds(pl.multiple_of(j * tq, tq), tq), :]
        acc = jnp.zeros((tq, tq), F32)
        for h in range(N_IDX_HEADS):
            d = _dot(ki_t, qit_ref[0, h * IDX_DIM:(h + 1) * IDX_DIM, :])
            acc = acc + wit_ref[0, h:h + 1, :] * jnp.maximum(d, 0.0)
        return acc

    def below(j, _):
        sc_ref[j] = score_tile(j)
        return 0

    lax.fori_loop(0, qb, below, 0)
    kpos = qb * tq + lax.broadcasted_iota(I32, (tq, tq), 0)
    sc_ref[qb] = jnp.where(kpos <= qpos, score_tile(qb), NEG)

    def valid(j, shape):
        return j * tq + lax.broadcasted_iota(I32, shape, 0) <= qpos

    def emit(j, sel):
        mask_ref[0, 0, j] = sel.astype(BF16)

    idx_bits = max(1, int(s - 1).bit_length())
    _select_mask(sc_ref, 0, n_tiles, n_sel, (n_all - n_tiles) * tq, idx_bits, valid, emit)

    def clear(j, _):
        mask_ref[0, 0, j] = jnp.zeros((tq, tq), BF16)
        return 0

    lax.fori_loop(n_tiles, n_all, clear, 0)


def _sel_prompt(kib, qitb, wit, *, tq, n_sel):
    bsz, s, _ = kib.shape
    nq = s // tq
    return pl.pallas_call(
        functools.partial(_sel_prompt_kernel, n_sel=n_sel),
        grid=(bsz, nq),
        in_specs=[pl.BlockSpec((1, s, kib.shape[2]), lambda b, i: (b, 0, 0)),
                  pl.BlockSpec((1, qitb.shape[1], tq), lambda b, i: (b, 0, i)),
                  pl.BlockSpec((1, wit.shape[1], tq), lambda b, i: (b, 0, i))],
        out_specs=pl.BlockSpec((1, 1, nq, tq, tq), lambda b, i: (b, i, 0, 0, 0)),
        out_shape=jax.ShapeDtypeStruct((bsz, nq, nq, tq, tq), BF16),
        scratch_shapes=[pltpu.VMEM((nq, tq, tq), F32)],
        compiler_params=_cparams(2),
        name="sel_prompt",
    )(kib, qitb, wit)


def _attn_prompt_kernel(qt_ref, k_ref, vt_ref, mask_ref, o_ref, q2_ref, s_ref, mx_ref, l_ref, out_ref):
    tq = qt_ref.shape[2]
    n_tiles = pl.program_id(1) + 1
    pair = 2 * HEAD_DIM
    part = 4 * SUBLANES
    fold = tq // part
    in_pair = lax.broadcasted_iota(I32, (pair, tq), 0)
    for hp in range(N_HEADS // 2):
        qp = qt_ref[0, hp * pair:(hp + 1) * pair, :]
        q2_ref[2 * hp] = jnp.where(in_pair < HEAD_DIM, qp, jnp.zeros_like(qp))
        q2_ref[2 * hp + 1] = jnp.where(in_pair >= HEAD_DIM, qp, jnp.zeros_like(qp))

    def key_rows(j):
        return pl.ds(pl.multiple_of(j * tq, tq), tq)

    group = s_ref.shape[0]
    for h0 in range(0, N_HEADS, group):
        mx_ref[...] = jnp.full(mx_ref.shape, NEG, F32)
        l_ref[...] = jnp.zeros(l_ref.shape, F32)
        out_ref[h0 * HEAD_DIM:(h0 + group) * HEAD_DIM, :] = jnp.zeros((group * HEAD_DIM, tq), F32)

        def scores(j, _, h0=h0):
            sel = mask_ref[0, 0, j].astype(F32) > 0.0
            for g in range(group):
                h = h0 + g
                k_pair = k_ref[0, key_rows(j), (h // 2) * pair:(h // 2 + 1) * pair]
                s = jnp.where(sel, _dot(k_pair, q2_ref[h]), NEG)
                s_ref[g, j] = s
                mx_ref[g] = jnp.maximum(mx_ref[g], jnp.max(s.reshape(fold, part, tq), axis=0))
            return 0

        lax.fori_loop(0, n_tiles, scores, 0)
        m = [jnp.max(mx_ref[g], axis=0, keepdims=True) for g in range(group)]

        def weigh(j, _, h0=h0, m=m):
            for g in range(group):
                rows = slice((h0 + g) * HEAD_DIM, (h0 + g + 1) * HEAD_DIM)
                p = jnp.exp(s_ref[g, j] - m[g])
                l_ref[g] = l_ref[g] + jnp.sum(p.reshape(fold, part, tq), axis=0)
                out_ref[rows, :] = out_ref[rows, :] + _dot(vt_ref[0, rows, key_rows(j)], p.astype(BF16))
            return 0

        lax.fori_loop(0, n_tiles, weigh, 0)
        for g in range(group):
            rows = slice((h0 + g) * HEAD_DIM, (h0 + g + 1) * HEAD_DIM)
            out_ref[rows, :] = out_ref[rows, :] / jnp.sum(l_ref[g], axis=0, keepdims=True)
    o_ref[0] = out_ref[...].T.astype(BF16)


def _attn_prompt(qtb, kb, vtb, mask, *, tq, group=4):
    bsz, da, s = qtb.shape
    nq = s // tq
    part = 4 * SUBLANES
    return pl.pallas_call(
        _attn_prompt_kernel,
        grid=(bsz, nq),
        in_specs=[pl.BlockSpec((1, da, tq), lambda b, i: (b, 0, i)),
                  pl.BlockSpec((1, s, da), lambda b, i: (b, 0, 0)),
                  pl.BlockSpec((1, da, s), lambda b, i: (b, 0, 0)),
                  pl.BlockSpec((1, 1, nq, tq, tq), lambda b, i: (b, i, 0, 0, 0))],
        out_specs=pl.BlockSpec((1, tq, da), lambda b, i: (b, i, 0)),
        out_shape=jax.ShapeDtypeStruct((bsz, s, da), BF16),
        scratch_shapes=[pltpu.VMEM((N_HEADS, 2 * HEAD_DIM, tq), BF16), pltpu.VMEM((group, nq, tq, tq), F32),
                        pltpu.VMEM((group, part, tq), F32), pltpu.VMEM((group, part, tq), F32),
                        pltpu.VMEM((da, tq), F32)],
        compiler_params=_cparams(2),
        name="attn_prompt",
    )(qtb, kb, vtb, mask)


def _score_sample_kernel(pt_ref, qi_ref, wi_ref, kin_ref, *rest, pages):
    page_refs = rest[:pages]
    past_ref, new_ref = rest[pages:]
    t = new_ref.shape[1]
    qi = qi_ref[0]
    wi = wi_ref[0]

    def scores(dots):
        w = wi * jnp.maximum(dots, 0.0)
        acc = w[0:t]
        for h in range(1, N_IDX_HEADS):
            acc = acc + w[h * t:(h + 1) * t]
        return acc

    for i in range(pages):
        past_ref[0, :, i * PAGE_SIZE:(i + 1) * PAGE_SIZE] = scores(_dot(qi, page_refs[i][0].astype(BF16)))

    @pl.when(pl.program_id(1) == 0)
    def _():
        ext = jnp.concatenate([kin_ref[0], jnp.zeros((LANES - t, IDX_DIM), F32)], axis=0).astype(BF16)
        sc = scores(_dot_nt(qi, ext))
        qpos = lax.broadcasted_iota(I32, (t, LANES), 0)
        kpos = lax.broadcasted_iota(I32, (t, LANES), 1)
        new_ref[0] = jnp.where(kpos < t, jnp.where(kpos <= qpos, sc, NEG), -jnp.inf)


def _score_sample(page_table, qi_hm, wi_hm, ki_new, cache_idx_k, *, pages):
    n, n_pages = page_table.shape
    t = ki_new.shape[1]
    ht = qi_hm.shape[1]
    steps = n_pages // pages

    def page_spec(i):
        return pl.BlockSpec((1, IDX_DIM, PAGE_SIZE), lambda b, j, pt: (pt[b, j * pages + i], 0, 0))

    grid_spec = pltpu.PrefetchScalarGridSpec(
        num_scalar_prefetch=1,
        grid=(n, steps),
        in_specs=[pl.BlockSpec((1, ht, IDX_DIM), lambda b, j, pt: (b, 0, 0)),
                  pl.BlockSpec((1, ht, 1), lambda b, j, pt: (b, 0, 0)),
                  pl.BlockSpec((1, t, IDX_DIM), lambda b, j, pt: (b, 0, 0))]
                 + [page_spec(i) for i in range(pages)],
        out_specs=[pl.BlockSpec((1, t, pages * PAGE_SIZE), lambda b, j, pt: (b, 0, j)),
                   pl.BlockSpec((1, t, LANES), lambda b, j, pt: (b, 0, 0))],
    )
    return pl.pallas_call(
        functools.partial(_score_sample_kernel, pages=pages),
        grid_spec=grid_spec,
        out_shape=[jax.ShapeDtypeStruct((n, t, n_pages * PAGE_SIZE), F32),
                   jax.ShapeDtypeStruct((n, t, LANES), F32)],
        compiler_params=_cparams(2),
        name="score_sample",
    )(page_table, qi_hm, wi_hm, ki_new, *([cache_idx_k] * pages))


def _sel_sample_kernel(past_ref, new_ref, mpast_ref, mnew_ref, key_ref, mask_ref, *, n_sel, t, tile):
    rows = past_ref.shape[0]
    n_past = past_ref.shape[1] // tile
    for j in range(n_past):
        key_ref[j] = past_ref[:, j * tile:(j + 1) * tile]
    pad = jnp.full((rows, tile - LANES), -jnp.inf, F32)
    key_ref[n_past] = jnp.concatenate([new_ref[...], pad], axis=1)
    qpos = lax.rem(lax.broadcasted_iota(I32, (rows, tile), 0), t)

    def valid(j, shape):
        kpos = lax.broadcasted_iota(I32, shape, 1)
        return kpos <= qpos + jnp.minimum(n_past - j, 1) * tile

    def emit(j, sel):
        mask_ref[j] = sel

    idx_bits = int((n_past + 1) * tile - 1).bit_length()
    _select_mask(key_ref, 1, n_past + 1, n_sel, 0, idx_bits, valid, emit)
    for j in range(n_past):
        mpast_ref[:, j * tile:(j + 1) * tile] = mask_ref[j]
    mnew_ref[...] = mask_ref[n_past][:, :LANES]


def _sel_sample(key_past, key_new, *, n_sel, t, rows=64, tile=2048):
    r, n_keys = key_past.shape
    n_t = n_keys // tile + 1
    return pl.pallas_call(
        functools.partial(_sel_sample_kernel, n_sel=n_sel, t=t, tile=tile),
        grid=(r // rows,),
        in_specs=[pl.BlockSpec((rows, n_keys), lambda i: (i, 0)), pl.BlockSpec((rows, LANES), lambda i: (i, 0))],
        out_specs=[pl.BlockSpec((rows, n_keys), lambda i: (i, 0)), pl.BlockSpec((rows, LANES), lambda i: (i, 0))],
        out_shape=[jax.ShapeDtypeStruct((r, n_keys), F32), jax.ShapeDtypeStruct((r, LANES), F32)],
        scratch_shapes=[pltpu.VMEM((n_t, rows, tile), F32), pltpu.VMEM((n_t, rows, tile), F32)],
        compiler_params=_cparams(1),
        name="sel_sample",
    )(key_past, key_new)


def _attn_sample_kernel(pt_ref, q_ref, kn_ref, vn_ref, mpast_ref, mnew_ref, *rest, pages):
    k_refs = rest[:pages]
    v_refs = rest[pages:2 * pages]
    o_ref, qbd_ref, m_ref, l_ref, acc_ref = rest[2 * pages:]
    t, da = q_ref.shape[1], q_ref.shape[2]
    ht = N_HEADS * t
    j = pl.program_id(1)

    @pl.when(j == 0)
    def _():
        q_rep = jnp.concatenate([q_ref[0].astype(F32)] * N_HEADS, axis=0)
        head_of_row = lax.broadcasted_iota(I32, (ht, da), 0) // t
        head_of_col = lax.broadcasted_iota(I32, (ht, da), 1) // HEAD_DIM
        qbd_ref[...] = jnp.where(head_of_row == head_of_col, q_rep, 0.0).astype(BF16)
        m_ref[...] = jnp.full(m_ref.shape, NEG, F32)
        l_ref[...] = jnp.zeros(l_ref.shape, F32)
        acc_ref[...] = jnp.zeros(acc_ref.shape, F32)

    def attend(k_pages, v_pages, mask_t, feature_major):
        qbd = qbd_ref[...]
        qk = [(_dot(qbd, kp) if feature_major else _dot_nt(qbd, kp)) for kp in k_pages]
        sel = jnp.concatenate([mask_t] * N_HEADS, axis=0) > 0.0
        s = jnp.where(sel, qk[0] if len(qk) == 1 else jnp.concatenate(qk, axis=1), NEG)
        m_old = m_ref[...]
        m_new = jnp.maximum(m_old, jnp.max(s, axis=-1, keepdims=True))
        alpha = jnp.exp(m_old - m_new)
        p = jnp.where(sel, jnp.exp(s - m_new), 0.0)
        l_ref[...] = alpha * l_ref[...] + jnp.sum(p, axis=-1, keepdims=True)
        pb = p.astype(BF16)
        pv = None
        for i, vp in enumerate(v_pages):
            p_i = pb[:, i * PAGE_SIZE:(i + 1) * PAGE_SIZE]
            term = _dot_nt(p_i, vp) if feature_major else _dot(p_i, vp)
            pv = term if pv is None else pv + term
        acc_ref[...] = alpha * acc_ref[...] + pv
        m_ref[...] = m_new

    attend([r[0].astype(BF16) for r in k_refs], [r[0].astype(BF16) for r in v_refs], mpast_ref[0], True)

    @pl.when(j == pl.num_programs(1) - 1)
    def _():
        zeros = jnp.zeros((PAGE_SIZE - t, da), F32)
        attend([jnp.concatenate([kn_ref[0], zeros], axis=0).astype(BF16)],
               [jnp.concatenate([vn_ref[0], zeros], axis=0).astype(BF16)], mnew_ref[0], False)
        out = acc_ref[...] / l_ref[...]
        o_ref[0] = jnp.concatenate(
            [out[h * t:(h + 1) * t, h * HEAD_DIM:(h + 1) * HEAD_DIM] for h in range(N_HEADS)], axis=1).astype(BF16)


def _attn_sample(page_table, q, k_new, v_new, mask_past, mask_new, cache_k, cache_v, *, pages):
    n, n_pages = page_table.shape
    t, da = q.shape[1], q.shape[2]
    steps = n_pages // pages
    ht = N_HEADS * t

    def page_spec(i):
        return pl.BlockSpec((1, da, PAGE_SIZE), lambda b, j, pt: (pt[b, j * pages + i], 0, 0))

    per_seq = lambda shape: pl.BlockSpec((1,) + shape, lambda b, j, pt: (b, 0, 0))
    grid_spec = pltpu.PrefetchScalarGridSpec(
        num_scalar_prefetch=1,
        grid=(n, steps),
        in_specs=[per_seq((t, da)), per_seq((t, da)), per_seq((t, da)),
                  pl.BlockSpec((1, t, pages * PAGE_SIZE), lambda b, j, pt: (b, 0, j)), per_seq((t, LANES))]
                 + [page_spec(i) for i in range(pages)] + [page_spec(i) for i in range(pages)],
        out_specs=per_seq((t, da)),
        scratch_shapes=[pltpu.VMEM((ht, da), BF16), pltpu.VMEM((ht, 1), F32), pltpu.VMEM((ht, 1), F32),
                        pltpu.VMEM((ht, da), F32)],
    )
    return pl.pallas_call(
        functools.partial(_attn_sample_kernel, pages=pages),
        grid_spec=grid_spec,
        out_shape=jax.ShapeDtypeStruct((n, t, da), BF16),
        compiler_params=_cparams(2),
        name="attn_sample",
    )(page_table, q, k_new, v_new, mask_past, mask_new, *([cache_k] * pages), *([cache_v] * pages))


def _merge_kernel(x_ref, h_ref, c_ref, a_ref, m_ref, wg_ref, wc_ref, wa_ref, wm_ref, wo_ref, o_ref):
    d = x_ref.shape[1]
    gates = jax.nn.sigmoid(_dot(h_ref[...], wg_ref[...]))
    mix = (gates[:, :d] * _dot(c_ref[...], wc_ref[...])
           + gates[:, d:2 * d] * _dot(a_ref[...], wa_ref[...])
           + gates[:, 2 * d:] * _dot(m_ref[...], wm_ref[...]))
    o_ref[...] = x_ref[...] + _dot(mix.astype(BF16), wo_ref[...])


def _merge(x, h, c, a, mo, wg, wc, wa, wm, wo, *, tm):
    m, d = x.shape
    row = lambda n: pl.BlockSpec((tm, n), lambda i: (i, 0))
    return pl.pallas_call(
        _merge_kernel,
        grid=(m // tm,),
        in_specs=[row(d), row(d), row(c.shape[1]), row(a.shape[1]), row(mo.shape[1]),
                  _const_spec(wg.shape), _const_spec(wc.shape), _const_spec(wa.shape), _const_spec(wm.shape),
                  _const_spec(wo.shape)],
        out_specs=row(d),
        out_shape=jax.ShapeDtypeStruct((m, d), F32),
        compiler_params=_cparams(1),
        name="merge",
    )(x, h, c, a, mo, wg, wc, wa, wm, wo)


def _ffn_kernel(x_ref, g_ref, wi_ref, wo_ref, gf_ref, o_ref):
    x = x_ref[...]
    dff = wo_ref.shape[0]
    hn = _rmsnorm(x, g_ref[...]).astype(BF16)
    a = _dot(hn, wi_ref[:, :dff])
    b = _dot(hn, wi_ref[:, dff:])
    y = x + _dot((a * jax.nn.sigmoid(a) * b).astype(BF16), wo_ref[...])
    o_ref[...] = _rmsnorm(y, gf_ref[...])


def _ffn(x, g, w_in, w_out, g_final, *, tm):
    m, d = x.shape
    return pl.pallas_call(
        _ffn_kernel,
        grid=(m // tm,),
        in_specs=[pl.BlockSpec((tm, d), lambda i: (i, 0)), _const_spec((1, d)), _const_spec(w_in.shape),
                  _const_spec(w_out.shape), _const_spec((1, d))],
        out_specs=pl.BlockSpec((tm, d), lambda i: (i, 0)),
        out_shape=jax.ShapeDtypeStruct((m, d), F32),
        compiler_params=_cparams(1),
        name="ffn",
    )(x, g, w_in, w_out, g_final)


def kernel(x_prompt, x_sample, mem_prompt, cache_conv, cache_k, cache_v, cache_idx_k, cache_mem_k, cache_mem_v,
           page_table, g_mix, w_in, w_conv_dw, b_conv_dw, g_conv_ln, b_conv_ln, w_conv_out, w_att_out, g_mem,
           w_mem_kv, w_mem_out, w_out, g_ffn, w_ffn_in, w_ffn_out, g_final):
    depth = g_mix.shape[0]
    assert depth == 1
    bsz, seq, d = x_prompt.shape
    n_dec, t_dec, _ = x_sample.shape
    n_mem = mem_prompt.shape[1]
    d_conv = w_conv_dw.shape[2]
    d_att = N_HEADS * HEAD_DIM
    d_idx = N_IDX_HEADS * IDX_DIM
    d_mem = MEM_HEADS * MEM_HEAD_DIM
    n_phys = cache_k.shape[1]
    past = page_table.shape[1] * PAGE_SIZE
    l = 0

    splits = [2 * d_conv, d_att, d_att, d_att, d_idx, IDX_DIM, N_IDX_HEADS, d_mem, 3 * d]
    offs = np.concatenate([[0], np.cumsum(splits)])
    w = w_in[l]
    col = lambda i: w[:, offs[i]:offs[i + 1]]
    pad_cols = lambda a, n: jnp.pad(a, ((0, 0), (0, n - a.shape[1])))
    glu, wq, wk, wv, wqi, wki, wwi, wqm, wgates = (col(i) for i in range(9))
    shared = [glu, wqm, pad_cols(wki, LANES)]
    w_prompt = jnp.concatenate(shared, axis=1).astype(BF16)
    w_prompt_t = jnp.concatenate([wk, wv, wki, pad_cols(wwi, WI_ROWS), wqi, wq], axis=1).T.astype(BF16)
    w_sample = jnp.concatenate(shared + [wq, wqi, pad_cols(wwi, LANES), wk, wv], axis=1).astype(BF16)
    w_gates = wgates.astype(BF16)
    bf = lambda a: a[l].astype(BF16)
    row2 = lambda a: a.reshape(1, -1)
    wdw, bdw, gln, bln = w_conv_dw[l], row2(b_conv_dw[l]), row2(g_conv_ln[l]), row2(b_conv_ln[l])
    dims = dict(d_conv=d_conv, d_att=d_att, d_idx=d_idx, d_mem=d_mem)
    merge = functools.partial(_merge, wg=w_gates, wc=bf(w_conv_out), wa=bf(w_att_out), wm=bf(w_mem_out),
                              wo=bf(w_out))
    ffn = functools.partial(_ffn, g=row2(g_ffn[l]), w_in=bf(w_ffn_in), w_out=bf(w_ffn_out),
                            g_final=row2(g_final))

    mp = bsz * seq
    xp = x_prompt.reshape(mp, d)
    u, qm, h, kib, kb, kt, vt, vtb, kit, qitb, wit, qtb = _inproj_prompt(x_prompt, row2(g_mix[l]), w_prompt,
                                                                         w_prompt_t, tm=512, **dims)
    per_seq = lambda a: a.reshape(bsz, seq, a.shape[-1])
    u3 = per_seq(u)
    c = _conv_prompt(u3, wdw, bdw, gln, bln)
    n_sel = min(TOPK_MAX, seq // 4)
    tq = 256
    mask = _sel_prompt(per_seq(kib), qitb, wit, tq=tq, n_sel=n_sel)
    oa = _attn_prompt(qtb, per_seq(kb), vtb, mask, tq=tq)
    mk, mv = _memkv(mem_prompt.reshape(bsz * n_mem, d), row2(g_mem[l]), bf(w_mem_kv))
    om = _memattn(per_seq(qm), mk.reshape(bsz, n_mem, d_mem), mv.reshape(bsz, n_mem, d_mem), tq=512)
    x1 = merge(xp, h, c.reshape(mp, d_conv), oa.reshape(mp, d_att), om.reshape(mp, d_mem), tm=512)
    y_prompt = ffn(x1, tm=256).reshape(bsz, seq, d)
    conv_state_prompt = u3[:, seq - (CONV_WIDTH - 1):][None]
    k_prompt = kt.reshape(bsz, N_HEADS, HEAD_DIM, seq).transpose(0, 3, 1, 2)[None]
    v_prompt = vt.reshape(bsz, N_HEADS, HEAD_DIM, seq).transpose(0, 3, 1, 2)[None]
    idx_k_prompt = kit.transpose(0, 2, 1)[None]
    mem_k_prompt = mk.reshape(1, bsz, n_mem, MEM_HEADS, MEM_HEAD_DIM)
    mem_v_prompt = mv.reshape(1, bsz, n_mem, MEM_HEADS, MEM_HEAD_DIM)

    ms = n_dec * t_dec
    xs = x_sample.reshape(ms, d)
    u, q, qm, h, qi, wi, k, v, ki = _inproj_sample(xs, row2(g_mix[l]), w_sample, **dims)
    per_seq = lambda a: a.reshape(n_dec, t_dec, a.shape[-1])
    u_ext = jnp.concatenate([cache_conv[l], per_seq(u)], axis=1)
    c = _conv_sample(u_ext, wdw, bdw, gln, bln)
    n_sel = min(TOPK_MAX, (past + t_dec) // 4)
    qi_hm = per_seq(qi).reshape(n_dec, t_dec, N_IDX_HEADS, IDX_DIM).transpose(0, 2, 1, 3)
    qi_hm = qi_hm.reshape(n_dec, N_IDX_HEADS * t_dec, IDX_DIM)
    wi_hm = per_seq(wi).transpose(0, 2, 1).reshape(n_dec, N_IDX_HEADS * t_dec, 1)
    page_major = lambda a: jnp.moveaxis(a[l], 1, -1).reshape(n_phys, -1, PAGE_SIZE)
    key_past, key_new = _score_sample(page_table, qi_hm, wi_hm, per_seq(ki), page_major(cache_idx_k), pages=16)
    mask_past, mask_new = _sel_sample(key_past.reshape(ms, past), key_new.reshape(ms, LANES), n_sel=n_sel, t=t_dec)
    oa = _attn_sample(page_table, per_seq(q), per_seq(k), per_seq(v), mask_past.reshape(n_dec, t_dec, past),
                      mask_new.reshape(n_dec, t_dec, LANES), page_major(cache_k), page_major(cache_v), pages=16)
    om = _memattn(per_seq(qm), cache_mem_k[l].reshape(n_dec, n_mem, d_mem),
                  cache_mem_v[l].reshape(n_dec, n_mem, d_mem), tq=t_dec)
    x1 = merge(xs, h, c.reshape(ms, d_conv), oa.reshape(ms, d_att), om.reshape(ms, d_mem), tm=ms)
    y_sample = ffn(x1, tm=ms).reshape(n_dec, t_dec, d)
    conv_state_sample = u_ext[:, t_dec:][None]
    k_sample = k.reshape(1, n_dec, t_dec, N_HEADS, HEAD_DIM)
    v_sample = v.reshape(1, n_dec, t_dec, N_HEADS, HEAD_DIM)
    idx_k_sample = ki.reshape(1, n_dec, t_dec, IDX_DIM)

    return (y_prompt, y_sample, conv_state_prompt, k_prompt, v_prompt, idx_k_prompt, mem_k_prompt, mem_v_prompt,
            conv_state_sample, k_sample, v_sample, idx_k_sample)
```

```python
import functools

import jax
import jax.numpy as jnp
import numpy as np
from jax import lax
from jax.experimental import pallas as pl
from jax.experimental.pallas import tpu as pltpu

EPS = 1e-6
NEG = -1e30
CONV_WIDTH = 31
N_HEADS = 8
HEAD_DIM = 64
N_IDX_HEADS = 8
IDX_DIM = 64
MEM_HEADS = 4
MEM_HEAD_DIM = 128
TOPK_MAX = 256
PAGE_SIZE = 128
LANES = 128
SUBLANES = 8
VMEM_LIMIT = 56 * 1024 * 1024
INT_MIN = -(2 ** 31)
LOG2E = 1.4426950408889634

BF16 = jnp.bfloat16
F32 = jnp.float32
I32 = jnp.int32


def _cparams(n_axes):
    return pltpu.CompilerParams(dimension_semantics=("arbitrary",) * n_axes, vmem_limit_bytes=VMEM_LIMIT)


def _const_spec(shape):
    zeros = (0,) * len(shape)
    return pl.BlockSpec(shape, lambda *_: zeros)


def _rmsnorm(x, g):
    return x * lax.rsqrt(jnp.mean(x * x, axis=-1, keepdims=True) + EPS) * g


def _dot(a, b):
    return jnp.dot(a, b, preferred_element_type=F32)


def _dot_nt(a, b):
    return lax.dot_general(a, b, (((1,), (1,)), ((), ())), preferred_element_type=F32)


WI_ROWS = 16


def _glu_qm(h, w_ref, u_ref, qm_ref, d_conv, d_mem):
    o = 0
    glu = _dot(h, w_ref[:, o:o + 2 * d_conv]); o += 2 * d_conv
    u_ref[...] = glu[:, :d_conv] * jax.nn.sigmoid(glu[:, d_conv:])
    qm_ref[...] = _dot(h, w_ref[:, o:o + d_mem]).astype(BF16); o += d_mem
    return o


def _inproj_prompt_kernel(x_ref, g_ref, w_ref, wt_ref, u_ref, qm_ref, h_ref, kib_ref, kb_ref,
                          kt_ref, vt_ref, vtb_ref, kit_ref, qit_ref, wit_ref, qt_ref, *, d_conv, d_att, d_idx, d_mem):
    h = _rmsnorm(x_ref[0], g_ref[...]).astype(BF16)
    h_ref[...] = h
    o = _glu_qm(h, w_ref, u_ref, qm_ref, d_conv, d_mem)
    kib_ref[...] = _dot(h, w_ref[:, o:o + LANES])[:, :IDX_DIM].astype(BF16)
    r = 0
    kt = _dot_nt(wt_ref[r:r + d_att, :], h); r += d_att
    kt_ref[0] = kt
    kb_ref[...] = kt.T.astype(BF16)
    vt = _dot_nt(wt_ref[r:r + d_att, :], h); r += d_att
    vt_ref[0] = vt
    vtb_ref[0] = vt.astype(BF16)
    kit_ref[0] = _dot_nt(wt_ref[r:r + IDX_DIM, :], h); r += IDX_DIM
    wit_ref[0] = _dot_nt(wt_ref[r:r + WI_ROWS, :], h)[:N_IDX_HEADS] * (N_IDX_HEADS ** -0.5); r += WI_ROWS
    qit_ref[0] = (_dot_nt(wt_ref[r:r + d_idx, :], h) * (IDX_DIM ** -0.5)).astype(BF16); r += d_idx
    qt_ref[0] = (_dot_nt(wt_ref[r:r + d_att, :], h) * (HEAD_DIM ** -0.5)).astype(BF16)


def _inproj_prompt(x, g, w, wt, *, tm, d_conv, d_att, d_idx, d_mem):
    bsz, s, d = x.shape
    nt = s // tm
    row = lambda n: pl.BlockSpec((tm, n), lambda b, i: (b * nt + i, 0))
    col = lambda n: pl.BlockSpec((1, n, tm), lambda b, i: (b, 0, i))
    outs = [(d_conv, F32), (d_mem, BF16), (d, BF16), (IDX_DIM, BF16), (d_att, BF16)]
    outs_t = [(d_att, F32), (d_att, F32), (d_att, BF16), (IDX_DIM, F32), (d_idx, BF16), (N_IDX_HEADS, F32),
              (d_att, BF16)]
    return pl.pallas_call(
        functools.partial(_inproj_prompt_kernel, d_conv=d_conv, d_att=d_att, d_idx=d_idx, d_mem=d_mem),
        grid=(bsz, nt),
        in_specs=[pl.BlockSpec((1, tm, d), lambda b, i: (b, i, 0)), _const_spec((1, d)), _const_spec(w.shape),
                  _const_spec(wt.shape)],
        out_specs=[row(n) for n, _ in outs] + [col(n) for n, _ in outs_t],
        out_shape=[jax.ShapeDtypeStruct((bsz * s, n), dt) for n, dt in outs]
                  + [jax.ShapeDtypeStruct((bsz, n, s), dt) for n, dt in outs_t],
        compiler_params=_cparams(2),
        name="inproj_prompt",
    )(x, g, w, wt)


def _inproj_sample_kernel(x_ref, g_ref, w_ref, u_ref, q_ref, qm_ref, h_ref, qi_ref, wi_ref, k_ref, v_ref, ki_ref,
                          *, d_conv, d_att, d_idx, d_mem):
    h = _rmsnorm(x_ref[...], g_ref[...]).astype(BF16)
    h_ref[...] = h
    o = _glu_qm(h, w_ref, u_ref, qm_ref, d_conv, d_mem)
    ki_ref[...] = _dot(h, w_ref[:, o:o + LANES])[:, :IDX_DIM]; o += LANES
    q_ref[...] = (_dot(h, w_ref[:, o:o + d_att]) * (HEAD_DIM ** -0.5)).astype(BF16); o += d_att
    qi_ref[...] = (_dot(h, w_ref[:, o:o + d_idx]) * (IDX_DIM ** -0.5)).astype(BF16); o += d_idx
    wi_ref[...] = _dot(h, w_ref[:, o:o + LANES])[:, :N_IDX_HEADS] * (N_IDX_HEADS ** -0.5); o += LANES
    k_ref[...] = _dot(h, w_ref[:, o:o + d_att]); o += d_att
    v_ref[...] = _dot(h, w_ref[:, o:o + d_att])


def _inproj_sample(x, g, w, *, d_conv, d_att, d_idx, d_mem):
    m, d = x.shape
    outs = [(d_conv, F32), (d_att, BF16), (d_mem, BF16), (d, BF16), (d_idx, BF16), (N_IDX_HEADS, F32),
            (d_att, F32), (d_att, F32), (IDX_DIM, F32)]
    return pl.pallas_call(
        functools.partial(_inproj_sample_kernel, d_conv=d_conv, d_att=d_att, d_idx=d_idx, d_mem=d_mem),
        grid=(1,),
        in_specs=[_const_spec((m, d)), _const_spec((1, d)), _const_spec(w.shape)],
        out_specs=[_const_spec((m, n)) for n, _ in outs],
        out_shape=[jax.ShapeDtypeStruct((m, n), dt) for n, dt in outs],
        compiler_params=_cparams(1),
        name="inproj_sample",
    )(x, g, w)


def _conv_rows(ext_ref, start, rows, w_ref, b_ref, g_ref, bl_ref):
    acc = w_ref[0:1, :] * ext_ref[pl.ds(start, rows), :]
    for j in range(1, CONV_WIDTH):
        acc = acc + w_ref[j:j + 1, :] * ext_ref[pl.ds(start + j, rows), :]
    c = acc + b_ref[...]
    mu = jnp.mean(c, axis=-1, keepdims=True)
    xc = c - mu
    y = xc * lax.rsqrt(jnp.mean(xc * xc, axis=-1, keepdims=True) + EPS) * g_ref[...] + bl_ref[...]
    return (y * jax.nn.sigmoid(y)).astype(BF16)


def _conv_prompt_kernel(prev_ref, cur_ref, w_ref, b_ref, g_ref, bl_ref, o_ref, ext_ref, *, halo, chunk):
    t = cur_ref.shape[1]

    @pl.when(pl.program_id(1) == 0)
    def _():
        ext_ref[0:halo, :] = jnp.zeros((halo, ext_ref.shape[1]), F32)

    @pl.when(pl.program_id(1) > 0)
    def _():
        ext_ref[0:halo, :] = prev_ref[0]

    ext_ref[halo:halo + t, :] = cur_ref[0]
    first = halo - (CONV_WIDTH - 1)
    for c in range(t // chunk):
        o_ref[0, c * chunk:(c + 1) * chunk, :] = _conv_rows(ext_ref, first + c * chunk, chunk, w_ref, b_ref, g_ref,
                                                            bl_ref)


def _conv_prompt(u, w, b, g, bl, *, t=512, halo=32, chunk=64):
    bsz, s, dc = u.shape
    assert s % t == 0 and t % halo == 0 and halo >= CONV_WIDTH - 1 and t % chunk == 0
    r = t // halo
    return pl.pallas_call(
        functools.partial(_conv_prompt_kernel, halo=halo, chunk=chunk),
        grid=(bsz, s // t),
        in_specs=[pl.BlockSpec((1, halo, dc), lambda bi, i: (bi, jnp.maximum(i * r - 1, 0), 0)),
                  pl.BlockSpec((1, t, dc), lambda bi, i: (bi, i, 0)),
                  _const_spec(w.shape), _const_spec((1, dc)), _const_spec((1, dc)), _const_spec((1, dc))],
        out_specs=pl.BlockSpec((1, t, dc), lambda bi, i: (bi, i, 0)),
        out_shape=jax.ShapeDtypeStruct((bsz, s, dc), BF16),
        scratch_shapes=[pltpu.VMEM((halo + t, dc), F32)],
        compiler_params=_cparams(2),
        name="conv_prompt",
    )(u, u, w, b, g, bl)


def _conv_sample_kernel(ext_ref, w_ref, b_ref, g_ref, bl_ref, o_ref):
    t = o_ref.shape[1]
    o_ref[0] = _conv_rows(ext_ref.at[0], 0, t, w_ref, b_ref, g_ref, bl_ref)


def _conv_sample(u_ext, w, b, g, bl):
    n, te, dc = u_ext.shape
    t = te - (CONV_WIDTH - 1)
    return pl.pallas_call(
        _conv_sample_kernel,
        grid=(n,),
        in_specs=[pl.BlockSpec((1, te, dc), lambda i: (i, 0, 0)),
                  _const_spec(w.shape), _const_spec((1, dc)), _const_spec((1, dc)), _const_spec((1, dc))],
        out_specs=pl.BlockSpec((1, t, dc), lambda i: (i, 0, 0)),
        out_shape=jax.ShapeDtypeStruct((n, t, dc), BF16),
        compiler_params=_cparams(1),
        name="conv_sample",
    )(u_ext, w, b, g, bl)


def _memkv_kernel(x_ref, g_ref, w_ref, k_ref, v_ref):
    h = _rmsnorm(x_ref[...], g_ref[...]).astype(BF16)
    kv = _dot(h, w_ref[...])
    d = k_ref.shape[1]
    k_ref[...] = kv[:, :d]
    v_ref[...] = kv[:, d:]


def _memkv(x, g, w, *, tm=256):
    m, d = x.shape
    dm = w.shape[1] // 2
    return pl.pallas_call(
        _memkv_kernel,
        grid=(m // tm,),
        in_specs=[pl.BlockSpec((tm, d), lambda i: (i, 0)), _const_spec((1, d)), _const_spec(w.shape)],
        out_specs=[pl.BlockSpec((tm, dm), lambda i: (i, 0))] * 2,
        out_shape=[jax.ShapeDtypeStruct((m, dm), F32)] * 2,
        compiler_params=_cparams(1),
        name="memkv",
    )(x, g, w)


def _memattn_kernel(q_ref, k_ref, v_ref, o_ref):
    for h in range(MEM_HEADS):
        sl = slice(h * MEM_HEAD_DIM, (h + 1) * MEM_HEAD_DIM)
        s = _dot_nt(q_ref[0, :, sl], k_ref[0, :, sl].astype(BF16)) * (MEM_HEAD_DIM ** -0.5)
        e = jnp.exp(s - jnp.max(s, axis=-1, keepdims=True))
        p = e / jnp.sum(e, axis=-1, keepdims=True)
        o_ref[0, :, sl] = _dot(p.astype(BF16), v_ref[0, :, sl].astype(BF16)).astype(BF16)


def _memattn(q, k, v, *, tq):
    n, t, dm = q.shape
    nm = k.shape[1]
    return pl.pallas_call(
        _memattn_kernel,
        grid=(n, t // tq),
        in_specs=[pl.BlockSpec((1, tq, dm), lambda i, j: (i, j, 0)),
                  pl.BlockSpec((1, nm, dm), lambda i, j: (i, 0, 0)),
                  pl.BlockSpec((1, nm, dm), lambda i, j: (i, 0, 0))],
        out_specs=pl.BlockSpec((1, tq, dm), lambda i, j: (i, j, 0)),
        out_shape=jax.ShapeDtypeStruct((n, t, dm), BF16),
        compiler_params=_cparams(2),
        name="memattn",
    )(q, k, v)


def _ordinal_to_float(o):
    return pltpu.bitcast(jnp.where(o >= 0, o, (-o) | INT_MIN), F32)


def _select_mask(score_ref, key_axis, n_tiles, n_sel, outside, idx_bits, valid_fn, emit_fn):
    _, ta, tb = score_ref.shape
    tile = (ta, tb)
    unit = LANES if key_axis == 1 else 4 * SUBLANES
    tk = tile[key_axis]
    fold = tk // unit
    shp = (ta, LANES) if key_axis == 1 else (unit, tb)
    k_f = jnp.float32(n_sel)
    out_f = jnp.float32(1.0) * outside

    def key_fold(x):
        if key_axis == 0:
            return jnp.sum(x.reshape(fold, unit, tb), axis=0)
        acc = x[:, :LANES]
        for i in range(1, fold):
            acc = acc + x[:, i * LANES:(i + 1) * LANES]
        return acc

    def rep(x):
        return x if fold == 1 else jnp.concatenate([x] * fold, axis=key_axis)

    def row_total(part):
        return jnp.broadcast_to(jnp.sum(part, axis=key_axis, keepdims=True), shp)

    def key_index(j):
        return j * tk + lax.broadcasted_iota(I32, tile, key_axis)

    def count_ge(thr):
        thr_t = rep(thr)

        def body(j, part):
            return part + key_fold(jnp.where(score_ref[j] >= thr_t, 1.0, 0.0))

        part = lax.fori_loop(0, n_tiles, body, jnp.zeros(shp, F32))
        return row_total(part) + jnp.where(thr <= NEG, out_f, 0.0)

    def bit_step(it, carry):
        u, cnt_u = carry
        bit = lax.shift_left(jnp.int32(1), 31 - it)
        cand = u | bit
        cnt = count_ge(_ordinal_to_float(cand ^ INT_MIN))
        ok = cnt >= k_f
        return jnp.where(ok, cand, u), jnp.where(ok, cnt, cnt_u)

    total = jnp.float32(1.0) * (n_tiles * tk) + out_f
    u, cnt_ge = lax.fori_loop(0, 32, bit_step, (jnp.zeros(shp, I32), jnp.broadcast_to(total, shp)))
    thr = _ordinal_to_float(u ^ INT_MIN)
    thr_t = rep(thr)

    surplus = jnp.max(jnp.where(cnt_ge > k_f, 1.0, 0.0)) > 0.0

    def tie_search():
        def count_gt_body(j, part):
            return part + key_fold(jnp.where(score_ref[j] > thr_t, 1.0, 0.0))

        cnt_gt = row_total(lax.fori_loop(0, n_tiles, count_gt_body, jnp.zeros(shp, F32)))
        cnt_gt = cnt_gt + jnp.where(thr < NEG, out_f, 0.0)
        in_tiles = n_tiles * tk

        def idx_step(it, v):
            bit = lax.shift_left(jnp.int32(1), idx_bits - 1 - it)
            cand = v | bit
            cand_t = rep(cand)

            def body(j, part):
                hit = jnp.where(score_ref[j] == thr_t, jnp.where(key_index(j) < cand_t, 1.0, 0.0), 0.0)
                return part + key_fold(hit)

            ties_below = row_total(lax.fori_loop(0, n_tiles, body, jnp.zeros(shp, F32)))
            out_below = jnp.clip(cand - in_tiles, 0, outside).astype(F32)
            ties_below = ties_below + jnp.where(thr == NEG, out_below, 0.0)
            return jnp.where(cnt_gt + ties_below < k_f, cand, v)

        return lax.fori_loop(0, idx_bits, idx_step, jnp.zeros(shp, I32))

    jmax = lax.cond(surplus, tie_search, lambda: jnp.full(shp, 2 ** idx_bits - 1, I32))
    jmax_t = rep(jmax)

    def emit(j, _):
        sc = score_ref[j]
        sel = jnp.where(sc > thr_t, 1.0, jnp.where(sc == thr_t, jnp.where(key_index(j) <= jmax_t, 1.0, 0.0), 0.0))
        emit_fn(j, jnp.where(valid_fn(j, tile), sel, 0.0))
        return 0

    lax.fori_loop(0, n_tiles, emit, 0)


def _sel_prompt_kernel(kib_ref, qit_ref, wit_ref, mask_ref, sc_ref, *, n_sel):
    tq = qit_ref.shape[2]
    s = kib_ref.shape[1]
    n_all = s // tq
    qb = pl.program_id(1)
    n_tiles = qb + 1
    qpos = qb * tq + lax.broadcasted_iota(I32, (tq, tq), 1)

    def score_tile(j):
        ki_t = kib_ref[0, pl.ds(pl.multiple_of(j * tq, tq), tq), :]
        acc = jnp.zeros((tq, tq), F32)
        for h in range(N_IDX_HEADS):
            d = _dot(ki_t, qit_ref[0, h * IDX_DIM:(h + 1) * IDX_DIM, :])
            acc = acc + wit_ref[0, h:h + 1, :] * jnp.maximum(d, 0.0)
        return acc

    def below(j, _):
        sc_ref[j] = score_tile(j)
        return 0

    lax.fori_loop(0, qb, below, 0)
    kpos = qb * tq + lax.broadcasted_iota(I32, (tq, tq), 0)
    sc_ref[qb] = jnp.where(kpos <= qpos, score_tile(qb), NEG)

    def valid(j, shape):
        return j * tq + lax.broadcasted_iota(I32, shape, 0) <= qpos

    def emit(j, sel):
        mask_ref[0, 0, j] = sel.astype(BF16)

    idx_bits = max(1, int(s - 1).bit_length())
    _select_mask(sc_ref, 0, n_tiles, n_sel, (n_all - n_tiles) * tq, idx_bits, valid, emit)

    def clear(j, _):
        mask_ref[0, 0, j] = jnp.zeros((tq, tq), BF16)
        return 0

    lax.fori_loop(n_tiles, n_all, clear, 0)


def _sel_prompt(kib, qitb, wit, *, tq, n_sel):
    bsz, s, _ = kib.shape
    nq = s // tq
    return pl.pallas_call(
        functools.partial(_sel_prompt_kernel, n_sel=n_sel),
        grid=(bsz, nq),
        in_specs=[pl.BlockSpec((1, s, kib.shape[2]), lambda b, i: (b, 0, 0)),
                  pl.BlockSpec((1, qitb.shape[1], tq), lambda b, i: (b, 0, i)),
                  pl.BlockSpec((1, wit.shape[1], tq), lambda b, i: (b, 0, i))],
        out_specs=pl.BlockSpec((1, 1, nq, tq, tq), lambda b, i: (b, i, 0, 0, 0)),
        out_shape=jax.ShapeDtypeStruct((bsz, nq, nq, tq, tq), BF16),
        scratch_shapes=[pltpu.VMEM((nq, tq, tq), F32)],
        compiler_params=_cparams(2),
        name="sel_prompt",
    )(kib, qitb, wit)


def _attn_prompt_kernel(qt_ref, k_ref, vt_ref, mask_ref, o_ref, q2_ref, s_ref, mx_ref, l_ref, out_ref):
    tq = qt_ref.shape[2]
    n_tiles = pl.program_id(1) + 1
    pair = 2 * HEAD_DIM
    part = 4 * SUBLANES
    fold = tq // part
    in_pair = lax.broadcasted_iota(I32, (pair, tq), 0)
    for hp in range(N_HEADS // 2):
        qp = qt_ref[0, hp * pair:(hp + 1) * pair, :]
        q2_ref[2 * hp] = jnp.where(in_pair < HEAD_DIM, qp, jnp.zeros_like(qp))
        q2_ref[2 * hp + 1] = jnp.where(in_pair >= HEAD_DIM, qp, jnp.zeros_like(qp))

    def key_rows(j):
        return pl.ds(pl.multiple_of(j * tq, tq), tq)

    group = s_ref.shape[1]
    n_groups = N_HEADS // group

    def score_tile(j, sel, slot, h0):
        for g in range(group):
            h = h0 + g
            k_pair = k_ref[0, key_rows(j), (h // 2) * pair:(h // 2 + 1) * pair]
            s = jnp.where(sel, _dot(k_pair, q2_ref[h]) * LOG2E, NEG)
            s_ref[slot, g, j] = s
            mx_ref[slot, g] = jnp.maximum(mx_ref[slot, g], jnp.max(s.reshape(fold, part, tq), axis=0))

    def weigh_tile(j, slot, h0, m):
        for g in range(group):
            rows = slice((h0 + g) * HEAD_DIM, (h0 + g + 1) * HEAD_DIM)
            p = jnp.exp2(s_ref[slot, g, j] - m[g])
            l_ref[g] = l_ref[g] + jnp.sum(p.reshape(fold, part, tq), axis=0)
            out_ref[rows, :] = out_ref[rows, :] + _dot(vt_ref[0, rows, key_rows(j)], p.astype(BF16))

    for gi in range(n_groups + 1):
        slot, prev = gi % 2, (gi - 1) % 2
        h0, h_prev = gi * group, (gi - 1) * group
        m_prev = None
        if gi > 0:
            m_prev = [jnp.max(mx_ref[prev, g], axis=0, keepdims=True) for g in range(group)]
            l_ref[...] = jnp.zeros(l_ref.shape, F32)
            out_ref[h_prev * HEAD_DIM:h0 * HEAD_DIM, :] = jnp.zeros((group * HEAD_DIM, tq), F32)
        if gi < n_groups:
            mx_ref[slot] = jnp.full(mx_ref.shape[1:], NEG, F32)

        def sweep(j, _, gi=gi, slot=slot, prev=prev, h0=h0, h_prev=h_prev, m_prev=m_prev):
            if gi < n_groups:
                score_tile(j, mask_ref[0, 0, j].astype(F32) > 0.0, slot, h0)
            if gi > 0:
                weigh_tile(j, prev, h_prev, m_prev)
            return 0

        lax.fori_loop(0, n_tiles, sweep, 0)
        if gi > 0:
            for g in range(group):
                rows = slice((h_prev + g) * HEAD_DIM, (h_prev + g + 1) * HEAD_DIM)
                out_ref[rows, :] = out_ref[rows, :] / jnp.sum(l_ref[g], axis=0, keepdims=True)
    o_ref[0] = out_ref[...].T.astype(BF16)


def _attn_prompt(qtb, kb, vtb, mask, *, tq, group=2):
    bsz, da, s = qtb.shape
    nq = s // tq
    part = 4 * SUBLANES
    return pl.pallas_call(
        _attn_prompt_kernel,
        grid=(bsz, nq),
        in_specs=[pl.BlockSpec((1, da, tq), lambda b, i: (b, 0, i)),
                  pl.BlockSpec((1, s, da), lambda b, i: (b, 0, 0)),
                  pl.BlockSpec((1, da, s), lambda b, i: (b, 0, 0)),
                  pl.BlockSpec((1, 1, nq, tq, tq), lambda b, i: (b, i, 0, 0, 0))],
        out_specs=pl.BlockSpec((1, tq, da), lambda b, i: (b, i, 0)),
        out_shape=jax.ShapeDtypeStruct((bsz, s, da), BF16),
        scratch_shapes=[pltpu.VMEM((N_HEADS, 2 * HEAD_DIM, tq), BF16), pltpu.VMEM((2, group, nq, tq, tq), F32),
                        pltpu.VMEM((2, group, part, tq), F32), pltpu.VMEM((group, part, tq), F32),
                        pltpu.VMEM((da, tq), F32)],
        compiler_params=_cparams(2),
        name="attn_prompt",
    )(qtb, kb, vtb, mask)


def _score_sample_kernel(pt_ref, qi_ref, wi_ref, kin_ref, cache_ref, past_ref, new_ref, buf_ref, sem_ref, *, chunk):
    seq = pl.program_id(0)
    n_seq = pl.num_programs(0)
    n_pages = buf_ref.shape[1]
    t = new_ref.shape[1]
    slot = lax.rem(seq, 2)

    def page_copy(s, i, buf_slot):
        return pltpu.make_async_copy(cache_ref.at[pt_ref[s, i]], buf_ref.at[buf_slot, i], sem_ref.at[buf_slot])

    def start_pages(s, buf_slot):
        def body(i, _):
            page_copy(s, i, buf_slot).start()
            return 0
        lax.fori_loop(0, n_pages, body, 0)

    @pl.when(seq == 0)
    def _():
        start_pages(seq, slot)

    @pl.when(seq + 1 < n_seq)
    def _():
        start_pages(seq + 1, 1 - slot)

    def wait_body(i, _):
        page_copy(seq, i, slot).wait()
        return 0

    lax.fori_loop(0, n_pages, wait_body, 0)

    qi = qi_ref[0]
    wi = wi_ref[0]

    def scores(dots):
        w = wi * jnp.maximum(dots, 0.0)
        acc = w[0:t]
        for h in range(1, N_IDX_HEADS):
            acc = acc + w[h * t:(h + 1) * t]
        return acc

    for c in range(n_pages // chunk):
        keys_t = jnp.concatenate([buf_ref[slot, c * chunk + i].astype(BF16) for i in range(chunk)], axis=1)
        past_ref[0, :, c * chunk * PAGE_SIZE:(c + 1) * chunk * PAGE_SIZE] = scores(_dot(qi, keys_t))

    ext = jnp.concatenate([kin_ref[0], jnp.zeros((LANES - t, IDX_DIM), F32)], axis=0).astype(BF16)
    sc = scores(_dot_nt(qi, ext))
    qpos = lax.broadcasted_iota(I32, (t, LANES), 0)
    kpos = lax.broadcasted_iota(I32, (t, LANES), 1)
    new_ref[0] = jnp.where(kpos < t, jnp.where(kpos <= qpos, sc, NEG), -jnp.inf)


def _score_sample(page_table, qi_hm, wi_hm, ki_new, cache_idx_k, *, chunk=16):
    n, n_pages = page_table.shape
    t = ki_new.shape[1]
    ht = qi_hm.shape[1]
    grid_spec = pltpu.PrefetchScalarGridSpec(
        num_scalar_prefetch=1,
        grid=(n,),
        in_specs=[pl.BlockSpec((1, ht, IDX_DIM), lambda b, pt: (b, 0, 0)),
                  pl.BlockSpec((1, ht, 1), lambda b, pt: (b, 0, 0)),
                  pl.BlockSpec((1, t, IDX_DIM), lambda b, pt: (b, 0, 0)),
                  pl.BlockSpec(memory_space=pl.ANY)],
        out_specs=[pl.BlockSpec((1, t, n_pages * PAGE_SIZE), lambda b, pt: (b, 0, 0)),
                   pl.BlockSpec((1, t, LANES), lambda b, pt: (b, 0, 0))],
        scratch_shapes=[pltpu.VMEM((2, n_pages, IDX_DIM, PAGE_SIZE), F32), pltpu.SemaphoreType.DMA((2,))],
    )
    return pl.pallas_call(
        functools.partial(_score_sample_kernel, chunk=chunk),
        grid_spec=grid_spec,
        out_shape=[jax.ShapeDtypeStruct((n, t, n_pages * PAGE_SIZE), F32),
                   jax.ShapeDtypeStruct((n, t, LANES), F32)],
        compiler_params=_cparams(1),
        name="score_sample",
    )(page_table, qi_hm, wi_hm, ki_new, cache_idx_k)


def _sel_sample_kernel(past_ref, new_ref, mpast_ref, mnew_ref, key_ref, mask_ref, *, n_sel, t, tile):
    rows = past_ref.shape[0]
    n_past = past_ref.shape[1] // tile
    for j in range(n_past):
        key_ref[j] = past_ref[:, j * tile:(j + 1) * tile]
    pad = jnp.full((rows, tile - LANES), -jnp.inf, F32)
    key_ref[n_past] = jnp.concatenate([new_ref[...], pad], axis=1)
    qpos = lax.rem(lax.broadcasted_iota(I32, (rows, tile), 0), t)

    def valid(j, shape):
        kpos = lax.broadcasted_iota(I32, shape, 1)
        return kpos <= qpos + jnp.minimum(n_past - j, 1) * tile

    def emit(j, sel):
        mask_ref[j] = sel

    idx_bits = int((n_past + 1) * tile - 1).bit_length()
    _select_mask(key_ref, 1, n_past + 1, n_sel, 0, idx_bits, valid, emit)
    for j in range(n_past):
        mpast_ref[:, j * tile:(j + 1) * tile] = mask_ref[j]
    mnew_ref[...] = mask_ref[n_past][:, :LANES]


def _sel_sample(key_past, key_new, *, n_sel, t, rows=64, tile=2048):
    r, n_keys = key_past.shape
    n_t = n_keys // tile + 1
    return pl.pallas_call(
        functools.partial(_sel_sample_kernel, n_sel=n_sel, t=t, tile=tile),
        grid=(r // rows,),
        in_specs=[pl.BlockSpec((rows, n_keys), lambda i: (i, 0)), pl.BlockSpec((rows, LANES), lambda i: (i, 0))],
        out_specs=[pl.BlockSpec((rows, n_keys), lambda i: (i, 0)), pl.BlockSpec((rows, LANES), lambda i: (i, 0))],
        out_shape=[jax.ShapeDtypeStruct((r, n_keys), F32), jax.ShapeDtypeStruct((r, LANES), F32)],
        scratch_shapes=[pltpu.VMEM((n_t, rows, tile), F32), pltpu.VMEM((n_t, rows, tile), F32)],
        compiler_params=_cparams(1),
        name="sel_sample",
    )(key_past, key_new)


def _attn_sample_kernel(pt_ref, q_ref, kn_ref, vn_ref, mpast_ref, mnew_ref, *rest, pages):
    k_refs = rest[:pages]
    v_refs = rest[pages:2 * pages]
    o_ref, qbd_ref, m_ref, l_ref, acc_ref = rest[2 * pages:]
    t, da = q_ref.shape[1], q_ref.shape[2]
    ht = N_HEADS * t
    j = pl.program_id(1)

    @pl.when(j == 0)
    def _():
        q_rep = jnp.concatenate([q_ref[0].astype(F32)] * N_HEADS, axis=0)
        head_of_row = lax.broadcasted_iota(I32, (ht, da), 0) // t
        head_of_col = lax.broadcasted_iota(I32, (ht, da), 1) // HEAD_DIM
        qbd_ref[...] = jnp.where(head_of_row == head_of_col, q_rep, 0.0).astype(BF16)
        m_ref[...] = jnp.full(m_ref.shape, NEG, F32)
        l_ref[...] = jnp.zeros(l_ref.shape, F32)
        acc_ref[...] = jnp.zeros(acc_ref.shape, F32)

    def attend(k_pages, v_pages, mask_t, feature_major):
        qbd = qbd_ref[...]
        qk = [(_dot(qbd, kp) if feature_major else _dot_nt(qbd, kp)) for kp in k_pages]
        sel = jnp.concatenate([mask_t] * N_HEADS, axis=0) > 0.0
        s = jnp.where(sel, qk[0] if len(qk) == 1 else jnp.concatenate(qk, axis=1), NEG)
        m_old = m_ref[...]
        m_new = jnp.maximum(m_old, jnp.max(s, axis=-1, keepdims=True))
        alpha = jnp.exp(m_old - m_new)
        p = jnp.where(sel, jnp.exp(s - m_new), 0.0)
        l_ref[...] = alpha * l_ref[...] + jnp.sum(p, axis=-1, keepdims=True)
        pb = p.astype(BF16)
        pv = None
        for i, vp in enumerate(v_pages):
            p_i = pb[:, i * PAGE_SIZE:(i + 1) * PAGE_SIZE]
            term = _dot_nt(p_i, vp) if feature_major else _dot(p_i, vp)
            pv = term if pv is None else pv + term
        acc_ref[...] = alpha * acc_ref[...] + pv
        m_ref[...] = m_new

    attend([r[0].astype(BF16) for r in k_refs], [r[0].astype(BF16) for r in v_refs], mpast_ref[0], True)

    @pl.when(j == pl.num_programs(1) - 1)
    def _():
        zeros = jnp.zeros((PAGE_SIZE - t, da), F32)
        attend([jnp.concatenate([kn_ref[0], zeros], axis=0).astype(BF16)],
               [jnp.concatenate([vn_ref[0], zeros], axis=0).astype(BF16)], mnew_ref[0], False)
        out = acc_ref[...] / l_ref[...]
        o_ref[0] = jnp.concatenate(
            [out[h * t:(h + 1) * t, h * HEAD_DIM:(h + 1) * HEAD_DIM] for h in range(N_HEADS)], axis=1).astype(BF16)


def _attn_sample(page_table, q, k_new, v_new, mask_past, mask_new, cache_k, cache_v, *, pages):
    n, n_pages = page_table.shape
    t, da = q.shape[1], q.shape[2]
    steps = n_pages // pages
    ht = N_HEADS * t

    def page_spec(i):
        return pl.BlockSpec((1, da, PAGE_SIZE), lambda b, j, pt: (pt[b, j * pages + i], 0, 0))

    per_seq = lambda shape: pl.BlockSpec((1,) + shape, lambda b, j, pt: (b, 0, 0))
    grid_spec = pltpu.PrefetchScalarGridSpec(
        num_scalar_prefetch=1,
        grid=(n, steps),
        in_specs=[per_seq((t, da)), per_seq((t, da)), per_seq((t, da)),
                  pl.BlockSpec((1, t, pages * PAGE_SIZE), lambda b, j, pt: (b, 0, j)), per_seq((t, LANES))]
                 + [page_spec(i) for i in range(pages)] + [page_spec(i) for i in range(pages)],
        out_specs=per_seq((t, da)),
        scratch_shapes=[pltpu.VMEM((ht, da), BF16), pltpu.VMEM((ht, 1), F32), pltpu.VMEM((ht, 1), F32),
                        pltpu.VMEM((ht, da), F32)],
    )
    return pl.pallas_call(
        functools.partial(_attn_sample_kernel, pages=pages),
        grid_spec=grid_spec,
        out_shape=jax.ShapeDtypeStruct((n, t, da), BF16),
        compiler_params=_cparams(2),
        name="attn_sample",
    )(page_table, q, k_new, v_new, mask_past, mask_new, *([cache_k] * pages), *([cache_v] * pages))


def _merge_kernel(x_ref, h_ref, c_ref, a_ref, m_ref, wg_ref, wc_ref, wa_ref, wm_ref, wo_ref, o_ref):
    d = x_ref.shape[1]
    gates = jax.nn.sigmoid(_dot(h_ref[...], wg_ref[...]))
    mix = (gates[:, :d] * _dot(c_ref[...], wc_ref[...])
           + gates[:, d:2 * d] * _dot(a_ref[...], wa_ref[...])
           + gates[:, 2 * d:] * _dot(m_ref[...], wm_ref[...]))
    o_ref[...] = x_ref[...] + _dot(mix.astype(BF16), wo_ref[...])


def _merge(x, h, c, a, mo, wg, wc, wa, wm, wo, *, tm):
    m, d = x.shape
    row = lambda n: pl.BlockSpec((tm, n), lambda i: (i, 0))
    return pl.pallas_call(
        _merge_kernel,
        grid=(m // tm,),
        in_specs=[row(d), row(d), row(c.shape[1]), row(a.shape[1]), row(mo.shape[1]),
                  _const_spec(wg.shape), _const_spec(wc.shape), _const_spec(wa.shape), _const_spec(wm.shape),
                  _const_spec(wo.shape)],
        out_specs=row(d),
        out_shape=jax.ShapeDtypeStruct((m, d), F32),
        compiler_params=_cparams(1),
        name="merge",
    )(x, h, c, a, mo, wg, wc, wa, wm, wo)


def _ffn_kernel(x_ref, g_ref, wi_ref, wo_ref, gf_ref, o_ref):
    x = x_ref[...]
    dff = wo_ref.shape[0]
    hn = _rmsnorm(x, g_ref[...]).astype(BF16)
    a = _dot(hn, wi_ref[:, :dff])
    b = _dot(hn, wi_ref[:, dff:])
    y = x + _dot((a * jax.nn.sigmoid(a) * b).astype(BF16), wo_ref[...])
    o_ref[...] = _rmsnorm(y, gf_ref[...])


def _ffn(x, g, w_in, w_out, g_final, *, tm):
    m, d = x.shape
    return pl.pallas_call(
        _ffn_kernel,
        grid=(m // tm,),
        in_specs=[pl.BlockSpec((tm, d), lambda i: (i, 0)), _const_spec((1, d)), _const_spec(w_in.shape),
                  _const_spec(w_out.shape), _const_spec((1, d))],
        out_specs=pl.BlockSpec((tm, d), lambda i: (i, 0)),
        out_shape=jax.ShapeDtypeStruct((m, d), F32),
        compiler_params=_cparams(1),
        name="ffn",
    )(x, g, w_in, w_out, g_final)


def kernel(x_prompt, x_sample, mem_prompt, cache_conv, cache_k, cache_v, cache_idx_k, cache_mem_k, cache_mem_v,
           page_table, g_mix, w_in, w_conv_dw, b_conv_dw, g_conv_ln, b_conv_ln, w_conv_out, w_att_out, g_mem,
           w_mem_kv, w_mem_out, w_out, g_ffn, w_ffn_in, w_ffn_out, g_final):
    depth = g_mix.shape[0]
    assert depth == 1
    bsz, seq, d = x_prompt.shape
    n_dec, t_dec, _ = x_sample.shape
    n_mem = mem_prompt.shape[1]
    d_conv = w_conv_dw.shape[2]
    d_att = N_HEADS * HEAD_DIM
    d_idx = N_IDX_HEADS * IDX_DIM
    d_mem = MEM_HEADS * MEM_HEAD_DIM
    n_phys = cache_k.shape[1]
    past = page_table.shape[1] * PAGE_SIZE
    l = 0

    splits = [2 * d_conv, d_att, d_att, d_att, d_idx, IDX_DIM, N_IDX_HEADS, d_mem, 3 * d]
    offs = np.concatenate([[0], np.cumsum(splits)])
    w = w_in[l]
    col = lambda i: w[:, offs[i]:offs[i + 1]]
    pad_cols = lambda a, n: jnp.pad(a, ((0, 0), (0, n - a.shape[1])))
    glu, wq, wk, wv, wqi, wki, wwi, wqm, wgates = (col(i) for i in range(9))
    shared = [glu, wqm, pad_cols(wki, LANES)]
    w_prompt = jnp.concatenate(shared, axis=1).astype(BF16)
    w_prompt_t = jnp.concatenate([wk, wv, wki, pad_cols(wwi, WI_ROWS), wqi, wq], axis=1).T.astype(BF16)
    w_sample = jnp.concatenate(shared + [wq, wqi, pad_cols(wwi, LANES), wk, wv], axis=1).astype(BF16)
    w_gates = wgates.astype(BF16)
    bf = lambda a: a[l].astype(BF16)
    row2 = lambda a: a.reshape(1, -1)
    wdw, bdw, gln, bln = w_conv_dw[l], row2(b_conv_dw[l]), row2(g_conv_ln[l]), row2(b_conv_ln[l])
    dims = dict(d_conv=d_conv, d_att=d_att, d_idx=d_idx, d_mem=d_mem)
    merge = functools.partial(_merge, wg=w_gates, wc=bf(w_conv_out), wa=bf(w_att_out), wm=bf(w_mem_out),
                              wo=bf(w_out))
    ffn = functools.partial(_ffn, g=row2(g_ffn[l]), w_in=bf(w_ffn_in), w_out=bf(w_ffn_out),
                            g_final=row2(g_final))

    mp = bsz * seq
    xp = x_prompt.reshape(mp, d)
    u, qm, h, kib, kb, kt, vt, vtb, kit, qitb, wit, qtb = _inproj_prompt(x_prompt, row2(g_mix[l]), w_prompt,
                                                                         w_prompt_t, tm=512, **dims)
    per_seq = lambda a: a.reshape(bsz, seq, a.shape[-1])
    u3 = per_seq(u)
    c = _conv_prompt(u3, wdw, bdw, gln, bln)
    n_sel = min(TOPK_MAX, seq // 4)
    tq = 256
    mask = _sel_prompt(per_seq(kib), qitb, wit, tq=tq, n_sel=n_sel)
    oa = _attn_prompt(qtb, per_seq(kb), vtb, mask, tq=tq)
    mk, mv = _memkv(mem_prompt.reshape(bsz * n_mem, d), row2(g_mem[l]), bf(w_mem_kv))
    om = _memattn(per_seq(qm), mk.reshape(bsz, n_mem, d_mem), mv.reshape(bsz, n_mem, d_mem), tq=512)
    x1 = merge(xp, h, c.reshape(mp, d_conv), oa.reshape(mp, d_att), om.reshape(mp, d_mem), tm=512)
    y_prompt = ffn(x1, tm=256).reshape(bsz, seq, d)
    conv_state_prompt = u3[:, seq - (CONV_WIDTH - 1):][None]
    k_prompt = kt.reshape(bsz, N_HEADS, HEAD_DIM, seq).transpose(0, 3, 1, 2)[None]
    v_prompt = vt.reshape(bsz, N_HEADS, HEAD_DIM, seq).transpose(0, 3, 1, 2)[None]
    idx_k_prompt = kit.transpose(0, 2, 1)[None]
    mem_k_prompt = mk.reshape(1, bsz, n_mem, MEM_HEADS, MEM_HEAD_DIM)
    mem_v_prompt = mv.reshape(1, bsz, n_mem, MEM_HEADS, MEM_HEAD_DIM)

    ms = n_dec * t_dec
    xs = x_sample.reshape(ms, d)
    u, q, qm, h, qi, wi, k, v, ki = _inproj_sample(xs, row2(g_mix[l]), w_sample, **dims)
    per_seq = lambda a: a.reshape(n_dec, t_dec, a.shape[-1])
    u_ext = jnp.concatenate([cache_conv[l], per_seq(u)], axis=1)
    c = _conv_sample(u_ext, wdw, bdw, gln, bln)
    n_sel = min(TOPK_MAX, (past + t_dec) // 4)
    qi_hm = per_seq(qi).reshape(n_dec, t_dec, N_IDX_HEADS, IDX_DIM).transpose(0, 2, 1, 3)
    qi_hm = qi_hm.reshape(n_dec, N_IDX_HEADS * t_dec, IDX_DIM)
    wi_hm = per_seq(wi).transpose(0, 2, 1).reshape(n_dec, N_IDX_HEADS * t_dec, 1)
    page_major = lambda a: jnp.moveaxis(a[l], 1, -1).reshape(n_phys, -1, PAGE_SIZE)
    key_past, key_new = _score_sample(page_table, qi_hm, wi_hm, per_seq(ki), page_major(cache_idx_k))
    mask_past, mask_new = _sel_sample(key_past.reshape(ms, past), key_new.reshape(ms, LANES), n_sel=n_sel, t=t_dec)
    oa = _attn_sample(page_table, per_seq(q), per_seq(k), per_seq(v), mask_past.reshape(n_dec, t_dec, past),
                      mask_new.reshape(n_dec, t_dec, LANES), page_major(cache_k), page_major(cache_v), pages=32)
    om = _memattn(per_seq(qm), cache_mem_k[l].reshape(n_dec, n_mem, d_mem),
                  cache_mem_v[l].reshape(n_dec, n_mem, d_mem), tq=t_dec)
    x1 = merge(xs, h, c.reshape(ms, d_conv), oa.reshape(ms, d_att), om.reshape(ms, d_mem), tm=ms)
    y_sample = ffn(x1, tm=ms).reshape(n_dec, t_dec, d)
    conv_state_sample = u_ext[:, t_dec:][None]
    k_sample = k.reshape(1, n_dec, t_dec, N_HEADS, HEAD_DIM)
    v_sample = v.reshape(1, n_dec, t_dec, N_HEADS, HEAD_DIM)
    idx_k_sample = ki.reshape(1, n_dec, t_dec, IDX_DIM)

    return (y_prompt, y_sample, conv_state_prompt, k_prompt, v_prompt, idx_k_prompt, mem_k_prompt, mem_v_prompt,
            conv_state_sample, k_sample, v_sample, idx_k_sample)
```

```python
import functools

import jax
import jax.numpy as jnp
import numpy as np
from jax import lax
from jax.experimental import pallas as pl
from jax.experimental.pallas import tpu as pltpu

EPS = 1e-6
NEG = -1e30
CONV_WIDTH = 31
N_HEADS = 8
HEAD_DIM = 64
N_IDX_HEADS = 8
IDX_DIM = 64
MEM_HEADS = 4
MEM_HEAD_DIM = 128
TOPK_MAX = 256
PAGE_SIZE = 128
LANES = 128
SUBLANES = 8
VMEM_LIMIT = 56 * 1024 * 1024
INT_MIN = -(2 ** 31)
LOG2E = 1.4426950408889634
TILES_PER_STEP = 2

BF16 = jnp.bfloat16
F32 = jnp.float32
I32 = jnp.int32


def _cparams(n_axes):
    return pltpu.CompilerParams(dimension_semantics=("arbitrary",) * n_axes, vmem_limit_bytes=VMEM_LIMIT)


def _const_spec(shape):
    zeros = (0,) * len(shape)
    return pl.BlockSpec(shape, lambda *_: zeros)


def _rmsnorm(x, g):
    return x * lax.rsqrt(jnp.mean(x * x, axis=-1, keepdims=True) + EPS) * g


def _dot(a, b):
    return jnp.dot(a, b, preferred_element_type=F32)


def _dot_nt(a, b):
    return lax.dot_general(a, b, (((1,), (1,)), ((), ())), preferred_element_type=F32)


WI_ROWS = 16


def _glu_qm(h, w_ref, u_ref, qm_ref, d_conv, d_mem):
    o = 0
    glu = _dot(h, w_ref[:, o:o + 2 * d_conv]); o += 2 * d_conv
    u_ref[...] = glu[:, :d_conv] * jax.nn.sigmoid(glu[:, d_conv:])
    qm_ref[...] = _dot(h, w_ref[:, o:o + d_mem]).astype(BF16); o += d_mem
    return o


def _inproj_prompt_kernel(x_ref, g_ref, w_ref, wt_ref, u_ref, qm_ref, h_ref, kib_ref, kb_ref,
                          kt_ref, vt_ref, vtb_ref, kit_ref, qit_ref, wit_ref, qt_ref, *, d_conv, d_att, d_idx, d_mem):
    h = _rmsnorm(x_ref[0], g_ref[...]).astype(BF16)
    h_ref[...] = h
    o = _glu_qm(h, w_ref, u_ref, qm_ref, d_conv, d_mem)
    kib_ref[...] = _dot(h, w_ref[:, o:o + LANES])[:, :IDX_DIM].astype(BF16)
    r = 0
    kt = _dot_nt(wt_ref[r:r + d_att, :], h); r += d_att
    kt_ref[0] = kt
    kb_ref[...] = kt.T.astype(BF16)
    vt = _dot_nt(wt_ref[r:r + d_att, :], h); r += d_att
    vt_ref[0] = vt
    vtb_ref[0] = vt.astype(BF16)
    kit_ref[0] = _dot_nt(wt_ref[r:r + IDX_DIM, :], h); r += IDX_DIM
    wit_ref[0] = _dot_nt(wt_ref[r:r + WI_ROWS, :], h)[:N_IDX_HEADS] * (N_IDX_HEADS ** -0.5); r += WI_ROWS
    qit_ref[0] = (_dot_nt(wt_ref[r:r + d_idx, :], h) * (IDX_DIM ** -0.5)).astype(BF16); r += d_idx
    qt_ref[0] = (_dot_nt(wt_ref[r:r + d_att, :], h) * (HEAD_DIM ** -0.5)).astype(BF16)


def _inproj_prompt(x, g, w, wt, *, tm, d_conv, d_att, d_idx, d_mem):
    bsz, s, d = x.shape
    nt = s // tm
    row = lambda n: pl.BlockSpec((tm, n), lambda b, i: (b * nt + i, 0))
    col = lambda n: pl.BlockSpec((1, n, tm), lambda b, i: (b, 0, i))
    outs = [(d_conv, F32), (d_mem, BF16), (d, BF16), (IDX_DIM, BF16), (d_att, BF16)]
    outs_t = [(d_att, F32), (d_att, F32), (d_att, BF16), (IDX_DIM, F32), (d_idx, BF16), (N_IDX_HEADS, F32),
              (d_att, BF16)]
    return pl.pallas_call(
        functools.partial(_inproj_prompt_kernel, d_conv=d_conv, d_att=d_att, d_idx=d_idx, d_mem=d_mem),
        grid=(bsz, nt),
        in_specs=[pl.BlockSpec((1, tm, d), lambda b, i: (b, i, 0)), _const_spec((1, d)), _const_spec(w.shape),
                  _const_spec(wt.shape)],
        out_specs=[row(n) for n, _ in outs] + [col(n) for n, _ in outs_t],
        out_shape=[jax.ShapeDtypeStruct((bsz * s, n), dt) for n, dt in outs]
                  + [jax.ShapeDtypeStruct((bsz, n, s), dt) for n, dt in outs_t],
        compiler_params=_cparams(2),
        name="inproj_prompt",
    )(x, g, w, wt)


def _inproj_sample_kernel(x_ref, g_ref, w_ref, u_ref, q_ref, qm_ref, h_ref, qi_ref, wi_ref, k_ref, v_ref, ki_ref,
                          *, d_conv, d_att, d_idx, d_mem):
    h = _rmsnorm(x_ref[...], g_ref[...]).astype(BF16)
    h_ref[...] = h
    o = _glu_qm(h, w_ref, u_ref, qm_ref, d_conv, d_mem)
    ki_ref[...] = _dot(h, w_ref[:, o:o + LANES])[:, :IDX_DIM]; o += LANES
    q_ref[...] = (_dot(h, w_ref[:, o:o + d_att]) * (HEAD_DIM ** -0.5)).astype(BF16); o += d_att
    qi_ref[...] = (_dot(h, w_ref[:, o:o + d_idx]) * (IDX_DIM ** -0.5)).astype(BF16); o += d_idx
    wi_ref[...] = _dot(h, w_ref[:, o:o + LANES])[:, :N_IDX_HEADS] * (N_IDX_HEADS ** -0.5); o += LANES
    k_ref[...] = _dot(h, w_ref[:, o:o + d_att]); o += d_att
    v_ref[...] = _dot(h, w_ref[:, o:o + d_att])


def _inproj_sample(x, g, w, *, d_conv, d_att, d_idx, d_mem):
    m, d = x.shape
    outs = [(d_conv, F32), (d_att, BF16), (d_mem, BF16), (d, BF16), (d_idx, BF16), (N_IDX_HEADS, F32),
            (d_att, F32), (d_att, F32), (IDX_DIM, F32)]
    return pl.pallas_call(
        functools.partial(_inproj_sample_kernel, d_conv=d_conv, d_att=d_att, d_idx=d_idx, d_mem=d_mem),
        grid=(1,),
        in_specs=[_const_spec((m, d)), _const_spec((1, d)), _const_spec(w.shape)],
        out_specs=[_const_spec((m, n)) for n, _ in outs],
        out_shape=[jax.ShapeDtypeStruct((m, n), dt) for n, dt in outs],
        compiler_params=_cparams(1),
        name="inproj_sample",
    )(x, g, w)


def _conv_taps(ext_ref, start, rows, w_ref, by_phase):
    if not by_phase:
        acc = w_ref[0:1, :] * ext_ref[pl.ds(start, rows), :]
        for j in range(1, CONV_WIDTH):
            acc = acc + w_ref[j:j + 1, :] * ext_ref[pl.ds(start + j, rows), :]
        return acc
    acc = None
    for r in range(SUBLANES):
        y = None
        for j in range(CONV_WIDTH):
            if (start + j) % SUBLANES == r:
                term = w_ref[j:j + 1, :] * ext_ref[pl.ds(start + j - r, rows + (SUBLANES if r else 0)), :]
                y = term if y is None else y + term
        if y is not None:
            y = y[r:r + rows] if r else y
            acc = y if acc is None else acc + y
    return acc


def _conv_rows(ext_ref, start, rows, w_ref, b_ref, g_ref, bl_ref, by_phase=False):
    c = _conv_taps(ext_ref, start, rows, w_ref, by_phase) + b_ref[...]
    mu = jnp.mean(c, axis=-1, keepdims=True)
    xc = c - mu
    y = xc * lax.rsqrt(jnp.mean(xc * xc, axis=-1, keepdims=True) + EPS) * g_ref[...] + bl_ref[...]
    return (y * jax.nn.sigmoid(y)).astype(BF16)


def _conv_prompt_kernel(prev_ref, cur_ref, w_ref, b_ref, g_ref, bl_ref, o_ref, ext_ref, *, halo, chunk):
    t = cur_ref.shape[1]

    @pl.when(pl.program_id(1) == 0)
    def _():
        ext_ref[0:halo, :] = jnp.zeros((halo, ext_ref.shape[1]), F32)

    @pl.when(pl.program_id(1) > 0)
    def _():
        ext_ref[0:halo, :] = prev_ref[0]

    ext_ref[halo:halo + t, :] = cur_ref[0]
    ext_ref[halo + t:, :] = jnp.zeros((SUBLANES, ext_ref.shape[1]), F32)
    first = halo - (CONV_WIDTH - 1)
    for c in range(t // chunk):
        o_ref[0, c * chunk:(c + 1) * chunk, :] = _conv_rows(ext_ref, first + c * chunk, chunk, w_ref, b_ref, g_ref,
                                                            bl_ref, by_phase=True)


def _conv_prompt(u, w, b, g, bl, *, t=512, halo=32, chunk=128):
    bsz, s, dc = u.shape
    assert s % t == 0 and t % halo == 0 and halo >= CONV_WIDTH - 1 and t % chunk == 0
    r = t // halo
    return pl.pallas_call(
        functools.partial(_conv_prompt_kernel, halo=halo, chunk=chunk),
        grid=(bsz, s // t),
        in_specs=[pl.BlockSpec((1, halo, dc), lambda bi, i: (bi, jnp.maximum(i * r - 1, 0), 0)),
                  pl.BlockSpec((1, t, dc), lambda bi, i: (bi, i, 0)),
                  _const_spec(w.shape), _const_spec((1, dc)), _const_spec((1, dc)), _const_spec((1, dc))],
        out_specs=pl.BlockSpec((1, t, dc), lambda bi, i: (bi, i, 0)),
        out_shape=jax.ShapeDtypeStruct((bsz, s, dc), BF16),
        scratch_shapes=[pltpu.VMEM((halo + t + SUBLANES, dc), F32)],
        compiler_params=_cparams(2),
        name="conv_prompt",
    )(u, u, w, b, g, bl)


def _conv_sample_kernel(ext_ref, w_ref, b_ref, g_ref, bl_ref, o_ref):
    t = o_ref.shape[1]
    o_ref[0] = _conv_rows(ext_ref.at[0], 0, t, w_ref, b_ref, g_ref, bl_ref)


def _conv_sample(u_ext, w, b, g, bl):
    n, te, dc = u_ext.shape
    t = te - (CONV_WIDTH - 1)
    return pl.pallas_call(
        _conv_sample_kernel,
        grid=(n,),
        in_specs=[pl.BlockSpec((1, te, dc), lambda i: (i, 0, 0)),
                  _const_spec(w.shape), _const_spec((1, dc)), _const_spec((1, dc)), _const_spec((1, dc))],
        out_specs=pl.BlockSpec((1, t, dc), lambda i: (i, 0, 0)),
        out_shape=jax.ShapeDtypeStruct((n, t, dc), BF16),
        compiler_params=_cparams(1),
        name="conv_sample",
    )(u_ext, w, b, g, bl)


def _memkv_kernel(x_ref, g_ref, w_ref, k_ref, v_ref):
    h = _rmsnorm(x_ref[...], g_ref[...]).astype(BF16)
    kv = _dot(h, w_ref[...])
    d = k_ref.shape[1]
    k_ref[...] = kv[:, :d]
    v_ref[...] = kv[:, d:]


def _memkv(x, g, w, *, tm=256):
    m, d = x.shape
    dm = w.shape[1] // 2
    return pl.pallas_call(
        _memkv_kernel,
        grid=(m // tm,),
        in_specs=[pl.BlockSpec((tm, d), lambda i: (i, 0)), _const_spec((1, d)), _const_spec(w.shape)],
        out_specs=[pl.BlockSpec((tm, dm), lambda i: (i, 0))] * 2,
        out_shape=[jax.ShapeDtypeStruct((m, dm), F32)] * 2,
        compiler_params=_cparams(1),
        name="memkv",
    )(x, g, w)


def _memattn_kernel(q_ref, k_ref, v_ref, o_ref):
    for h in range(MEM_HEADS):
        sl = slice(h * MEM_HEAD_DIM, (h + 1) * MEM_HEAD_DIM)
        s = _dot_nt(q_ref[0, :, sl], k_ref[0, :, sl].astype(BF16)) * (MEM_HEAD_DIM ** -0.5)
        e = jnp.exp(s - jnp.max(s, axis=-1, keepdims=True))
        p = e / jnp.sum(e, axis=-1, keepdims=True)
        o_ref[0, :, sl] = _dot(p.astype(BF16), v_ref[0, :, sl].astype(BF16)).astype(BF16)


def _memattn(q, k, v, *, tq):
    n, t, dm = q.shape
    nm = k.shape[1]
    return pl.pallas_call(
        _memattn_kernel,
        grid=(n, t // tq),
        in_specs=[pl.BlockSpec((1, tq, dm), lambda i, j: (i, j, 0)),
                  pl.BlockSpec((1, nm, dm), lambda i, j: (i, 0, 0)),
                  pl.BlockSpec((1, nm, dm), lambda i, j: (i, 0, 0))],
        out_specs=pl.BlockSpec((1, tq, dm), lambda i, j: (i, j, 0)),
        out_shape=jax.ShapeDtypeStruct((n, t, dm), BF16),
        compiler_params=_cparams(2),
        name="memattn",
    )(q, k, v)


def _ordinal_to_float(o):
    return pltpu.bitcast(jnp.where(o >= 0, o, (-o) | INT_MIN), F32)


def _select_mask(score_ref, key_axis, n_tiles, n_sel, outside, idx_bits, valid_fn, emit_fn):
    _, ta, tb = score_ref.shape
    tile = (ta, tb)
    unit = LANES if key_axis == 1 else 4 * SUBLANES
    tk = tile[key_axis]
    fold = tk // unit
    shp = (ta, LANES) if key_axis == 1 else (unit, tb)
    k_f = jnp.float32(n_sel)
    out_f = jnp.float32(1.0) * outside

    def key_fold(x):
        if key_axis == 0:
            return jnp.sum(x.reshape(fold, unit, tb), axis=0)
        acc = x[:, :LANES]
        for i in range(1, fold):
            acc = acc + x[:, i * LANES:(i + 1) * LANES]
        return acc

    def rep(x):
        return x if fold == 1 else jnp.concatenate([x] * fold, axis=key_axis)

    def row_total(part):
        return jnp.broadcast_to(jnp.sum(part, axis=key_axis, keepdims=True), shp)

    def key_index(j):
        return j * tk + lax.broadcasted_iota(I32, tile, key_axis)

    def count_ge(thr):
        thr_t = rep(thr)

        def body(j, part):
            return part + key_fold(jnp.where(score_ref[j] >= thr_t, 1.0, 0.0))

        part = lax.fori_loop(0, n_tiles, body, jnp.zeros(shp, F32))
        return row_total(part) + jnp.where(thr <= NEG, out_f, 0.0)

    def bit_step(it, carry):
        u, cnt_u = carry
        bit = lax.shift_left(jnp.int32(1), 31 - it)
        cand = u | bit
        cnt = count_ge(_ordinal_to_float(cand ^ INT_MIN))
        ok = cnt >= k_f
        return jnp.where(ok, cand, u), jnp.where(ok, cnt, cnt_u)

    total = jnp.float32(1.0) * (n_tiles * tk) + out_f
    u, cnt_ge = lax.fori_loop(0, 32, bit_step, (jnp.zeros(shp, I32), jnp.broadcast_to(total, shp)))
    thr = _ordinal_to_float(u ^ INT_MIN)
    thr_t = rep(thr)

    surplus = jnp.max(jnp.where(cnt_ge > k_f, 1.0, 0.0)) > 0.0

    def tie_search():
        def count_gt_body(j, part):
            return part + key_fold(jnp.where(score_ref[j] > thr_t, 1.0, 0.0))

        cnt_gt = row_total(lax.fori_loop(0, n_tiles, count_gt_body, jnp.zeros(shp, F32)))
        cnt_gt = cnt_gt + jnp.where(thr < NEG, out_f, 0.0)
        in_tiles = n_tiles * tk

        def idx_step(it, v):
            bit = lax.shift_left(jnp.int32(1), idx_bits - 1 - it)
            cand = v | bit
            cand_t = rep(cand)

            def body(j, part):
                hit = jnp.where(score_ref[j] == thr_t, jnp.where(key_index(j) < cand_t, 1.0, 0.0), 0.0)
                return part + key_fold(hit)

            ties_below = row_total(lax.fori_loop(0, n_tiles, body, jnp.zeros(shp, F32)))
            out_below = jnp.clip(cand - in_tiles, 0, outside).astype(F32)
            ties_below = ties_below + jnp.where(thr == NEG, out_below, 0.0)
            return jnp.where(cnt_gt + ties_below < k_f, cand, v)

        return lax.fori_loop(0, idx_bits, idx_step, jnp.zeros(shp, I32))

    jmax = lax.cond(surplus, tie_search, lambda: jnp.full(shp, 2 ** idx_bits - 1, I32))
    jmax_t = rep(jmax)

    def emit(j, _):
        sc = score_ref[j]
        sel = jnp.where(sc > thr_t, 1.0, jnp.where(sc == thr_t, jnp.where(key_index(j) <= jmax_t, 1.0, 0.0), 0.0))
        emit_fn(j, jnp.where(valid_fn(j, tile), sel, 0.0))
        return 0

    lax.fori_loop(0, n_tiles, emit, 0)


def _sel_prompt_kernel(kib_ref, qit_ref, wit_ref, mask_ref, sc_ref, *, n_sel):
    tq = qit_ref.shape[2]
    s = kib_ref.shape[1]
    n_all = s // tq
    qb = pl.program_id(1)
    n_tiles = qb + 1
    qpos = qb * tq + lax.broadcasted_iota(I32, (tq, tq), 1)

    def score_tile(j):
        ki_t = kib_ref[0, pl.ds(pl.multiple_of(j * tq, tq), tq), :]
        acc = jnp.zeros((tq, tq), F32)
        for h in range(N_IDX_HEADS):
            d = _dot(ki_t, qit_ref[0, h * IDX_DIM:(h + 1) * IDX_DIM, :])
            acc = acc + wit_ref[0, h:h + 1, :] * jnp.maximum(d, 0.0)
        return acc

    def below(j, _):
        sc_ref[j] = score_tile(j)
        return 0

    lax.fori_loop(0, qb, below, 0)
    kpos = qb * tq + lax.broadcasted_iota(I32, (tq, tq), 0)
    sc_ref[qb] = jnp.where(kpos <= qpos, score_tile(qb), NEG)

    def valid(j, shape):
        return j * tq + lax.broadcasted_iota(I32, shape, 0) <= qpos

    def emit(j, sel):
        mask_ref[0, 0, j] = sel.astype(BF16)

    idx_bits = max(1, int(s - 1).bit_length())
    _select_mask(sc_ref, 0, n_tiles, n_sel, (n_all - n_tiles) * tq, idx_bits, valid, emit)

    def clear(j, _):
        mask_ref[0, 0, j] = jnp.zeros((tq, tq), BF16)
        return 0

    lax.fori_loop(n_tiles, n_all, clear, 0)


def _sel_prompt(kib, qitb, wit, *, tq, n_sel):
    bsz, s, _ = kib.shape
    nq = s // tq
    return pl.pallas_call(
        functools.partial(_sel_prompt_kernel, n_sel=n_sel),
        grid=(bsz, nq),
        in_specs=[pl.BlockSpec((1, s, kib.shape[2]), lambda b, i: (b, 0, 0)),
                  pl.BlockSpec((1, qitb.shape[1], tq), lambda b, i: (b, 0, i)),
                  pl.BlockSpec((1, wit.shape[1], tq), lambda b, i: (b, 0, i))],
        out_specs=pl.BlockSpec((1, 1, nq, tq, tq), lambda b, i: (b, i, 0, 0, 0)),
        out_shape=jax.ShapeDtypeStruct((bsz, nq, nq, tq, tq), BF16),
        scratch_shapes=[pltpu.VMEM((nq, tq, tq), F32)],
        compiler_params=_cparams(2),
        name="sel_prompt",
    )(kib, qitb, wit)


def _attn_prompt_kernel(qt_ref, k_ref, vt_ref, mask_ref, o_ref, q2_ref, s_ref, mx_ref, l_ref, out_ref):
    tq = qt_ref.shape[2]
    n_tiles = pl.program_id(1) + 1
    pair = 2 * HEAD_DIM
    part = 4 * SUBLANES
    fold = tq // part
    in_pair = lax.broadcasted_iota(I32, (pair, tq), 0)
    for hp in range(N_HEADS // 2):
        qp = qt_ref[0, hp * pair:(hp + 1) * pair, :]
        q2_ref[2 * hp] = jnp.where(in_pair < HEAD_DIM, qp, jnp.zeros_like(qp))
        q2_ref[2 * hp + 1] = jnp.where(in_pair >= HEAD_DIM, qp, jnp.zeros_like(qp))

    def key_rows(j):
        return pl.ds(pl.multiple_of(j * tq, tq), tq)

    group = s_ref.shape[1]
    n_groups = N_HEADS // group

    def score_tile(j, sel, slot, h0):
        for g in range(group):
            h = h0 + g
            k_pair = k_ref[0, key_rows(j), (h // 2) * pair:(h // 2 + 1) * pair]
            s = jnp.where(sel, _dot(k_pair, q2_ref[h]) * LOG2E, NEG)
            s_ref[slot, g, j] = s
            mx_ref[slot, g] = jnp.maximum(mx_ref[slot, g], jnp.max(s.reshape(fold, part, tq), axis=0))

    def weigh_tile(j, slot, h0, m):
        for g in range(group):
            rows = slice((h0 + g) * HEAD_DIM, (h0 + g + 1) * HEAD_DIM)
            p = jnp.exp2(s_ref[slot, g, j] - m[g])
            l_ref[g] = l_ref[g] + jnp.sum(p.reshape(fold, part, tq), axis=0)
            out_ref[rows, :] = out_ref[rows, :] + _dot(vt_ref[0, rows, key_rows(j)], p.astype(BF16))

    for gi in range(n_groups + 1):
        slot, prev = gi % 2, (gi - 1) % 2
        h0, h_prev = gi * group, (gi - 1) * group
        m_prev = None
        if gi > 0:
            m_prev = [jnp.max(mx_ref[prev, g], axis=0, keepdims=True) for g in range(group)]
            l_ref[...] = jnp.zeros(l_ref.shape, F32)
            out_ref[h_prev * HEAD_DIM:h0 * HEAD_DIM, :] = jnp.zeros((group * HEAD_DIM, tq), F32)
        if gi < n_groups:
            mx_ref[slot] = jnp.full(mx_ref.shape[1:], NEG, F32)

        def sweep(jj, _, gi=gi, slot=slot, prev=prev, h0=h0, h_prev=h_prev, m_prev=m_prev):
            for i in range(TILES_PER_STEP):
                j = jj * TILES_PER_STEP + i
                if gi < n_groups:
                    score_tile(j, mask_ref[0, 0, j].astype(F32) > 0.0, slot, h0)
                if gi > 0:
                    weigh_tile(j, prev, h_prev, m_prev)
            return 0

        lax.fori_loop(0, pl.cdiv(n_tiles, TILES_PER_STEP), sweep, 0)
        if gi > 0:
            for g in range(group):
                rows = slice((h_prev + g) * HEAD_DIM, (h_prev + g + 1) * HEAD_DIM)
                out_ref[rows, :] = out_ref[rows, :] / jnp.sum(l_ref[g], axis=0, keepdims=True)
    o_ref[0] = out_ref[...].T.astype(BF16)


def _attn_prompt(qtb, kb, vtb, mask, *, tq, group=2):
    bsz, da, s = qtb.shape
    nq = s // tq
    part = 4 * SUBLANES
    return pl.pallas_call(
        _attn_prompt_kernel,
        grid=(bsz, nq),
        in_specs=[pl.BlockSpec((1, da, tq), lambda b, i: (b, 0, i)),
                  pl.BlockSpec((1, s, da), lambda b, i: (b, 0, 0)),
                  pl.BlockSpec((1, da, s), lambda b, i: (b, 0, 0)),
                  pl.BlockSpec((1, 1, nq, tq, tq), lambda b, i: (b, i, 0, 0, 0))],
        out_specs=pl.BlockSpec((1, tq, da), lambda b, i: (b, i, 0)),
        out_shape=jax.ShapeDtypeStruct((bsz, s, da), BF16),
        scratch_shapes=[pltpu.VMEM((N_HEADS, 2 * HEAD_DIM, tq), BF16), pltpu.VMEM((2, group, nq, tq, tq), F32),
                        pltpu.VMEM((2, group, part, tq), F32), pltpu.VMEM((group, part, tq), F32),
                        pltpu.VMEM((da, tq), F32)],
        compiler_params=_cparams(2),
        name="attn_prompt",
    )(qtb, kb, vtb, mask)


def _score_sample_kernel(pt_ref, qi_ref, wi_ref, kin_ref, cache_ref, past_ref, new_ref, buf_ref, sem_ref, *, chunk):
    seq = pl.program_id(0)
    n_seq = pl.num_programs(0)
    n_pages = buf_ref.shape[1]
    t = new_ref.shape[1]
    slot = lax.rem(seq, 2)

    def page_copy(s, i, buf_slot):
        return pltpu.make_async_copy(cache_ref.at[pt_ref[s, i]], buf_ref.at[buf_slot, i], sem_ref.at[buf_slot])

    def start_pages(s, buf_slot):
        def body(i, _):
            page_copy(s, i, buf_slot).start()
            return 0
        lax.fori_loop(0, n_pages, body, 0)

    @pl.when(seq == 0)
    def _():
        start_pages(seq, slot)

    @pl.when(seq + 1 < n_seq)
    def _():
        start_pages(seq + 1, 1 - slot)

    def wait_body(i, _):
        page_copy(seq, i, slot).wait()
        return 0

    lax.fori_loop(0, n_pages, wait_body, 0)

    qi = qi_ref[0]
    wi = wi_ref[0]

    def scores(dots):
        w = wi * jnp.maximum(dots, 0.0)
        acc = w[0:t]
        for h in range(1, N_IDX_HEADS):
            acc = acc + w[h * t:(h + 1) * t]
        return acc

    for c in range(n_pages // chunk):
        keys_t = jnp.concatenate([buf_ref[slot, c * chunk + i].astype(BF16) for i in range(chunk)], axis=1)
        past_ref[0, :, c * chunk * PAGE_SIZE:(c + 1) * chunk * PAGE_SIZE] = scores(_dot(qi, keys_t))

    ext = jnp.concatenate([kin_ref[0], jnp.zeros((LANES - t, IDX_DIM), F32)], axis=0).astype(BF16)
    sc = scores(_dot_nt(qi, ext))
    qpos = lax.broadcasted_iota(I32, (t, LANES), 0)
    kpos = lax.broadcasted_iota(I32, (t, LANES), 1)
    new_ref[0] = jnp.where(kpos < t, jnp.where(kpos <= qpos, sc, NEG), -jnp.inf)


def _score_sample(page_table, qi_hm, wi_hm, ki_new, cache_idx_k, *, chunk=16):
    n, n_pages = page_table.shape
    t = ki_new.shape[1]
    ht = qi_hm.shape[1]
    grid_spec = pltpu.PrefetchScalarGridSpec(
        num_scalar_prefetch=1,
        grid=(n,),
        in_specs=[pl.BlockSpec((1, ht, IDX_DIM), lambda b, pt: (b, 0, 0)),
                  pl.BlockSpec((1, ht, 1), lambda b, pt: (b, 0, 0)),
                  pl.BlockSpec((1, t, IDX_DIM), lambda b, pt: (b, 0, 0)),
                  pl.BlockSpec(memory_space=pl.ANY)],
        out_specs=[pl.BlockSpec((1, t, n_pages * PAGE_SIZE), lambda b, pt: (b, 0, 0)),
                   pl.BlockSpec((1, t, LANES), lambda b, pt: (b, 0, 0))],
        scratch_shapes=[pltpu.VMEM((2, n_pages, IDX_DIM, PAGE_SIZE), F32), pltpu.SemaphoreType.DMA((2,))],
    )
    return pl.pallas_call(
        functools.partial(_score_sample_kernel, chunk=chunk),
        grid_spec=grid_spec,
        out_shape=[jax.ShapeDtypeStruct((n, t, n_pages * PAGE_SIZE), F32),
                   jax.ShapeDtypeStruct((n, t, LANES), F32)],
        compiler_params=_cparams(1),
        name="score_sample",
    )(page_table, qi_hm, wi_hm, ki_new, cache_idx_k)


def _sel_sample_kernel(past_ref, new_ref, mpast_ref, mnew_ref, key_ref, mask_ref, *, n_sel, t, tile):
    rows = past_ref.shape[0]
    n_past = past_ref.shape[1] // tile
    for j in range(n_past):
        key_ref[j] = past_ref[:, j * tile:(j + 1) * tile]
    pad = jnp.full((rows, tile - LANES), -jnp.inf, F32)
    key_ref[n_past] = jnp.concatenate([new_ref[...], pad], axis=1)
    qpos = lax.rem(lax.broadcasted_iota(I32, (rows, tile), 0), t)

    def valid(j, shape):
        kpos = lax.broadcasted_iota(I32, shape, 1)
        return kpos <= qpos + jnp.minimum(n_past - j, 1) * tile

    def emit(j, sel):
        mask_ref[j] = sel

    idx_bits = int((n_past + 1) * tile - 1).bit_length()
    _select_mask(key_ref, 1, n_past + 1, n_sel, 0, idx_bits, valid, emit)
    for j in range(n_past):
        mpast_ref[:, j * tile:(j + 1) * tile] = mask_ref[j]
    mnew_ref[...] = mask_ref[n_past][:, :LANES]


def _sel_sample(key_past, key_new, *, n_sel, t, rows=64, tile=2048):
    r, n_keys = key_past.shape
    n_t = n_keys // tile + 1
    return pl.pallas_call(
        functools.partial(_sel_sample_kernel, n_sel=n_sel, t=t, tile=tile),
        grid=(r // rows,),
        in_specs=[pl.BlockSpec((rows, n_keys), lambda i: (i, 0)), pl.BlockSpec((rows, LANES), lambda i: (i, 0))],
        out_specs=[pl.BlockSpec((rows, n_keys), lambda i: (i, 0)), pl.BlockSpec((rows, LANES), lambda i: (i, 0))],
        out_shape=[jax.ShapeDtypeStruct((r, n_keys), F32), jax.ShapeDtypeStruct((r, LANES), F32)],
        scratch_shapes=[pltpu.VMEM((n_t, rows, tile), F32), pltpu.VMEM((n_t, rows, tile), F32)],
        compiler_params=_cparams(1),
        name="sel_sample",
    )(key_past, key_new)


def _attn_sample_kernel(pt_ref, q_ref, kn_ref, vn_ref, mpast_ref, mnew_ref, *rest, pages):
    k_refs = rest[:pages]
    v_refs = rest[pages:2 * pages]
    o_ref, qbd_ref, m_ref, l_ref, acc_ref = rest[2 * pages:]
    t, da = q_ref.shape[1], q_ref.shape[2]
    ht = N_HEADS * t
    j = pl.program_id(1)

    @pl.when(j == 0)
    def _():
        q_rep = jnp.concatenate([q_ref[0].astype(F32)] * N_HEADS, axis=0)
        head_of_row = lax.broadcasted_iota(I32, (ht, da), 0) // t
        head_of_col = lax.broadcasted_iota(I32, (ht, da), 1) // HEAD_DIM
        qbd_ref[...] = jnp.where(head_of_row == head_of_col, q_rep, 0.0).astype(BF16)
        m_ref[...] = jnp.full(m_ref.shape, NEG, F32)
        l_ref[...] = jnp.zeros(l_ref.shape, F32)
        acc_ref[...] = jnp.zeros(acc_ref.shape, F32)

    def attend(k_pages, v_pages, mask_t, feature_major):
        qbd = qbd_ref[...]
        qk = [(_dot(qbd, kp) if feature_major else _dot_nt(qbd, kp)) for kp in k_pages]
        sel = jnp.concatenate([mask_t] * N_HEADS, axis=0) > 0.0
        s = jnp.where(sel, qk[0] if len(qk) == 1 else jnp.concatenate(qk, axis=1), NEG)
        m_old = m_ref[...]
        m_new = jnp.maximum(m_old, jnp.max(s, axis=-1, keepdims=True))
        alpha = jnp.exp(m_old - m_new)
        p = jnp.where(sel, jnp.exp(s - m_new), 0.0)
        l_ref[...] = alpha * l_ref[...] + jnp.sum(p, axis=-1, keepdims=True)
        pb = p.astype(BF16)
        pv = None
        for i, vp in enumerate(v_pages):
            p_i = pb[:, i * PAGE_SIZE:(i + 1) * PAGE_SIZE]
            term = _dot_nt(p_i, vp) if feature_major else _dot(p_i, vp)
            pv = term if pv is None else pv + term
        acc_ref[...] = alpha * acc_ref[...] + pv
        m_ref[...] = m_new

    attend([r[0].astype(BF16) for r in k_refs], [r[0].astype(BF16) for r in v_refs], mpast_ref[0], True)

    @pl.when(j == pl.num_programs(1) - 1)
    def _():
        zeros = jnp.zeros((PAGE_SIZE - t, da), F32)
        attend([jnp.concatenate([kn_ref[0], zeros], axis=0).astype(BF16)],
               [jnp.concatenate([vn_ref[0], zeros], axis=0).astype(BF16)], mnew_ref[0], False)
        out = acc_ref[...] / l_ref[...]
        o_ref[0] = jnp.concatenate(
            [out[h * t:(h + 1) * t, h * HEAD_DIM:(h + 1) * HEAD_DIM] for h in range(N_HEADS)], axis=1).astype(BF16)


def _attn_sample(page_table, q, k_new, v_new, mask_past, mask_new, cache_k, cache_v, *, pages):
    n, n_pages = page_table.shape
    t, da = q.shape[1], q.shape[2]
    steps = n_pages // pages
    ht = N_HEADS * t

    def page_spec(i):
        return pl.BlockSpec((1, da, PAGE_SIZE), lambda b, j, pt: (pt[b, j * pages + i], 0, 0))

    per_seq = lambda shape: pl.BlockSpec((1,) + shape, lambda b, j, pt: (b, 0, 0))
    grid_spec = pltpu.PrefetchScalarGridSpec(
        num_scalar_prefetch=1,
        grid=(n, steps),
        in_specs=[per_seq((t, da)), per_seq((t, da)), per_seq((t, da)),
                  pl.BlockSpec((1, t, pages * PAGE_SIZE), lambda b, j, pt: (b, 0, j)), per_seq((t, LANES))]
                 + [page_spec(i) for i in range(pages)] + [page_spec(i) for i in range(pages)],
        out_specs=per_seq((t, da)),
        scratch_shapes=[pltpu.VMEM((ht, da), BF16), pltpu.VMEM((ht, 1), F32), pltpu.VMEM((ht, 1), F32),
                        pltpu.VMEM((ht, da), F32)],
    )
    return pl.pallas_call(
        functools.partial(_attn_sample_kernel, pages=pages),
        grid_spec=grid_spec,
        out_shape=jax.ShapeDtypeStruct((n, t, da), BF16),
        compiler_params=_cparams(2),
        name="attn_sample",
    )(page_table, q, k_new, v_new, mask_past, mask_new, *([cache_k] * pages), *([cache_v] * pages))


def _merge_kernel(x_ref, h_ref, c_ref, a_ref, m_ref, wg_ref, wc_ref, wa_ref, wm_ref, wo_ref, o_ref):
    d = x_ref.shape[1]
    gates = jax.nn.sigmoid(_dot(h_ref[...], wg_ref[...]))
    mix = (gates[:, :d] * _dot(c_ref[...], wc_ref[...])
           + gates[:, d:2 * d] * _dot(a_ref[...], wa_ref[...])
           + gates[:, 2 * d:] * _dot(m_ref[...], wm_ref[...]))
    o_ref[...] = x_ref[...] + _dot(mix.astype(BF16), wo_ref[...])


def _merge(x, h, c, a, mo, wg, wc, wa, wm, wo, *, tm):
    m, d = x.shape
    row = lambda n: pl.BlockSpec((tm, n), lambda i: (i, 0))
    return pl.pallas_call(
        _merge_kernel,
        grid=(m // tm,),
        in_specs=[row(d), row(d), row(c.shape[1]), row(a.shape[1]), row(mo.shape[1]),
                  _const_spec(wg.shape), _const_spec(wc.shape), _const_spec(wa.shape), _const_spec(wm.shape),
                  _const_spec(wo.shape)],
        out_specs=row(d),
        out_shape=jax.ShapeDtypeStruct((m, d), F32),
        compiler_params=_cparams(1),
        name="merge",
    )(x, h, c, a, mo, wg, wc, wa, wm, wo)


def _ffn_kernel(x_ref, g_ref, wi_ref, wo_ref, gf_ref, o_ref):
    x = x_ref[...]
    dff = wo_ref.shape[0]
    hn = _rmsnorm(x, g_ref[...]).astype(BF16)
    a = _dot(hn, wi_ref[:, :dff])
    b = _dot(hn, wi_ref[:, dff:])
    y = x + _dot((a * jax.nn.sigmoid(a) * b).astype(BF16), wo_ref[...])
    o_ref[...] = _rmsnorm(y, gf_ref[...])


def _ffn(x, g, w_in, w_out, g_final, *, tm):
    m, d = x.shape
    return pl.pallas_call(
        _ffn_kernel,
        grid=(m // tm,),
        in_specs=[pl.BlockSpec((tm, d), lambda i: (i, 0)), _const_spec((1, d)), _const_spec(w_in.shape),
                  _const_spec(w_out.shape), _const_spec((1, d))],
        out_specs=pl.BlockSpec((tm, d), lambda i: (i, 0)),
        out_shape=jax.ShapeDtypeStruct((m, d), F32),
        compiler_params=_cparams(1),
        name="ffn",
    )(x, g, w_in, w_out, g_final)


def kernel(x_prompt, x_sample, mem_prompt, cache_conv, cache_k, cache_v, cache_idx_k, cache_mem_k, cache_mem_v,
           page_table, g_mix, w_in, w_conv_dw, b_conv_dw, g_conv_ln, b_conv_ln, w_conv_out, w_att_out, g_mem,
           w_mem_kv, w_mem_out, w_out, g_ffn, w_ffn_in, w_ffn_out, g_final):
    depth = g_mix.shape[0]
    assert depth == 1
    bsz, seq, d = x_prompt.shape
    n_dec, t_dec, _ = x_sample.shape
    n_mem = mem_prompt.shape[1]
    d_conv = w_conv_dw.shape[2]
    d_att = N_HEADS * HEAD_DIM
    d_idx = N_IDX_HEADS * IDX_DIM
    d_mem = MEM_HEADS * MEM_HEAD_DIM
    n_phys = cache_k.shape[1]
    past = page_table.shape[1] * PAGE_SIZE
    l = 0

    splits = [2 * d_conv, d_att, d_att, d_att, d_idx, IDX_DIM, N_IDX_HEADS, d_mem, 3 * d]
    offs = np.concatenate([[0], np.cumsum(splits)])
    w = w_in[l]
    col = lambda i: w[:, offs[i]:offs[i + 1]]
    pad_cols = lambda a, n: jnp.pad(a, ((0, 0), (0, n - a.shape[1])))
    glu, wq, wk, wv, wqi, wki, wwi, wqm, wgates = (col(i) for i in range(9))
    shared = [glu, wqm, pad_cols(wki, LANES)]
    w_prompt = jnp.concatenate(shared, axis=1).astype(BF16)
    w_prompt_t = jnp.concatenate([wk, wv, wki, pad_cols(wwi, WI_ROWS), wqi, wq], axis=1).T.astype(BF16)
    w_sample = jnp.concatenate(shared + [wq, wqi, pad_cols(wwi, LANES), wk, wv], axis=1).astype(BF16)
    w_gates = wgates.astype(BF16)
    bf = lambda a: a[l].astype(BF16)
    row2 = lambda a: a.reshape(1, -1)
    wdw, bdw, gln, bln = w_conv_dw[l], row2(b_conv_dw[l]), row2(g_conv_ln[l]), row2(b_conv_ln[l])
    dims = dict(d_conv=d_conv, d_att=d_att, d_idx=d_idx, d_mem=d_mem)
    merge = functools.partial(_merge, wg=w_gates, wc=bf(w_conv_out), wa=bf(w_att_out), wm=bf(w_mem_out),
                              wo=bf(w_out))
    ffn = functools.partial(_ffn, g=row2(g_ffn[l]), w_in=bf(w_ffn_in), w_out=bf(w_ffn_out),
                            g_final=row2(g_final))

    mp = bsz * seq
    xp = x_prompt.reshape(mp, d)
    u, qm, h, kib, kb, kt, vt, vtb, kit, qitb, wit, qtb = _inproj_prompt(x_prompt, row2(g_mix[l]), w_prompt,
                                                                         w_prompt_t, tm=512, **dims)
    per_seq = lambda a: a.reshape(bsz, seq, a.shape[-1])
    u3 = per_seq(u)
    c = _conv_prompt(u3, wdw, bdw, gln, bln)
    n_sel = min(TOPK_MAX, seq // 4)
    tq = 256
    mask = _sel_prompt(per_seq(kib), qitb, wit, tq=tq, n_sel=n_sel)
    oa = _attn_prompt(qtb, per_seq(kb), vtb, mask, tq=tq)
    mk, mv = _memkv(mem_prompt.reshape(bsz * n_mem, d), row2(g_mem[l]), bf(w_mem_kv))
    om = _memattn(per_seq(qm), mk.reshape(bsz, n_mem, d_mem), mv.reshape(bsz, n_mem, d_mem), tq=512)
    x1 = merge(xp, h, c.reshape(mp, d_conv), oa.reshape(mp, d_att), om.reshape(mp, d_mem), tm=512)
    y_prompt = ffn(x1, tm=256).reshape(bsz, seq, d)
    conv_state_prompt = u3[:, seq - (CONV_WIDTH - 1):][None]
    k_prompt = kt.reshape(bsz, N_HEADS, HEAD_DIM, seq).transpose(0, 3, 1, 2)[None]
    v_prompt = vt.reshape(bsz, N_HEADS, HEAD_DIM, seq).transpose(0, 3, 1, 2)[None]
    idx_k_prompt = kit.transpose(0, 2, 1)[None]
    mem_k_prompt = mk.reshape(1, bsz, n_mem, MEM_HEADS, MEM_HEAD_DIM)
    mem_v_prompt = mv.reshape(1, bsz, n_mem, MEM_HEADS, MEM_HEAD_DIM)

    ms = n_dec * t_dec
    xs = x_sample.reshape(ms, d)
    u, q, qm, h, qi, wi, k, v, ki = _inproj_sample(xs, row2(g_mix[l]), w_sample, **dims)
    per_seq = lambda a: a.reshape(n_dec, t_dec, a.shape[-1])
    u_ext = jnp.concatenate([cache_conv[l], per_seq(u)], axis=1)
    c = _conv_sample(u_ext, wdw, bdw, gln, bln)
    n_sel = min(TOPK_MAX, (past + t_dec) // 4)
    qi_hm = per_seq(qi).reshape(n_dec, t_dec, N_IDX_HEADS, IDX_DIM).transpose(0, 2, 1, 3)
    qi_hm = qi_hm.reshape(n_dec, N_IDX_HEADS * t_dec, IDX_DIM)
    wi_hm = per_seq(wi).transpose(0, 2, 1).reshape(n_dec, N_IDX_HEADS * t_dec, 1)
    page_major = lambda a: jnp.moveaxis(a[l], 1, -1).reshape(n_phys, -1, PAGE_SIZE)
    key_past, key_new = _score_sample(page_table, qi_hm, wi_hm, per_seq(ki), page_major(cache_idx_k))
    mask_past, mask_new = _sel_sample(key_past.reshape(ms, past), key_new.reshape(ms, LANES), n_sel=n_sel, t=t_dec)
    oa = _attn_sample(page_table, per_seq(q), per_seq(k), per_seq(v), mask_past.reshape(n_dec, t_dec, past),
                      mask_new.reshape(n_dec, t_dec, LANES), page_major(cache_k), page_major(cache_v), pages=32)
    om = _memattn(per_seq(qm), cache_mem_k[l].reshape(n_dec, n_mem, d_mem),
                  cache_mem_v[l].reshape(n_dec, n_mem, d_mem), tq=t_dec)
    x1 = merge(xs, h, c.reshape(ms, d_conv), oa.reshape(ms, d_att), om.reshape(ms, d_mem), tm=ms)
    y_sample = ffn(x1, tm=ms).reshape(n_dec, t_dec, d)
    conv_state_sample = u_ext[:, t_dec:][None]
    k_sample = k.reshape(1, n_dec, t_dec, N_HEADS, HEAD_DIM)
    v_sample = v.reshape(1, n_dec, t_dec, N_HEADS, HEAD_DIM)
    idx_k_sample = ki.reshape(1, n_dec, t_dec, IDX_DIM)

    return (y_prompt, y_sample, conv_state_prompt, k_prompt, v_prompt, idx_k_prompt, mem_k_prompt, mem_v_prompt,
            conv_state_sample, k_sample, v_sample, idx_k_sample)
```

```python
import functools

import jax
import jax.numpy as jnp
import numpy as np
from jax import lax
from jax.experimental import pallas as pl
from jax.experimental.pallas import tpu as pltpu

EPS = 1e-6
NEG = -1e30
CONV_WIDTH = 31
N_HEADS = 8
HEAD_DIM = 64
N_IDX_HEADS = 8
IDX_DIM = 64
MEM_HEADS = 4
MEM_HEAD_DIM = 128
TOPK_MAX = 256
PAGE_SIZE = 128
LANES = 128
SUBLANES = 8
VMEM_LIMIT = 56 * 1024 * 1024
INT_MIN = -(2 ** 31)
LOG2E = 1.4426950408889634
TILES_PER_STEP = 2

BF16 = jnp.bfloat16
F32 = jnp.float32
I32 = jnp.int32


def _cparams(n_axes):
    return pltpu.CompilerParams(dimension_semantics=("arbitrary",) * n_axes, vmem_limit_bytes=VMEM_LIMIT)


def _const_spec(shape):
    zeros = (0,) * len(shape)
    return pl.BlockSpec(shape, lambda *_: zeros)


def _rmsnorm(x, g):
    return x * lax.rsqrt(jnp.mean(x * x, axis=-1, keepdims=True) + EPS) * g


def _dot(a, b):
    return jnp.dot(a, b, preferred_element_type=F32)


def _dot_nt(a, b):
    return lax.dot_general(a, b, (((1,), (1,)), ((), ())), preferred_element_type=F32)


WI_ROWS = 16


def _glu_qm(h, w_ref, u_ref, qm_ref, d_conv, d_mem):
    o = 0
    glu = _dot(h, w_ref[:, o:o + 2 * d_conv]); o += 2 * d_conv
    u_ref[...] = glu[:, :d_conv] * jax.nn.sigmoid(glu[:, d_conv:])
    qm_ref[...] = _dot(h, w_ref[:, o:o + d_mem]).astype(BF16); o += d_mem
    return o


def _inproj_prompt_kernel(x_ref, g_ref, w_ref, wt_ref, u_ref, qm_ref, h_ref, kib_ref, kb_ref,
                          kt_ref, vt_ref, vtb_ref, kit_ref, qit_ref, wit_ref, qt_ref, *, d_conv, d_att, d_idx, d_mem):
    h = _rmsnorm(x_ref[0], g_ref[...]).astype(BF16)
    h_ref[...] = h
    o = _glu_qm(h, w_ref, u_ref, qm_ref, d_conv, d_mem)
    kib_ref[...] = _dot(h, w_ref[:, o:o + LANES])[:, :IDX_DIM].astype(BF16)
    r = 0
    kt = _dot_nt(wt_ref[r:r + d_att, :], h); r += d_att
    kt_ref[0] = kt
    kb_ref[...] = kt.T.astype(BF16)
    vt = _dot_nt(wt_ref[r:r + d_att, :], h); r += d_att
    vt_ref[0] = vt
    vtb_ref[0] = vt.astype(BF16)
    kit_ref[0] = _dot_nt(wt_ref[r:r + IDX_DIM, :], h); r += IDX_DIM
    wit_ref[0] = _dot_nt(wt_ref[r:r + WI_ROWS, :], h)[:N_IDX_HEADS] * (N_IDX_HEADS ** -0.5); r += WI_ROWS
    qit_ref[0] = (_dot_nt(wt_ref[r:r + d_idx, :], h) * (IDX_DIM ** -0.5)).astype(BF16); r += d_idx
    qt_ref[0] = (_dot_nt(wt_ref[r:r + d_att, :], h) * (HEAD_DIM ** -0.5)).astype(BF16)


def _inproj_prompt(x, g, w, wt, *, tm, d_conv, d_att, d_idx, d_mem):
    bsz, s, d = x.shape
    nt = s // tm
    row = lambda n: pl.BlockSpec((tm, n), lambda b, i: (b * nt + i, 0))
    col = lambda n: pl.BlockSpec((1, n, tm), lambda b, i: (b, 0, i))
    outs = [(d_conv, F32), (d_mem, BF16), (d, BF16), (IDX_DIM, BF16), (d_att, BF16)]
    outs_t = [(d_att, F32), (d_att, F32), (d_att, BF16), (IDX_DIM, F32), (d_idx, BF16), (N_IDX_HEADS, F32),
              (d_att, BF16)]
    return pl.pallas_call(
        functools.partial(_inproj_prompt_kernel, d_conv=d_conv, d_att=d_att, d_idx=d_idx, d_mem=d_mem),
        grid=(bsz, nt),
        in_specs=[pl.BlockSpec((1, tm, d), lambda b, i: (b, i, 0)), _const_spec((1, d)), _const_spec(w.shape),
                  _const_spec(wt.shape)],
        out_specs=[row(n) for n, _ in outs] + [col(n) for n, _ in outs_t],
        out_shape=[jax.ShapeDtypeStruct((bsz * s, n), dt) for n, dt in outs]
                  + [jax.ShapeDtypeStruct((bsz, n, s), dt) for n, dt in outs_t],
        compiler_params=_cparams(2),
        name="inproj_prompt",
    )(x, g, w, wt)


def _inproj_sample_kernel(x_ref, g_ref, w_ref, u_ref, q_ref, qm_ref, h_ref, qi_ref, wi_ref, k_ref, v_ref, ki_ref,
                          *, d_conv, d_att, d_idx, d_mem):
    h = _rmsnorm(x_ref[...], g_ref[...]).astype(BF16)
    h_ref[...] = h
    o = _glu_qm(h, w_ref, u_ref, qm_ref, d_conv, d_mem)
    ki_ref[...] = _dot(h, w_ref[:, o:o + LANES])[:, :IDX_DIM]; o += LANES
    q_ref[...] = (_dot(h, w_ref[:, o:o + d_att]) * (HEAD_DIM ** -0.5)).astype(BF16); o += d_att
    qi_ref[...] = (_dot(h, w_ref[:, o:o + d_idx]) * (IDX_DIM ** -0.5)).astype(BF16); o += d_idx
    wi_ref[...] = _dot(h, w_ref[:, o:o + LANES])[:, :N_IDX_HEADS] * (N_IDX_HEADS ** -0.5); o += LANES
    k_ref[...] = _dot(h, w_ref[:, o:o + d_att]); o += d_att
    v_ref[...] = _dot(h, w_ref[:, o:o + d_att])


def _inproj_sample(x, g, w, *, d_conv, d_att, d_idx, d_mem):
    m, d = x.shape
    outs = [(d_conv, F32), (d_att, BF16), (d_mem, BF16), (d, BF16), (d_idx, BF16), (N_IDX_HEADS, F32),
            (d_att, F32), (d_att, F32), (IDX_DIM, F32)]
    return pl.pallas_call(
        functools.partial(_inproj_sample_kernel, d_conv=d_conv, d_att=d_att, d_idx=d_idx, d_mem=d_mem),
        grid=(1,),
        in_specs=[_const_spec((m, d)), _const_spec((1, d)), _const_spec(w.shape)],
        out_specs=[_const_spec((m, n)) for n, _ in outs],
        out_shape=[jax.ShapeDtypeStruct((m, n), dt) for n, dt in outs],
        compiler_params=_cparams(1),
        name="inproj_sample",
    )(x, g, w)


def _conv_taps(ext_ref, start, rows, w_ref, by_phase):
    if not by_phase:
        acc = w_ref[0:1, :] * ext_ref[pl.ds(start, rows), :]
        for j in range(1, CONV_WIDTH):
            acc = acc + w_ref[j:j + 1, :] * ext_ref[pl.ds(start + j, rows), :]
        return acc
    acc = None
    for r in range(SUBLANES):
        y = None
        for j in range(CONV_WIDTH):
            if (start + j) % SUBLANES == r:
                term = w_ref[j:j + 1, :] * ext_ref[pl.ds(start + j - r, rows + (SUBLANES if r else 0)), :]
                y = term if y is None else y + term
        if y is not None:
            y = y[r:r + rows] if r else y
            acc = y if acc is None else acc + y
    return acc


def _conv_rows(ext_ref, start, rows, w_ref, b_ref, g_ref, bl_ref, by_phase=False):
    c = _conv_taps(ext_ref, start, rows, w_ref, by_phase) + b_ref[...]
    mu = jnp.mean(c, axis=-1, keepdims=True)
    xc = c - mu
    y = xc * lax.rsqrt(jnp.mean(xc * xc, axis=-1, keepdims=True) + EPS) * g_ref[...] + bl_ref[...]
    return (y * jax.nn.sigmoid(y)).astype(BF16)


def _conv_prompt_kernel(prev_ref, cur_ref, w_ref, b_ref, g_ref, bl_ref, o_ref, ext_ref, *, halo, chunk):
    t = cur_ref.shape[1]

    @pl.when(pl.program_id(1) == 0)
    def _():
        ext_ref[0:halo, :] = jnp.zeros((halo, ext_ref.shape[1]), F32)

    @pl.when(pl.program_id(1) > 0)
    def _():
        ext_ref[0:halo, :] = prev_ref[0]

    ext_ref[halo:halo + t, :] = cur_ref[0]
    ext_ref[halo + t:, :] = jnp.zeros((SUBLANES, ext_ref.shape[1]), F32)
    first = halo - (CONV_WIDTH - 1)
    for c in range(t // chunk):
        o_ref[0, c * chunk:(c + 1) * chunk, :] = _conv_rows(ext_ref, first + c * chunk, chunk, w_ref, b_ref, g_ref,
                                                            bl_ref, by_phase=True)


def _conv_prompt(u, w, b, g, bl, *, t=512, halo=32, chunk=128):
    bsz, s, dc = u.shape
    assert s % t == 0 and t % halo == 0 and halo >= CONV_WIDTH - 1 and t % chunk == 0
    r = t // halo
    return pl.pallas_call(
        functools.partial(_conv_prompt_kernel, halo=halo, chunk=chunk),
        grid=(bsz, s // t),
        in_specs=[pl.BlockSpec((1, halo, dc), lambda bi, i: (bi, jnp.maximum(i * r - 1, 0), 0)),
                  pl.BlockSpec((1, t, dc), lambda bi, i: (bi, i, 0)),
                  _const_spec(w.shape), _const_spec((1, dc)), _const_spec((1, dc)), _const_spec((1, dc))],
        out_specs=pl.BlockSpec((1, t, dc), lambda bi, i: (bi, i, 0)),
        out_shape=jax.ShapeDtypeStruct((bsz, s, dc), BF16),
        scratch_shapes=[pltpu.VMEM((halo + t + SUBLANES, dc), F32)],
        compiler_params=_cparams(2),
        name="conv_prompt",
    )(u, u, w, b, g, bl)


def _conv_sample_kernel(ext_ref, w_ref, b_ref, g_ref, bl_ref, o_ref):
    t = o_ref.shape[1]
    for s in range(o_ref.shape[0]):
        o_ref[s] = _conv_rows(ext_ref.at[s], 0, t, w_ref, b_ref, g_ref, bl_ref)


def _conv_sample(u_ext, w, b, g, bl, *, seqs=8):
    n, te, dc = u_ext.shape
    t = te - (CONV_WIDTH - 1)
    return pl.pallas_call(
        _conv_sample_kernel,
        grid=(n // seqs,),
        in_specs=[pl.BlockSpec((seqs, te, dc), lambda i: (i, 0, 0)),
                  _const_spec(w.shape), _const_spec((1, dc)), _const_spec((1, dc)), _const_spec((1, dc))],
        out_specs=pl.BlockSpec((seqs, t, dc), lambda i: (i, 0, 0)),
        out_shape=jax.ShapeDtypeStruct((n, t, dc), BF16),
        compiler_params=_cparams(1),
        name="conv_sample",
    )(u_ext, w, b, g, bl)


def _memkv_kernel(x_ref, g_ref, w_ref, k_ref, v_ref):
    h = _rmsnorm(x_ref[...], g_ref[...]).astype(BF16)
    kv = _dot(h, w_ref[...])
    d = k_ref.shape[1]
    k_ref[...] = kv[:, :d]
    v_ref[...] = kv[:, d:]


def _memkv(x, g, w, *, tm=256):
    m, d = x.shape
    dm = w.shape[1] // 2
    return pl.pallas_call(
        _memkv_kernel,
        grid=(m // tm,),
        in_specs=[pl.BlockSpec((tm, d), lambda i: (i, 0)), _const_spec((1, d)), _const_spec(w.shape)],
        out_specs=[pl.BlockSpec((tm, dm), lambda i: (i, 0))] * 2,
        out_shape=[jax.ShapeDtypeStruct((m, dm), F32)] * 2,
        compiler_params=_cparams(1),
        name="memkv",
    )(x, g, w)


def _memattn_kernel(q_ref, k_ref, v_ref, o_ref):
    for h in range(MEM_HEADS):
        sl = slice(h * MEM_HEAD_DIM, (h + 1) * MEM_HEAD_DIM)
        s = _dot_nt(q_ref[0, :, sl], k_ref[0, :, sl].astype(BF16)) * (MEM_HEAD_DIM ** -0.5)
        e = jnp.exp(s - jnp.max(s, axis=-1, keepdims=True))
        p = e / jnp.sum(e, axis=-1, keepdims=True)
        o_ref[0, :, sl] = _dot(p.astype(BF16), v_ref[0, :, sl].astype(BF16)).astype(BF16)


def _memattn(q, k, v, *, tq):
    n, t, dm = q.shape
    nm = k.shape[1]
    return pl.pallas_call(
        _memattn_kernel,
        grid=(n, t // tq),
        in_specs=[pl.BlockSpec((1, tq, dm), lambda i, j: (i, j, 0)),
                  pl.BlockSpec((1, nm, dm), lambda i, j: (i, 0, 0)),
                  pl.BlockSpec((1, nm, dm), lambda i, j: (i, 0, 0))],
        out_specs=pl.BlockSpec((1, tq, dm), lambda i, j: (i, j, 0)),
        out_shape=jax.ShapeDtypeStruct((n, t, dm), BF16),
        compiler_params=_cparams(2),
        name="memattn",
    )(q, k, v)


def _memattn_sample_kernel(q_ref, k_ref, v_ref, o_ref):
    n_mem = k_ref.shape[1] // MEM_HEADS
    for g in range(q_ref.shape[0]):
        for h in range(MEM_HEADS):
            sl = slice(h * MEM_HEAD_DIM, (h + 1) * MEM_HEAD_DIM)
            head_rows = pl.ds(h, n_mem, stride=MEM_HEADS)
            s = _dot_nt(q_ref[g, :, sl], k_ref[g, head_rows, :].astype(BF16)) * (MEM_HEAD_DIM ** -0.5)
            e = jnp.exp(s - jnp.max(s, axis=-1, keepdims=True))
            p = e / jnp.sum(e, axis=-1, keepdims=True)
            o_ref[g, :, sl] = _dot(p.astype(BF16), v_ref[g, head_rows, :].astype(BF16)).astype(BF16)


def _memattn_sample(q, k, v, *, seqs=4):
    n, t, dm = q.shape
    rows = k.shape[1]
    return pl.pallas_call(
        _memattn_sample_kernel,
        grid=(n // seqs,),
        in_specs=[pl.BlockSpec((seqs, t, dm), lambda i: (i, 0, 0)),
                  pl.BlockSpec((seqs, rows, MEM_HEAD_DIM), lambda i: (i, 0, 0)),
                  pl.BlockSpec((seqs, rows, MEM_HEAD_DIM), lambda i: (i, 0, 0))],
        out_specs=pl.BlockSpec((seqs, t, dm), lambda i: (i, 0, 0)),
        out_shape=jax.ShapeDtypeStruct((n, t, dm), BF16),
        compiler_params=_cparams(1),
        name="memattn_sample",
    )(q, k, v)


def _ordinal_to_float(o):
    return pltpu.bitcast(jnp.where(o >= 0, o, (-o) | INT_MIN), F32)


def _float_to_ordinal(x):
    b = pltpu.bitcast(x, I32)
    return jnp.where(b < 0, -(b & 0x7FFFFFFF), b)


ORD_NEG_INF = -0x7F800000
FIRST_CHECK = 18
CHECK_EVERY = 2
VALUE_PROBES = 48
MAX_PROBES = VALUE_PROBES + 34


def _select_mask(score_ref, key_axis, n_tiles, n_sel, outside, idx_bits, valid_fn, emit_fn):
    _, ta, tb = score_ref.shape
    tile = (ta, tb)
    unit = LANES if key_axis == 1 else 4 * SUBLANES
    tk = tile[key_axis]
    fold = tk // unit
    shp = (ta, LANES) if key_axis == 1 else (unit, tb)
    k_f = jnp.float32(n_sel)
    out_f = jnp.float32(1.0) * outside

    def key_fold(x, op=jnp.add):
        if key_axis == 0:
            x = x.reshape(fold, unit, tb)
            acc = x[0]
            for i in range(1, fold):
                acc = op(acc, x[i])
            return acc
        acc = x[:, :LANES]
        for i in range(1, fold):
            acc = op(acc, x[:, i * LANES:(i + 1) * LANES])
        return acc

    def rep(x):
        return x if fold == 1 else jnp.concatenate([x] * fold, axis=key_axis)

    def row_total(part):
        return jnp.broadcast_to(jnp.sum(part, axis=key_axis, keepdims=True), shp)

    def key_index(j):
        return j * tk + lax.broadcasted_iota(I32, tile, key_axis)

    def count_ge(thr):
        thr_t = rep(thr)

        def body(j, part):
            return part + key_fold(jnp.where(score_ref[j] >= thr_t, 1.0, 0.0))

        part = lax.fori_loop(0, n_tiles, body, jnp.zeros(shp, F32))
        return row_total(part) + jnp.where(thr <= NEG, out_f, 0.0)

    def extremes(j, carry):
        mx, mn = carry
        sc = score_ref[j]
        return (jnp.maximum(mx, key_fold(sc, jnp.maximum)),
                jnp.minimum(mn, key_fold(jnp.where(sc > NEG, sc, jnp.inf), jnp.minimum)))

    mx, mn = lax.fori_loop(0, n_tiles, extremes, (jnp.full(shp, -jnp.inf, F32), jnp.full(shp, jnp.inf, F32)))
    mx = jnp.broadcast_to(jnp.max(mx, axis=key_axis, keepdims=True), shp)
    mx = jnp.maximum(mx, jnp.where(out_f > 0.0, NEG, -jnp.inf))
    mn_o = _float_to_ordinal(jnp.broadcast_to(jnp.min(mn, axis=key_axis, keepdims=True), shp))

    def probe(it, state):
        lo_o, hi_o, c_lo = state
        settled = (c_lo == k_f) | (hi_o == lo_o + 1)
        omid = (lo_o >> 1) + (hi_o >> 1) + (lo_o & hi_o & 1)
        vmid = _float_to_ordinal(0.5 * _ordinal_to_float(lo_o) + 0.5 * _ordinal_to_float(hi_o))
        one_sign = (lo_o >= 0) | (hi_o <= 0)
        cand = jnp.where(one_sign, vmid, omid)
        cand = jnp.where(it == 0, mn_o, jnp.where(it == 1, 0, jnp.where(it == 2, 1, cand)))
        cand = jnp.where(it < VALUE_PROBES, cand, omid)
        mid = jnp.where((cand > lo_o) & (cand < hi_o), cand, omid)
        cnt = count_ge(_ordinal_to_float(mid))
        up = (cnt >= k_f) & ~settled
        down = (cnt < k_f) & ~settled
        return jnp.where(up, mid, lo_o), jnp.where(down, mid, hi_o), jnp.where(up, cnt, c_lo)

    def unsettled(state):
        lo_o, hi_o, c_lo = state
        return jnp.max(jnp.where((c_lo == k_f) | (hi_o == lo_o + 1), 0.0, 1.0)).astype(I32)

    def probe_block(carry):
        it, state, _ = carry
        state = lax.fori_loop(it, it + CHECK_EVERY, probe, state)
        return it + CHECK_EVERY, state, unsettled(state)

    total = jnp.float32(1.0) * (n_tiles * tk) + out_f
    state = (jnp.full(shp, ORD_NEG_INF, I32), _float_to_ordinal(mx) + 1, jnp.broadcast_to(total, shp))
    state = lax.fori_loop(0, FIRST_CHECK, probe, state)
    _, (lo_o, _, cnt_ge), _ = lax.while_loop(lambda c: (c[2] > 0) & (c[0] < MAX_PROBES), probe_block,
                                             (jnp.int32(FIRST_CHECK), state, unsettled(state)))
    thr = _ordinal_to_float(lo_o)
    thr_t = rep(thr)

    surplus = jnp.max(jnp.where(cnt_ge > k_f, 1.0, 0.0)) > 0.0

    def tie_search():
        def count_gt_body(j, part):
            return part + key_fold(jnp.where(score_ref[j] > thr_t, 1.0, 0.0))

        cnt_gt = row_total(lax.fori_loop(0, n_tiles, count_gt_body, jnp.zeros(shp, F32)))
        cnt_gt = cnt_gt + jnp.where(thr < NEG, out_f, 0.0)
        in_tiles = n_tiles * tk

        def idx_step(it, v):
            bit = lax.shift_left(jnp.int32(1), idx_bits - 1 - it)
            cand = v | bit
            cand_t = rep(cand)

            def body(j, part):
                hit = jnp.where(score_ref[j] == thr_t, jnp.where(key_index(j) < cand_t, 1.0, 0.0), 0.0)
                return part + key_fold(hit)

            ties_below = row_total(lax.fori_loop(0, n_tiles, body, jnp.zeros(shp, F32)))
            out_below = jnp.clip(cand - in_tiles, 0, outside).astype(F32)
            ties_below = ties_below + jnp.where(thr == NEG, out_below, 0.0)
            return jnp.where(cnt_gt + ties_below < k_f, cand, v)

        return lax.fori_loop(0, idx_bits, idx_step, jnp.zeros(shp, I32))

    jmax = lax.cond(surplus, tie_search, lambda: jnp.full(shp, 2 ** idx_bits - 1, I32))
    jmax_t = rep(jmax)

    def emit(j, _):
        sc = score_ref[j]
        sel = jnp.where(sc > thr_t, 1.0, jnp.where(sc == thr_t, jnp.where(key_index(j) <= jmax_t, 1.0, 0.0), 0.0))
        emit_fn(j, jnp.where(valid_fn(j, tile), sel, 0.0))
        return 0

    lax.fori_loop(0, n_tiles, emit, 0)


def _sel_prompt_kernel(kib_ref, qit_ref, wit_ref, mask_ref, sc_ref, *, n_sel):
    tq = qit_ref.shape[2]
    s = kib_ref.shape[1]
    n_all = s // tq
    qb = pl.program_id(1)
    n_tiles = qb + 1
    qpos = qb * tq + lax.broadcasted_iota(I32, (tq, tq), 1)

    def score_tile(j):
        ki_t = kib_ref[0, pl.ds(pl.multiple_of(j * tq, tq), tq), :]
        acc = jnp.zeros((tq, tq), F32)
        for h in range(N_IDX_HEADS):
            d = _dot(ki_t, qit_ref[0, h * IDX_DIM:(h + 1) * IDX_DIM, :])
            acc = acc + wit_ref[0, h:h + 1, :] * jnp.maximum(d, 0.0)
        return acc

    def below(j, _):
        sc_ref[j] = score_tile(j)
        return 0

    lax.fori_loop(0, qb, below, 0)
    kpos = qb * tq + lax.broadcasted_iota(I32, (tq, tq), 0)
    sc_ref[qb] = jnp.where(kpos <= qpos, score_tile(qb), NEG)

    def valid(j, shape):
        return j * tq + lax.broadcasted_iota(I32, shape, 0) <= qpos

    def emit(j, sel):
        mask_ref[0, 0, j] = sel.astype(BF16)

    idx_bits = max(1, int(s - 1).bit_length())
    _select_mask(sc_ref, 0, n_tiles, n_sel, (n_all - n_tiles) * tq, idx_bits, valid, emit)

    def clear(j, _):
        mask_ref[0, 0, j] = jnp.zeros((tq, tq), BF16)
        return 0

    lax.fori_loop(n_tiles, n_all, clear, 0)


def _sel_prompt(kib, qitb, wit, *, tq, n_sel):
    bsz, s, _ = kib.shape
    nq = s // tq
    return pl.pallas_call(
        functools.partial(_sel_prompt_kernel, n_sel=n_sel),
        grid=(bsz, nq),
        in_specs=[pl.BlockSpec((1, s, kib.shape[2]), lambda b, i: (b, 0, 0)),
                  pl.BlockSpec((1, qitb.shape[1], tq), lambda b, i: (b, 0, i)),
                  pl.BlockSpec((1, wit.shape[1], tq), lambda b, i: (b, 0, i))],
        out_specs=pl.BlockSpec((1, 1, nq, tq, tq), lambda b, i: (b, i, 0, 0, 0)),
        out_shape=jax.ShapeDtypeStruct((bsz, nq, nq, tq, tq), BF16),
        scratch_shapes=[pltpu.VMEM((nq, tq, tq), F32)],
        compiler_params=_cparams(2),
        name="sel_prompt",
    )(kib, qitb, wit)


def _attn_prompt_kernel(qt_ref, k_ref, vt_ref, mask_ref, o_ref, q2_ref, s_ref, mx_ref, l_ref, out_ref):
    tq = qt_ref.shape[2]
    n_tiles = pl.program_id(1) + 1
    pair = 2 * HEAD_DIM
    part = 4 * SUBLANES
    fold = tq // part
    in_pair = lax.broadcasted_iota(I32, (pair, tq), 0)
    for hp in range(N_HEADS // 2):
        qp = qt_ref[0, hp * pair:(hp + 1) * pair, :]
        q2_ref[2 * hp] = jnp.where(in_pair < HEAD_DIM, qp, jnp.zeros_like(qp))
        q2_ref[2 * hp + 1] = jnp.where(in_pair >= HEAD_DIM, qp, jnp.zeros_like(qp))

    def key_rows(j):
        return pl.ds(pl.multiple_of(j * tq, tq), tq)

    group = s_ref.shape[1]
    n_groups = N_HEADS // group

    def score_tile(j, sel, slot, h0):
        for g in range(group):
            h = h0 + g
            k_pair = k_ref[0, key_rows(j), (h // 2) * pair:(h // 2 + 1) * pair]
            s = jnp.where(sel, _dot(k_pair, q2_ref[h]) * LOG2E, NEG)
            s_ref[slot, g, j] = s
            mx_ref[slot, g] = jnp.maximum(mx_ref[slot, g], jnp.max(s.reshape(fold, part, tq), axis=0))

    def weigh_tile(j, slot, h0, m):
        for g in range(group):
            rows = slice((h0 + g) * HEAD_DIM, (h0 + g + 1) * HEAD_DIM)
            p = jnp.exp2(s_ref[slot, g, j] - m[g])
            l_ref[g] = l_ref[g] + jnp.sum(p.reshape(fold, part, tq), axis=0)
            out_ref[rows, :] = out_ref[rows, :] + _dot(vt_ref[0, rows, key_rows(j)], p.astype(BF16))

    for gi in range(n_groups + 1):
        slot, prev = gi % 2, (gi - 1) % 2
        h0, h_prev = gi * group, (gi - 1) * group
        m_prev = None
        if gi > 0:
            m_prev = [jnp.max(mx_ref[prev, g], axis=0, keepdims=True) for g in range(group)]
            l_ref[...] = jnp.zeros(l_ref.shape, F32)
            out_ref[h_prev * HEAD_DIM:h0 * HEAD_DIM, :] = jnp.zeros((group * HEAD_DIM, tq), F32)
        if gi < n_groups:
            mx_ref[slot] = jnp.full(mx_ref.shape[1:], NEG, F32)

        def sweep(jj, _, gi=gi, slot=slot, prev=prev, h0=h0, h_prev=h_prev, m_prev=m_prev):
            for i in range(TILES_PER_STEP):
                j = jj * TILES_PER_STEP + i
                if gi < n_groups:
                    score_tile(j, mask_ref[0, 0, j].astype(F32) > 0.0, slot, h0)
                if gi > 0:
                    weigh_tile(j, prev, h_prev, m_prev)
            return 0

        lax.fori_loop(0, pl.cdiv(n_tiles, TILES_PER_STEP), sweep, 0)
        if gi > 0:
            for g in range(group):
                rows = slice((h_prev + g) * HEAD_DIM, (h_prev + g + 1) * HEAD_DIM)
                out_ref[rows, :] = out_ref[rows, :] / jnp.sum(l_ref[g], axis=0, keepdims=True)
    o_ref[0] = out_ref[...].T.astype(BF16)


def _attn_prompt(qtb, kb, vtb, mask, *, tq, group=2):
    bsz, da, s = qtb.shape
    nq = s // tq
    part = 4 * SUBLANES
    return pl.pallas_call(
        _attn_prompt_kernel,
        grid=(bsz, nq),
        in_specs=[pl.BlockSpec((1, da, tq), lambda b, i: (b, 0, i)),
                  pl.BlockSpec((1, s, da), lambda b, i: (b, 0, 0)),
                  pl.BlockSpec((1, da, s), lambda b, i: (b, 0, 0)),
                  pl.BlockSpec((1, 1, nq, tq, tq), lambda b, i: (b, i, 0, 0, 0))],
        out_specs=pl.BlockSpec((1, tq, da), lambda b, i: (b, i, 0)),
        out_shape=jax.ShapeDtypeStruct((bsz, s, da), BF16),
        scratch_shapes=[pltpu.VMEM((N_HEADS, 2 * HEAD_DIM, tq), BF16), pltpu.VMEM((2, group, nq, tq, tq), F32),
                        pltpu.VMEM((2, group, part, tq), F32), pltpu.VMEM((group, part, tq), F32),
                        pltpu.VMEM((da, tq), F32)],
        compiler_params=_cparams(2),
        name="attn_prompt",
    )(qtb, kb, vtb, mask)


def _score_sample_kernel(pt_ref, qi_ref, wi_ref, kin_ref, cache_ref, past_ref, new_ref, buf_ref, sem_ref, *, chunk):
    seq = pl.program_id(0)
    n_seq = pl.num_programs(0)
    n_pages = buf_ref.shape[1]
    t = new_ref.shape[1]
    slot = lax.rem(seq, 2)

    def page_copy(s, i, buf_slot):
        return pltpu.make_async_copy(cache_ref.at[pt_ref[s, i]], buf_ref.at[buf_slot, i], sem_ref.at[buf_slot])

    def start_pages(s, buf_slot):
        def body(i, _):
            page_copy(s, i, buf_slot).start()
            return 0
        lax.fori_loop(0, n_pages, body, 0)

    @pl.when(seq == 0)
    def _():
        start_pages(seq, slot)

    @pl.when(seq + 1 < n_seq)
    def _():
        start_pages(seq + 1, 1 - slot)

    def wait_body(i, _):
        page_copy(seq, i, slot).wait()
        return 0

    lax.fori_loop(0, n_pages, wait_body, 0)

    qi = qi_ref[0]
    wi = wi_ref[0]

    def scores(dots):
        w = wi * jnp.maximum(dots, 0.0)
        acc = w[0:t]
        for h in range(1, N_IDX_HEADS):
            acc = acc + w[h * t:(h + 1) * t]
        return acc

    for c in range(n_pages // chunk):
        keys_t = jnp.concatenate([buf_ref[slot, c * chunk + i].astype(BF16) for i in range(chunk)], axis=1)
        past_ref[0, :, c * chunk * PAGE_SIZE:(c + 1) * chunk * PAGE_SIZE] = scores(_dot(qi, keys_t))

    ext = jnp.concatenate([kin_ref[0], jnp.zeros((LANES - t, IDX_DIM), F32)], axis=0).astype(BF16)
    sc = scores(_dot_nt(qi, ext))
    qpos = lax.broadcasted_iota(I32, (t, LANES), 0)
    kpos = lax.broadcasted_iota(I32, (t, LANES), 1)
    new_ref[0] = jnp.where(kpos < t, jnp.where(kpos <= qpos, sc, NEG), -jnp.inf)


def _score_sample(page_table, qi_hm, wi_hm, ki_new, cache_idx_k, *, chunk=16):
    n, n_pages = page_table.shape
    t = ki_new.shape[1]
    ht = qi_hm.shape[1]
    grid_spec = pltpu.PrefetchScalarGridSpec(
        num_scalar_prefetch=1,
        grid=(n,),
        in_specs=[pl.BlockSpec((1, ht, IDX_DIM), lambda b, pt: (b, 0, 0)),
                  pl.BlockSpec((1, ht, 1), lambda b, pt: (b, 0, 0)),
                  pl.BlockSpec((1, t, IDX_DIM), lambda b, pt: (b, 0, 0)),
                  pl.BlockSpec(memory_space=pl.ANY)],
        out_specs=[pl.BlockSpec((1, t, n_pages * PAGE_SIZE), lambda b, pt: (b, 0, 0)),
                   pl.BlockSpec((1, t, LANES), lambda b, pt: (b, 0, 0))],
        scratch_shapes=[pltpu.VMEM((2, n_pages, IDX_DIM, PAGE_SIZE), F32), pltpu.SemaphoreType.DMA((2,))],
    )
    return pl.pallas_call(
        functools.partial(_score_sample_kernel, chunk=chunk),
        grid_spec=grid_spec,
        out_shape=[jax.ShapeDtypeStruct((n, t, n_pages * PAGE_SIZE), F32),
                   jax.ShapeDtypeStruct((n, t, LANES), F32)],
        compiler_params=_cparams(1),
        name="score_sample",
    )(page_table, qi_hm, wi_hm, ki_new, cache_idx_k)


def _sel_sample_kernel(past_ref, new_ref, mpast_ref, mnew_ref, key_ref, mask_ref, *, n_sel, t, tile):
    rows = past_ref.shape[0]
    n_past = past_ref.shape[1] // tile
    for j in range(n_past):
        key_ref[j] = past_ref[:, j * tile:(j + 1) * tile]
    pad = jnp.full((rows, tile - LANES), -jnp.inf, F32)
    key_ref[n_past] = jnp.concatenate([new_ref[...], pad], axis=1)
    qpos = lax.rem(lax.broadcasted_iota(I32, (rows, tile), 0), t)

    def valid(j, shape):
        kpos = lax.broadcasted_iota(I32, shape, 1)
        return kpos <= qpos + jnp.minimum(n_past - j, 1) * tile

    def emit(j, sel):
        mask_ref[j] = sel

    idx_bits = int((n_past + 1) * tile - 1).bit_length()
    _select_mask(key_ref, 1, n_past + 1, n_sel, 0, idx_bits, valid, emit)
    for j in range(n_past):
        mpast_ref[:, j * tile:(j + 1) * tile] = mask_ref[j]
    mnew_ref[...] = mask_ref[n_past][:, :LANES]


def _sel_sample(key_past, key_new, *, n_sel, t, rows=64, tile=2048):
    r, n_keys = key_past.shape
    n_t = n_keys // tile + 1
    return pl.pallas_call(
        functools.partial(_sel_sample_kernel, n_sel=n_sel, t=t, tile=tile),
        grid=(r // rows,),
        in_specs=[pl.BlockSpec((rows, n_keys), lambda i: (i, 0)), pl.BlockSpec((rows, LANES), lambda i: (i, 0))],
        out_specs=[pl.BlockSpec((rows, n_keys), lambda i: (i, 0)), pl.BlockSpec((rows, LANES), lambda i: (i, 0))],
        out_shape=[jax.ShapeDtypeStruct((r, n_keys), F32), jax.ShapeDtypeStruct((r, LANES), F32)],
        scratch_shapes=[pltpu.VMEM((n_t, rows, tile), F32), pltpu.VMEM((n_t, rows, tile), F32)],
        compiler_params=_cparams(1),
        name="sel_sample",
    )(key_past, key_new)


def _attn_sample_kernel(pt_ref, q_ref, kn_ref, vn_ref, mpast_ref, mnew_ref, *rest, pages):
    k_refs = rest[:pages]
    v_refs = rest[pages:2 * pages]
    o_ref, qbd_ref, m_ref, l_ref, acc_ref = rest[2 * pages:]
    t, da = q_ref.shape[1], q_ref.shape[2]
    ht = N_HEADS * t
    j = pl.program_id(1)

    @pl.when(j == 0)
    def _():
        q_rep = jnp.concatenate([q_ref[0].astype(F32)] * N_HEADS, axis=0)
        head_of_row = lax.broadcasted_iota(I32, (ht, da), 0) // t
        head_of_col = lax.broadcasted_iota(I32, (ht, da), 1) // HEAD_DIM
        qbd_ref[...] = jnp.where(head_of_row == head_of_col, q_rep, 0.0).astype(BF16)
        m_ref[...] = jnp.full(m_ref.shape, NEG, F32)
        l_ref[...] = jnp.zeros(l_ref.shape, F32)
        acc_ref[...] = jnp.zeros(acc_ref.shape, F32)

    def attend(k_pages, v_pages, mask_t, feature_major):
        qbd = qbd_ref[...]
        qk = [(_dot(qbd, kp) if feature_major else _dot_nt(qbd, kp)) for kp in k_pages]
        sel = jnp.concatenate([mask_t] * N_HEADS, axis=0) > 0.0
        s = jnp.where(sel, qk[0] if len(qk) == 1 else jnp.concatenate(qk, axis=1), NEG)
        m_old = m_ref[...]
        m_new = jnp.maximum(m_old, jnp.max(s, axis=-1, keepdims=True))
        alpha = jnp.exp(m_old - m_new)
        p = jnp.where(sel, jnp.exp(s - m_new), 0.0)
        l_ref[...] = alpha * l_ref[...] + jnp.sum(p, axis=-1, keepdims=True)
        pb = p.astype(BF16)
        pv = None
        for i, vp in enumerate(v_pages):
            p_i = pb[:, i * PAGE_SIZE:(i + 1) * PAGE_SIZE]
            term = _dot_nt(p_i, vp) if feature_major else _dot(p_i, vp)
            pv = term if pv is None else pv + term
        acc_ref[...] = alpha * acc_ref[...] + pv
        m_ref[...] = m_new

    attend([r[0].astype(BF16) for r in k_refs], [r[0].astype(BF16) for r in v_refs], mpast_ref[0], True)

    @pl.when(j == pl.num_programs(1) - 1)
    def _():
        zeros = jnp.zeros((PAGE_SIZE - t, da), F32)
        attend([jnp.concatenate([kn_ref[0], zeros], axis=0).astype(BF16)],
               [jnp.concatenate([vn_ref[0], zeros], axis=0).astype(BF16)], mnew_ref[0], False)
        out = acc_ref[...] / l_ref[...]
        o_ref[0] = jnp.concatenate(
            [out[h * t:(h + 1) * t, h * HEAD_DIM:(h + 1) * HEAD_DIM] for h in range(N_HEADS)], axis=1).astype(BF16)


def _attn_sample(page_table, q, k_new, v_new, mask_past, mask_new, cache_k, cache_v, *, pages):
    n, n_pages = page_table.shape
    t, da = q.shape[1], q.shape[2]
    steps = n_pages // pages
    ht = N_HEADS * t

    def page_spec(i):
        return pl.BlockSpec((1, da, PAGE_SIZE), lambda b, j, pt: (pt[b, j * pages + i], 0, 0))

    per_seq = lambda shape: pl.BlockSpec((1,) + shape, lambda b, j, pt: (b, 0, 0))
    grid_spec = pltpu.PrefetchScalarGridSpec(
        num_scalar_prefetch=1,
        grid=(n, steps),
        in_specs=[per_seq((t, da)), per_seq((t, da)), per_seq((t, da)),
                  pl.BlockSpec((1, t, pages * PAGE_SIZE), lambda b, j, pt: (b, 0, j)), per_seq((t, LANES))]
                 + [page_spec(i) for i in range(pages)] + [page_spec(i) for i in range(pages)],
        out_specs=per_seq((t, da)),
        scratch_shapes=[pltpu.VMEM((ht, da), BF16), pltpu.VMEM((ht, 1), F32), pltpu.VMEM((ht, 1), F32),
                        pltpu.VMEM((ht, da), F32)],
    )
    return pl.pallas_call(
        functools.partial(_attn_sample_kernel, pages=pages),
        grid_spec=grid_spec,
        out_shape=jax.ShapeDtypeStruct((n, t, da), BF16),
        compiler_params=_cparams(2),
        name="attn_sample",
    )(page_table, q, k_new, v_new, mask_past, mask_new, *([cache_k] * pages), *([cache_v] * pages))


def _merge_kernel(x_ref, h_ref, c_ref, a_ref, m_ref, wg_ref, wc_ref, wa_ref, wm_ref, wo_ref, o_ref):
    d = x_ref.shape[1]
    gates = jax.nn.sigmoid(_dot(h_ref[...], wg_ref[...]))
    mix = (gates[:, :d] * _dot(c_ref[...], wc_ref[...])
           + gates[:, d:2 * d] * _dot(a_ref[...], wa_ref[...])
           + gates[:, 2 * d:] * _dot(m_ref[...], wm_ref[...]))
    o_ref[...] = x_ref[...] + _dot(mix.astype(BF16), wo_ref[...])


def _merge(x, h, c, a, mo, wg, wc, wa, wm, wo, *, tm):
    m, d = x.shape
    row = lambda n: pl.BlockSpec((tm, n), lambda i: (i, 0))
    return pl.pallas_call(
        _merge_kernel,
        grid=(m // tm,),
        in_specs=[row(d), row(d), row(c.shape[1]), row(a.shape[1]), row(mo.shape[1]),
                  _const_spec(wg.shape), _const_spec(wc.shape), _const_spec(wa.shape), _const_spec(wm.shape),
                  _const_spec(wo.shape)],
        out_specs=row(d),
        out_shape=jax.ShapeDtypeStruct((m, d), F32),
        compiler_params=_cparams(1),
        name="merge",
    )(x, h, c, a, mo, wg, wc, wa, wm, wo)


def _ffn_kernel(x_ref, g_ref, wi_ref, wo_ref, gf_ref, o_ref):
    x = x_ref[...]
    dff = wo_ref.shape[0]
    hn = _rmsnorm(x, g_ref[...]).astype(BF16)
    a = _dot(hn, wi_ref[:, :dff])
    b = _dot(hn, wi_ref[:, dff:])
    y = x + _dot((a * jax.nn.sigmoid(a) * b).astype(BF16), wo_ref[...])
    o_ref[...] = _rmsnorm(y, gf_ref[...])


def _ffn(x, g, w_in, w_out, g_final, *, tm):
    m, d = x.shape
    return pl.pallas_call(
        _ffn_kernel,
        grid=(m // tm,),
        in_specs=[pl.BlockSpec((tm, d), lambda i: (i, 0)), _const_spec((1, d)), _const_spec(w_in.shape),
                  _const_spec(w_out.shape), _const_spec((1, d))],
        out_specs=pl.BlockSpec((tm, d), lambda i: (i, 0)),
        out_shape=jax.ShapeDtypeStruct((m, d), F32),
        compiler_params=_cparams(1),
        name="ffn",
    )(x, g, w_in, w_out, g_final)


def kernel(x_prompt, x_sample, mem_prompt, cache_conv, cache_k, cache_v, cache_idx_k, cache_mem_k, cache_mem_v,
           page_table, g_mix, w_in, w_conv_dw, b_conv_dw, g_conv_ln, b_conv_ln, w_conv_out, w_att_out, g_mem,
           w_mem_kv, w_mem_out, w_out, g_ffn, w_ffn_in, w_ffn_out, g_final):
    depth = g_mix.shape[0]
    assert depth == 1
    bsz, seq, d = x_prompt.shape
    n_dec, t_dec, _ = x_sample.shape
    n_mem = mem_prompt.shape[1]
    d_conv = w_conv_dw.shape[2]
    d_att = N_HEADS * HEAD_DIM
    d_idx = N_IDX_HEADS * IDX_DIM
    d_mem = MEM_HEADS * MEM_HEAD_DIM
    n_phys = cache_k.shape[1]
    past = page_table.shape[1] * PAGE_SIZE
    l = 0

    splits = [2 * d_conv, d_att, d_att, d_att, d_idx, IDX_DIM, N_IDX_HEADS, d_mem, 3 * d]
    offs = np.concatenate([[0], np.cumsum(splits)])
    w = w_in[l]
    col = lambda i: w[:, offs[i]:offs[i + 1]]
    pad_cols = lambda a, n: jnp.pad(a, ((0, 0), (0, n - a.shape[1])))
    glu, wq, wk, wv, wqi, wki, wwi, wqm, wgates = (col(i) for i in range(9))
    shared = [glu, wqm, pad_cols(wki, LANES)]
    w_prompt = jnp.concatenate(shared, axis=1).astype(BF16)
    w_prompt_t = jnp.concatenate([wk, wv, wki, pad_cols(wwi, WI_ROWS), wqi, wq], axis=1).T.astype(BF16)
    w_sample = jnp.concatenate(shared + [wq, wqi, pad_cols(wwi, LANES), wk, wv], axis=1).astype(BF16)
    w_gates = wgates.astype(BF16)
    bf = lambda a: a[l].astype(BF16)
    row2 = lambda a: a.reshape(1, -1)
    wdw, bdw, gln, bln = w_conv_dw[l], row2(b_conv_dw[l]), row2(g_conv_ln[l]), row2(b_conv_ln[l])
    dims = dict(d_conv=d_conv, d_att=d_att, d_idx=d_idx, d_mem=d_mem)
    merge = functools.partial(_merge, wg=w_gates, wc=bf(w_conv_out), wa=bf(w_att_out), wm=bf(w_mem_out),
                              wo=bf(w_out))
    ffn = functools.partial(_ffn, g=row2(g_ffn[l]), w_in=bf(w_ffn_in), w_out=bf(w_ffn_out),
                            g_final=row2(g_final))

    mp = bsz * seq
    xp = x_prompt.reshape(mp, d)
    u, qm, h, kib, kb, kt, vt, vtb, kit, qitb, wit, qtb = _inproj_prompt(x_prompt, row2(g_mix[l]), w_prompt,
                                                                         w_prompt_t, tm=512, **dims)
    per_seq = lambda a: a.reshape(bsz, seq, a.shape[-1])
    u3 = per_seq(u)
    c = _conv_prompt(u3, wdw, bdw, gln, bln)
    n_sel = min(TOPK_MAX, seq // 4)
    tq = 256
    mask = _sel_prompt(per_seq(kib), qitb, wit, tq=tq, n_sel=n_sel)
    oa = _attn_prompt(qtb, per_seq(kb), vtb, mask, tq=tq)
    mk, mv = _memkv(mem_prompt.reshape(bsz * n_mem, d), row2(g_mem[l]), bf(w_mem_kv))
    om = _memattn(per_seq(qm), mk.reshape(bsz, n_mem, d_mem), mv.reshape(bsz, n_mem, d_mem), tq=512)
    x1 = merge(xp, h, c.reshape(mp, d_conv), oa.reshape(mp, d_att), om.reshape(mp, d_mem), tm=512)
    y_prompt = ffn(x1, tm=256).reshape(bsz, seq, d)
    conv_state_prompt = u3[:, seq - (CONV_WIDTH - 1):][None]
    k_prompt = kt.reshape(bsz, N_HEADS, HEAD_DIM, seq).transpose(0, 3, 1, 2)[None]
    v_prompt = vt.reshape(bsz, N_HEADS, HEAD_DIM, seq).transpose(0, 3, 1, 2)[None]
    idx_k_prompt = kit.transpose(0, 2, 1)[None]
    mem_k_prompt = mk.reshape(1, bsz, n_mem, MEM_HEADS, MEM_HEAD_DIM)
    mem_v_prompt = mv.reshape(1, bsz, n_mem, MEM_HEADS, MEM_HEAD_DIM)

    ms = n_dec * t_dec
    xs = x_sample.reshape(ms, d)
    u, q, qm, h, qi, wi, k, v, ki = _inproj_sample(xs, row2(g_mix[l]), w_sample, **dims)
    per_seq = lambda a: a.reshape(n_dec, t_dec, a.shape[-1])
    u_ext = jnp.concatenate([cache_conv[l], per_seq(u)], axis=1)
    c = _conv_sample(u_ext, wdw, bdw, gln, bln)
    n_sel = min(TOPK_MAX, (past + t_dec) // 4)
    qi_hm = per_seq(qi).reshape(n_dec, t_dec, N_IDX_HEADS, IDX_DIM).transpose(0, 2, 1, 3)
    qi_hm = qi_hm.reshape(n_dec, N_IDX_HEADS * t_dec, IDX_DIM)
    wi_hm = per_seq(wi).transpose(0, 2, 1).reshape(n_dec, N_IDX_HEADS * t_dec, 1)
    page_major = lambda a: jnp.moveaxis(a[l], 1, -1).reshape(n_phys, -1, PAGE_SIZE)
    key_past, key_new = _score_sample(page_table, qi_hm, wi_hm, per_seq(ki), page_major(cache_idx_k))
    mask_past, mask_new = _sel_sample(key_past.reshape(ms, past), key_new.reshape(ms, LANES), n_sel=n_sel, t=t_dec)
    oa = _attn_sample(page_table, per_seq(q), per_seq(k), per_seq(v), mask_past.reshape(n_dec, t_dec, past),
                      mask_new.reshape(n_dec, t_dec, LANES), page_major(cache_k), page_major(cache_v), pages=32)
    om = _memattn_sample(per_seq(qm), cache_mem_k[l].reshape(n_dec, n_mem * MEM_HEADS, MEM_HEAD_DIM),
                         cache_mem_v[l].reshape(n_dec, n_mem * MEM_HEADS, MEM_HEAD_DIM))
    x1 = merge(xs, h, c.reshape(ms, d_conv), oa.reshape(ms, d_att), om.reshape(ms, d_mem), tm=ms)
    y_sample = ffn(x1, tm=ms).reshape(n_dec, t_dec, d)
    conv_state_sample = u_ext[:, t_dec:][None]
    k_sample = k.reshape(1, n_dec, t_dec, N_HEADS, HEAD_DIM)
    v_sample = v.reshape(1, n_dec, t_dec, N_HEADS, HEAD_DIM)
    idx_k_sample = ki.reshape(1, n_dec, t_dec, IDX_DIM)

    return (y_prompt, y_sample, conv_state_prompt, k_prompt, v_prompt, idx_k_prompt, mem_k_prompt, mem_v_prompt,
            conv_state_sample, k_sample, v_sample, idx_k_sample)
```

```python
import functools

import jax
import jax.numpy as jnp
import numpy as np
from jax import lax
from jax.experimental import pallas as pl
from jax.experimental.pallas import tpu as pltpu

EPS = 1e-6
NEG = -1e30
CONV_WIDTH = 31
N_HEADS = 8
HEAD_DIM = 64
N_IDX_HEADS = 8
IDX_DIM = 64
MEM_HEADS = 4
MEM_HEAD_DIM = 128
TOPK_MAX = 256
PAGE_SIZE = 128
LANES = 128
SUBLANES = 8
VMEM_LIMIT = 56 * 1024 * 1024
INT_MIN = -(2 ** 31)
LOG2E = 1.4426950408889634
TILES_PER_STEP = 2

BF16 = jnp.bfloat16
F32 = jnp.float32
I32 = jnp.int32


def _cparams(n_axes):
    return pltpu.CompilerParams(dimension_semantics=("arbitrary",) * n_axes, vmem_limit_bytes=VMEM_LIMIT)


def _const_spec(shape):
    zeros = (0,) * len(shape)
    return pl.BlockSpec(shape, lambda *_: zeros)


def _rmsnorm(x, g):
    return x * lax.rsqrt(jnp.mean(x * x, axis=-1, keepdims=True) + EPS) * g


def _dot(a, b):
    return jnp.dot(a, b, preferred_element_type=F32)


def _dot_nt(a, b):
    return lax.dot_general(a, b, (((1,), (1,)), ((), ())), preferred_element_type=F32)


WI_ROWS = 16


def _glu_qm(h, w_ref, u_ref, qm_ref, d_conv, d_mem):
    o = 0
    glu = _dot(h, w_ref[:, o:o + 2 * d_conv]); o += 2 * d_conv
    u_ref[...] = glu[:, :d_conv] * jax.nn.sigmoid(glu[:, d_conv:])
    qm_ref[...] = _dot(h, w_ref[:, o:o + d_mem]).astype(BF16); o += d_mem
    return o


def _inproj_prompt_kernel(x_ref, g_ref, w_ref, wt_ref, u_ref, qm_ref, h_ref, kib_ref, kb_ref,
                          kt_ref, vt_ref, vtb_ref, kit_ref, qit_ref, wit_ref, qt_ref, *, d_conv, d_att, d_idx, d_mem):
    h = _rmsnorm(x_ref[0], g_ref[...]).astype(BF16)
    h_ref[...] = h
    o = _glu_qm(h, w_ref, u_ref, qm_ref, d_conv, d_mem)
    kib_ref[...] = _dot(h, w_ref[:, o:o + LANES])[:, :IDX_DIM].astype(BF16)
    r = 0
    kt = _dot_nt(wt_ref[r:r + d_att, :], h); r += d_att
    kt_ref[0] = kt
    kb_ref[...] = kt.T.astype(BF16)
    vt = _dot_nt(wt_ref[r:r + d_att, :], h); r += d_att
    vt_ref[0] = vt
    vtb_ref[0] = vt.astype(BF16)
    kit_ref[0] = _dot_nt(wt_ref[r:r + IDX_DIM, :], h); r += IDX_DIM
    wit_ref[0] = _dot_nt(wt_ref[r:r + WI_ROWS, :], h)[:N_IDX_HEADS] * (N_IDX_HEADS ** -0.5); r += WI_ROWS
    qit_ref[0] = (_dot_nt(wt_ref[r:r + d_idx, :], h) * (IDX_DIM ** -0.5)).astype(BF16); r += d_idx
    qt_ref[0] = (_dot_nt(wt_ref[r:r + d_att, :], h) * (HEAD_DIM ** -0.5)).astype(BF16)


def _inproj_prompt(x, g, w, wt, *, tm, d_conv, d_att, d_idx, d_mem):
    bsz, s, d = x.shape
    nt = s // tm
    row = lambda n: pl.BlockSpec((tm, n), lambda b, i: (b * nt + i, 0))
    col = lambda n: pl.BlockSpec((1, n, tm), lambda b, i: (b, 0, i))
    outs = [(d_conv, F32), (d_mem, BF16), (d, BF16), (IDX_DIM, BF16), (d_att, BF16)]
    outs_t = [(d_att, F32), (d_att, F32), (d_att, BF16), (IDX_DIM, F32), (d_idx, BF16), (N_IDX_HEADS, F32),
              (d_att, BF16)]
    return pl.pallas_call(
        functools.partial(_inproj_prompt_kernel, d_conv=d_conv, d_att=d_att, d_idx=d_idx, d_mem=d_mem),
        grid=(bsz, nt),
        in_specs=[pl.BlockSpec((1, tm, d), lambda b, i: (b, i, 0)), _const_spec((1, d)), _const_spec(w.shape),
                  _const_spec(wt.shape)],
        out_specs=[row(n) for n, _ in outs] + [col(n) for n, _ in outs_t],
        out_shape=[jax.ShapeDtypeStruct((bsz * s, n), dt) for n, dt in outs]
                  + [jax.ShapeDtypeStruct((bsz, n, s), dt) for n, dt in outs_t],
        compiler_params=_cparams(2),
        name="inproj_prompt",
    )(x, g, w, wt)


def _inproj_sample_kernel(x_ref, g_ref, w_ref, u_ref, q_ref, qm_ref, h_ref, qi_ref, wi_ref, k_ref, v_ref, ki_ref,
                          *, d_conv, d_att, d_idx, d_mem):
    h = _rmsnorm(x_ref[...], g_ref[...]).astype(BF16)
    h_ref[...] = h
    o = _glu_qm(h, w_ref, u_ref, qm_ref, d_conv, d_mem)
    ki_ref[...] = _dot(h, w_ref[:, o:o + LANES])[:, :IDX_DIM]; o += LANES
    q_ref[...] = (_dot(h, w_ref[:, o:o + d_att]) * (HEAD_DIM ** -0.5)).astype(BF16); o += d_att
    qi_ref[...] = (_dot(h, w_ref[:, o:o + d_idx]) * (IDX_DIM ** -0.5)).astype(BF16); o += d_idx
    wi_ref[...] = _dot(h, w_ref[:, o:o + LANES])[:, :N_IDX_HEADS] * (N_IDX_HEADS ** -0.5); o += LANES
    k_ref[...] = _dot(h, w_ref[:, o:o + d_att]); o += d_att
    v_ref[...] = _dot(h, w_ref[:, o:o + d_att])


def _inproj_sample(x, g, w, *, d_conv, d_att, d_idx, d_mem):
    m, d = x.shape
    outs = [(d_conv, F32), (d_att, BF16), (d_mem, BF16), (d, BF16), (d_idx, BF16), (N_IDX_HEADS, F32),
            (d_att, F32), (d_att, F32), (IDX_DIM, F32)]
    return pl.pallas_call(
        functools.partial(_inproj_sample_kernel, d_conv=d_conv, d_att=d_att, d_idx=d_idx, d_mem=d_mem),
        grid=(1,),
        in_specs=[_const_spec((m, d)), _const_spec((1, d)), _const_spec(w.shape)],
        out_specs=[_const_spec((m, n)) for n, _ in outs],
        out_shape=[jax.ShapeDtypeStruct((m, n), dt) for n, dt in outs],
        compiler_params=_cparams(1),
        name="inproj_sample",
    )(x, g, w)


def _conv_taps(ext_ref, start, rows, w_ref, by_phase):
    if not by_phase:
        acc = w_ref[0:1, :] * ext_ref[pl.ds(start, rows), :]
        for j in range(1, CONV_WIDTH):
            acc = acc + w_ref[j:j + 1, :] * ext_ref[pl.ds(start + j, rows), :]
        return acc
    acc = None
    for r in range(SUBLANES):
        y = None
        for j in range(CONV_WIDTH):
            if (start + j) % SUBLANES == r:
                term = w_ref[j:j + 1, :] * ext_ref[pl.ds(start + j - r, rows + (SUBLANES if r else 0)), :]
                y = term if y is None else y + term
        if y is not None:
            y = y[r:r + rows] if r else y
            acc = y if acc is None else acc + y
    return acc


def _conv_rows(ext_ref, start, rows, w_ref, b_ref, g_ref, bl_ref, by_phase=False):
    c = _conv_taps(ext_ref, start, rows, w_ref, by_phase) + b_ref[...]
    mu = jnp.mean(c, axis=-1, keepdims=True)
    xc = c - mu
    y = xc * lax.rsqrt(jnp.mean(xc * xc, axis=-1, keepdims=True) + EPS) * g_ref[...] + bl_ref[...]
    return (y * jax.nn.sigmoid(y)).astype(BF16)


def _conv_prompt_kernel(prev_ref, cur_ref, w_ref, b_ref, g_ref, bl_ref, o_ref, ext_ref, *, halo, chunk):
    t = cur_ref.shape[1]

    @pl.when(pl.program_id(1) == 0)
    def _():
        ext_ref[0:halo, :] = jnp.zeros((halo, ext_ref.shape[1]), F32)

    @pl.when(pl.program_id(1) > 0)
    def _():
        ext_ref[0:halo, :] = prev_ref[0]

    ext_ref[halo:halo + t, :] = cur_ref[0]
    ext_ref[halo + t:, :] = jnp.zeros((SUBLANES, ext_ref.shape[1]), F32)
    first = halo - (CONV_WIDTH - 1)
    for c in range(t // chunk):
        o_ref[0, c * chunk:(c + 1) * chunk, :] = _conv_rows(ext_ref, first + c * chunk, chunk, w_ref, b_ref, g_ref,
                                                            bl_ref, by_phase=True)


def _conv_prompt(u, w, b, g, bl, *, t=512, halo=32, chunk=128):
    bsz, s, dc = u.shape
    assert s % t == 0 and t % halo == 0 and halo >= CONV_WIDTH - 1 and t % chunk == 0
    r = t // halo
    return pl.pallas_call(
        functools.partial(_conv_prompt_kernel, halo=halo, chunk=chunk),
        grid=(bsz, s // t),
        in_specs=[pl.BlockSpec((1, halo, dc), lambda bi, i: (bi, jnp.maximum(i * r - 1, 0), 0)),
                  pl.BlockSpec((1, t, dc), lambda bi, i: (bi, i, 0)),
                  _const_spec(w.shape), _const_spec((1, dc)), _const_spec((1, dc)), _const_spec((1, dc))],
        out_specs=pl.BlockSpec((1, t, dc), lambda bi, i: (bi, i, 0)),
        out_shape=jax.ShapeDtypeStruct((bsz, s, dc), BF16),
        scratch_shapes=[pltpu.VMEM((halo + t + SUBLANES, dc), F32)],
        compiler_params=_cparams(2),
        name="conv_prompt",
    )(u, u, w, b, g, bl)


def _conv_sample_kernel(ext_ref, w_ref, b_ref, g_ref, bl_ref, o_ref):
    t = o_ref.shape[1]
    for s in range(o_ref.shape[0]):
        o_ref[s] = _conv_rows(ext_ref.at[s], 0, t, w_ref, b_ref, g_ref, bl_ref)


def _conv_sample(u_ext, w, b, g, bl, *, seqs=8):
    n, te, dc = u_ext.shape
    t = te - (CONV_WIDTH - 1)
    return pl.pallas_call(
        _conv_sample_kernel,
        grid=(n // seqs,),
        in_specs=[pl.BlockSpec((seqs, te, dc), lambda i: (i, 0, 0)),
                  _const_spec(w.shape), _const_spec((1, dc)), _const_spec((1, dc)), _const_spec((1, dc))],
        out_specs=pl.BlockSpec((seqs, t, dc), lambda i: (i, 0, 0)),
        out_shape=jax.ShapeDtypeStruct((n, t, dc), BF16),
        compiler_params=_cparams(1),
        name="conv_sample",
    )(u_ext, w, b, g, bl)


def _memkv_kernel(x_ref, g_ref, w_ref, k_ref, v_ref):
    h = _rmsnorm(x_ref[...], g_ref[...]).astype(BF16)
    kv = _dot(h, w_ref[...])
    d = k_ref.shape[1]
    k_ref[...] = kv[:, :d]
    v_ref[...] = kv[:, d:]


def _memkv(x, g, w, *, tm=256):
    m, d = x.shape
    dm = w.shape[1] // 2
    return pl.pallas_call(
        _memkv_kernel,
        grid=(m // tm,),
        in_specs=[pl.BlockSpec((tm, d), lambda i: (i, 0)), _const_spec((1, d)), _const_spec(w.shape)],
        out_specs=[pl.BlockSpec((tm, dm), lambda i: (i, 0))] * 2,
        out_shape=[jax.ShapeDtypeStruct((m, dm), F32)] * 2,
        compiler_params=_cparams(1),
        name="memkv",
    )(x, g, w)


def _memattn_kernel(q_ref, k_ref, v_ref, o_ref):
    for h in range(MEM_HEADS):
        sl = slice(h * MEM_HEAD_DIM, (h + 1) * MEM_HEAD_DIM)
        s = _dot_nt(q_ref[0, :, sl], k_ref[0, :, sl].astype(BF16)) * (MEM_HEAD_DIM ** -0.5)
        e = jnp.exp(s - jnp.max(s, axis=-1, keepdims=True))
        p = e / jnp.sum(e, axis=-1, keepdims=True)
        o_ref[0, :, sl] = _dot(p.astype(BF16), v_ref[0, :, sl].astype(BF16)).astype(BF16)


def _memattn(q, k, v, *, tq):
    n, t, dm = q.shape
    nm = k.shape[1]
    return pl.pallas_call(
        _memattn_kernel,
        grid=(n, t // tq),
        in_specs=[pl.BlockSpec((1, tq, dm), lambda i, j: (i, j, 0)),
                  pl.BlockSpec((1, nm, dm), lambda i, j: (i, 0, 0)),
                  pl.BlockSpec((1, nm, dm), lambda i, j: (i, 0, 0))],
        out_specs=pl.BlockSpec((1, tq, dm), lambda i, j: (i, j, 0)),
        out_shape=jax.ShapeDtypeStruct((n, t, dm), BF16),
        compiler_params=_cparams(2),
        name="memattn",
    )(q, k, v)


def _memattn_sample_kernel(q_ref, k_ref, v_ref, o_ref):
    n_mem = k_ref.shape[1] // MEM_HEADS
    for g in range(q_ref.shape[0]):
        for h in range(MEM_HEADS):
            sl = slice(h * MEM_HEAD_DIM, (h + 1) * MEM_HEAD_DIM)
            head_rows = pl.ds(h, n_mem, stride=MEM_HEADS)
            s = _dot_nt(q_ref[g, :, sl], k_ref[g, head_rows, :].astype(BF16)) * (MEM_HEAD_DIM ** -0.5)
            e = jnp.exp(s - jnp.max(s, axis=-1, keepdims=True))
            p = e / jnp.sum(e, axis=-1, keepdims=True)
            o_ref[g, :, sl] = _dot(p.astype(BF16), v_ref[g, head_rows, :].astype(BF16)).astype(BF16)


def _memattn_sample(q, k, v, *, seqs=4):
    n, t, dm = q.shape
    rows = k.shape[1]
    return pl.pallas_call(
        _memattn_sample_kernel,
        grid=(n // seqs,),
        in_specs=[pl.BlockSpec((seqs, t, dm), lambda i: (i, 0, 0)),
                  pl.BlockSpec((seqs, rows, MEM_HEAD_DIM), lambda i: (i, 0, 0)),
                  pl.BlockSpec((seqs, rows, MEM_HEAD_DIM), lambda i: (i, 0, 0))],
        out_specs=pl.BlockSpec((seqs, t, dm), lambda i: (i, 0, 0)),
        out_shape=jax.ShapeDtypeStruct((n, t, dm), BF16),
        compiler_params=_cparams(1),
        name="memattn_sample",
    )(q, k, v)


def _ordinal_to_float(o):
    return pltpu.bitcast(jnp.where(o >= 0, o, (-o) | INT_MIN), F32)


def _float_to_ordinal(x):
    b = pltpu.bitcast(x, I32)
    return jnp.where(b < 0, -(b & 0x7FFFFFFF), b)


ORD_NEG_INF = -0x7F800000
ORD_MIN_NORMAL = 0x00800000
FIRST_CHECK = 18
CHECK_EVERY = 2
VALUE_PROBES = 48
MAX_PROBES = VALUE_PROBES + 34


def _select_mask(score_ref, key_axis, n_tiles, n_sel, outside, idx_bits, valid_fn, emit_fn):
    _, ta, tb = score_ref.shape
    tile = (ta, tb)
    unit = LANES if key_axis == 1 else 4 * SUBLANES
    tk = tile[key_axis]
    fold = tk // unit
    shp = (ta, LANES) if key_axis == 1 else (unit, tb)
    k_f = jnp.float32(n_sel)
    out_f = jnp.float32(1.0) * outside

    def key_fold(x, op=jnp.add):
        if key_axis == 0:
            x = x.reshape(fold, unit, tb)
            acc = x[0]
            for i in range(1, fold):
                acc = op(acc, x[i])
            return acc
        acc = x[:, :LANES]
        for i in range(1, fold):
            acc = op(acc, x[:, i * LANES:(i + 1) * LANES])
        return acc

    def rep(x):
        return x if fold == 1 else jnp.concatenate([x] * fold, axis=key_axis)

    def row_total(part):
        return jnp.broadcast_to(jnp.sum(part, axis=key_axis, keepdims=True), shp)

    def key_index(j):
        return j * tk + lax.broadcasted_iota(I32, tile, key_axis)

    def count_ge(thr):
        thr_t = rep(thr)

        def body(j, part):
            return part + key_fold(jnp.where(score_ref[j] >= thr_t, 1.0, 0.0))

        part = lax.fori_loop(0, n_tiles, body, jnp.zeros(shp, F32))
        return row_total(part) + jnp.where(thr <= NEG, out_f, 0.0)

    def extremes(j, carry):
        mx, mn = carry
        sc = score_ref[j]
        return (jnp.maximum(mx, key_fold(sc, jnp.maximum)),
                jnp.minimum(mn, key_fold(jnp.where(sc > NEG, sc, jnp.inf), jnp.minimum)))

    mx, mn = lax.fori_loop(0, n_tiles, extremes, (jnp.full(shp, -jnp.inf, F32), jnp.full(shp, jnp.inf, F32)))
    mx = jnp.broadcast_to(jnp.max(mx, axis=key_axis, keepdims=True), shp)
    mx = jnp.maximum(mx, jnp.where(out_f > 0.0, NEG, -jnp.inf))
    mn_o = _float_to_ordinal(jnp.broadcast_to(jnp.min(mn, axis=key_axis, keepdims=True), shp))

    def is_settled(lo_o, hi_o, c_lo):
        return ((c_lo == k_f) | (hi_o == lo_o + 1) | ((lo_o >= 0) & (hi_o <= ORD_MIN_NORMAL))
                | ((hi_o <= 0) & (lo_o >= -ORD_MIN_NORMAL)))

    def probe(it, state):
        lo_o, hi_o, c_lo = state
        settled = is_settled(lo_o, hi_o, c_lo)
        omid = (lo_o >> 1) + (hi_o >> 1) + (lo_o & hi_o & 1)
        vmid = _float_to_ordinal(0.5 * _ordinal_to_float(lo_o) + 0.5 * _ordinal_to_float(hi_o))
        one_sign = (lo_o >= 0) | (hi_o <= 0)
        cand = jnp.where(one_sign, vmid, omid)
        cand = jnp.where(it == 0, mn_o, jnp.where(it == 1, 0, jnp.where(it == 2, ORD_MIN_NORMAL, cand)))
        cand = jnp.where(it < VALUE_PROBES, cand, omid)
        mid = jnp.where((cand > lo_o) & (cand < hi_o), cand, omid)
        cnt = count_ge(_ordinal_to_float(mid))
        up = (cnt >= k_f) & ~settled
        down = (cnt < k_f) & ~settled
        return jnp.where(up, mid, lo_o), jnp.where(down, mid, hi_o), jnp.where(up, cnt, c_lo)

    def unsettled(state):
        return jnp.max(jnp.where(is_settled(*state), 0.0, 1.0)).astype(I32)

    def probe_block(carry):
        it, state, _ = carry
        state = lax.fori_loop(it, it + CHECK_EVERY, probe, state)
        return it + CHECK_EVERY, state, unsettled(state)

    total = jnp.float32(1.0) * (n_tiles * tk) + out_f
    state = (jnp.full(shp, ORD_NEG_INF, I32), _float_to_ordinal(mx) + 1, jnp.broadcast_to(total, shp))
    state = lax.fori_loop(0, FIRST_CHECK, probe, state)
    _, (lo_o, _, cnt_ge), _ = lax.while_loop(lambda c: (c[2] > 0) & (c[0] < MAX_PROBES), probe_block,
                                             (jnp.int32(FIRST_CHECK), state, unsettled(state)))
    thr = _ordinal_to_float(lo_o)
    thr_t = rep(thr)

    surplus = jnp.max(jnp.where(cnt_ge > k_f, 1.0, 0.0)) > 0.0

    def tie_search():
        def count_gt_body(j, part):
            return part + key_fold(jnp.where(score_ref[j] > thr_t, 1.0, 0.0))

        cnt_gt = row_total(lax.fori_loop(0, n_tiles, count_gt_body, jnp.zeros(shp, F32)))
        cnt_gt = cnt_gt + jnp.where(thr < NEG, out_f, 0.0)
        in_tiles = n_tiles * tk

        def idx_step(it, v):
            bit = lax.shift_left(jnp.int32(1), idx_bits - 1 - it)
            cand = v | bit
            cand_t = rep(cand)

            def body(j, part):
                hit = jnp.where(score_ref[j] == thr_t, jnp.where(key_index(j) < cand_t, 1.0, 0.0), 0.0)
                return part + key_fold(hit)

            ties_below = row_total(lax.fori_loop(0, n_tiles, body, jnp.zeros(shp, F32)))
            out_below = jnp.clip(cand - in_tiles, 0, outside).astype(F32)
            ties_below = ties_below + jnp.where(thr == NEG, out_below, 0.0)
            return jnp.where(cnt_gt + ties_below < k_f, cand, v)

        return lax.fori_loop(0, idx_bits, idx_step, jnp.zeros(shp, I32))

    jmax = lax.cond(surplus, tie_search, lambda: jnp.full(shp, 2 ** idx_bits - 1, I32))
    jmax_t = rep(jmax)

    def emit(j, _):
        sc = score_ref[j]
        sel = jnp.where(sc > thr_t, 1.0, jnp.where(sc == thr_t, jnp.where(key_index(j) <= jmax_t, 1.0, 0.0), 0.0))
        emit_fn(j, jnp.where(valid_fn(j, tile), sel, 0.0))
        return 0

    lax.fori_loop(0, n_tiles, emit, 0)


def _sel_prompt_kernel(kib_ref, qit_ref, wit_ref, mask_ref, sc_ref, *, n_sel):
    tq = qit_ref.shape[2]
    s = kib_ref.shape[1]
    n_all = s // tq
    qb = pl.program_id(1)
    n_tiles = qb + 1
    qpos = qb * tq + lax.broadcasted_iota(I32, (tq, tq), 1)

    def score_tile(j):
        ki_t = kib_ref[0, pl.ds(pl.multiple_of(j * tq, tq), tq), :]
        acc = jnp.zeros((tq, tq), F32)
        for h in range(N_IDX_HEADS):
            d = _dot(ki_t, qit_ref[0, h * IDX_DIM:(h + 1) * IDX_DIM, :])
            acc = acc + wit_ref[0, h:h + 1, :] * jnp.maximum(d, 0.0)
        return acc

    def below(j, _):
        sc_ref[j] = score_tile(j)
        return 0

    lax.fori_loop(0, qb, below, 0)
    kpos = qb * tq + lax.broadcasted_iota(I32, (tq, tq), 0)
    sc_ref[qb] = jnp.where(kpos <= qpos, score_tile(qb), NEG)

    def valid(j, shape):
        return j * tq + lax.broadcasted_iota(I32, shape, 0) <= qpos

    def emit(j, sel):
        mask_ref[0, 0, j] = sel.astype(BF16)

    idx_bits = max(1, int(s - 1).bit_length())
    _select_mask(sc_ref, 0, n_tiles, n_sel, (n_all - n_tiles) * tq, idx_bits, valid, emit)

    def clear(j, _):
        mask_ref[0, 0, j] = jnp.zeros((tq, tq), BF16)
        return 0

    lax.fori_loop(n_tiles, n_all, clear, 0)


def _sel_prompt(kib, qitb, wit, *, tq, n_sel):
    bsz, s, _ = kib.shape
    nq = s // tq
    return pl.pallas_call(
        functools.partial(_sel_prompt_kernel, n_sel=n_sel),
        grid=(bsz, nq),
        in_specs=[pl.BlockSpec((1, s, kib.shape[2]), lambda b, i: (b, 0, 0)),
                  pl.BlockSpec((1, qitb.shape[1], tq), lambda b, i: (b, 0, i)),
                  pl.BlockSpec((1, wit.shape[1], tq), lambda b, i: (b, 0, i))],
        out_specs=pl.BlockSpec((1, 1, nq, tq, tq), lambda b, i: (b, i, 0, 0, 0)),
        out_shape=jax.ShapeDtypeStruct((bsz, nq, nq, tq, tq), BF16),
        scratch_shapes=[pltpu.VMEM((nq, tq, tq), F32)],
        compiler_params=_cparams(2),
        name="sel_prompt",
    )(kib, qitb, wit)


def _attn_prompt_kernel(qt_ref, k_ref, vt_ref, mask_ref, o_ref, q2_ref, s_ref, mx_ref, l_ref, out_ref):
    tq = qt_ref.shape[2]
    n_tiles = pl.program_id(1) + 1
    pair = 2 * HEAD_DIM
    part = 4 * SUBLANES
    fold = tq // part
    in_pair = lax.broadcasted_iota(I32, (pair, tq), 0)
    for hp in range(N_HEADS // 2):
        qp = qt_ref[0, hp * pair:(hp + 1) * pair, :]
        q2_ref[2 * hp] = jnp.where(in_pair < HEAD_DIM, qp, jnp.zeros_like(qp))
        q2_ref[2 * hp + 1] = jnp.where(in_pair >= HEAD_DIM, qp, jnp.zeros_like(qp))

    def key_rows(j):
        return pl.ds(pl.multiple_of(j * tq, tq), tq)

    group = s_ref.shape[1]
    n_groups = N_HEADS // group

    def score_tile(j, sel, slot, h0):
        for g in range(group):
            h = h0 + g
            k_pair = k_ref[0, key_rows(j), (h // 2) * pair:(h // 2 + 1) * pair]
            s = jnp.where(sel, _dot(k_pair, q2_ref[h]) * LOG2E, NEG)
            s_ref[slot, g, j] = s
            mx_ref[slot, g] = jnp.maximum(mx_ref[slot, g], jnp.max(s.reshape(fold, part, tq), axis=0))

    def weigh_tile(j, slot, h0, m):
        for g in range(group):
            rows = slice((h0 + g) * HEAD_DIM, (h0 + g + 1) * HEAD_DIM)
            p = jnp.exp2(s_ref[slot, g, j] - m[g])
            l_ref[g] = l_ref[g] + jnp.sum(p.reshape(fold, part, tq), axis=0)
            out_ref[rows, :] = out_ref[rows, :] + _dot(vt_ref[0, rows, key_rows(j)], p.astype(BF16))

    for gi in range(n_groups + 1):
        slot, prev = gi % 2, (gi - 1) % 2
        h0, h_prev = gi * group, (gi - 1) * group
        m_prev = None
        if gi > 0:
            m_prev = [jnp.max(mx_ref[prev, g], axis=0, keepdims=True) for g in range(group)]
            l_ref[...] = jnp.zeros(l_ref.shape, F32)
            out_ref[h_prev * HEAD_DIM:h0 * HEAD_DIM, :] = jnp.zeros((group * HEAD_DIM, tq), F32)
        if gi < n_groups:
            mx_ref[slot] = jnp.full(mx_ref.shape[1:], NEG, F32)

        def sweep(jj, _, gi=gi, slot=slot, prev=prev, h0=h0, h_prev=h_prev, m_prev=m_prev):
            for i in range(TILES_PER_STEP):
                j = jj * TILES_PER_STEP + i
                if gi < n_groups:
                    score_tile(j, mask_ref[0, 0, j].astype(F32) > 0.0, slot, h0)
                if gi > 0:
                    weigh_tile(j, prev, h_prev, m_prev)
            return 0

        lax.fori_loop(0, pl.cdiv(n_tiles, TILES_PER_STEP), sweep, 0)
        if gi > 0:
            for g in range(group):
                rows = slice((h_prev + g) * HEAD_DIM, (h_prev + g + 1) * HEAD_DIM)
                out_ref[rows, :] = out_ref[rows, :] / jnp.sum(l_ref[g], axis=0, keepdims=True)
    o_ref[0] = out_ref[...].T.astype(BF16)


def _attn_prompt(qtb, kb, vtb, mask, *, tq, group=2):
    bsz, da, s = qtb.shape
    nq = s // tq
    part = 4 * SUBLANES
    return pl.pallas_call(
        _attn_prompt_kernel,
        grid=(bsz, nq),
        in_specs=[pl.BlockSpec((1, da, tq), lambda b, i: (b, 0, i)),
                  pl.BlockSpec((1, s, da), lambda b, i: (b, 0, 0)),
                  pl.BlockSpec((1, da, s), lambda b, i: (b, 0, 0)),
                  pl.BlockSpec((1, 1, nq, tq, tq), lambda b, i: (b, i, 0, 0, 0))],
        out_specs=pl.BlockSpec((1, tq, da), lambda b, i: (b, i, 0)),
        out_shape=jax.ShapeDtypeStruct((bsz, s, da), BF16),
        scratch_shapes=[pltpu.VMEM((N_HEADS, 2 * HEAD_DIM, tq), BF16), pltpu.VMEM((2, group, nq, tq, tq), F32),
                        pltpu.VMEM((2, group, part, tq), F32), pltpu.VMEM((group, part, tq), F32),
                        pltpu.VMEM((da, tq), F32)],
        compiler_params=_cparams(2),
        name="attn_prompt",
    )(qtb, kb, vtb, mask)


def _score_sample_kernel(pt_ref, qi_ref, wi_ref, kin_ref, cache_ref, past_ref, new_ref, buf_ref, sem_ref, *, chunk):
    seq = pl.program_id(0)
    n_seq = pl.num_programs(0)
    n_pages = buf_ref.shape[1]
    t = new_ref.shape[1]
    slot = lax.rem(seq, 2)

    def page_copy(s, i, buf_slot):
        return pltpu.make_async_copy(cache_ref.at[pt_ref[s, i]], buf_ref.at[buf_slot, i], sem_ref.at[buf_slot])

    def start_pages(s, buf_slot):
        def body(i, _):
            page_copy(s, i, buf_slot).start()
            return 0
        lax.fori_loop(0, n_pages, body, 0)

    @pl.when(seq == 0)
    def _():
        start_pages(seq, slot)

    @pl.when(seq + 1 < n_seq)
    def _():
        start_pages(seq + 1, 1 - slot)

    def wait_body(i, _):
        page_copy(seq, i, slot).wait()
        return 0

    lax.fori_loop(0, n_pages, wait_body, 0)

    qi = qi_ref[0]
    wi = wi_ref[0]

    def scores(dots):
        w = wi * jnp.maximum(dots, 0.0)
        acc = w[0:t]
        for h in range(1, N_IDX_HEADS):
            acc = acc + w[h * t:(h + 1) * t]
        return acc

    for c in range(n_pages // chunk):
        keys_t = jnp.concatenate([buf_ref[slot, c * chunk + i].astype(BF16) for i in range(chunk)], axis=1)
        past_ref[0, :, c * chunk * PAGE_SIZE:(c + 1) * chunk * PAGE_SIZE] = scores(_dot(qi, keys_t))

    ext = jnp.concatenate([kin_ref[0], jnp.zeros((LANES - t, IDX_DIM), F32)], axis=0).astype(BF16)
    sc = scores(_dot_nt(qi, ext))
    qpos = lax.broadcasted_iota(I32, (t, LANES), 0)
    kpos = lax.broadcasted_iota(I32, (t, LANES), 1)
    new_ref[0] = jnp.where(kpos < t, jnp.where(kpos <= qpos, sc, NEG), -jnp.inf)


def _score_sample(page_table, qi_hm, wi_hm, ki_new, cache_idx_k, *, chunk=16):
    n, n_pages = page_table.shape
    t = ki_new.shape[1]
    ht = qi_hm.shape[1]
    grid_spec = pltpu.PrefetchScalarGridSpec(
        num_scalar_prefetch=1,
        grid=(n,),
        in_specs=[pl.BlockSpec((1, ht, IDX_DIM), lambda b, pt: (b, 0, 0)),
                  pl.BlockSpec((1, ht, 1), lambda b, pt: (b, 0, 0)),
                  pl.BlockSpec((1, t, IDX_DIM), lambda b, pt: (b, 0, 0)),
                  pl.BlockSpec(memory_space=pl.ANY)],
        out_specs=[pl.BlockSpec((1, t, n_pages * PAGE_SIZE), lambda b, pt: (b, 0, 0)),
                   pl.BlockSpec((1, t, LANES), lambda b, pt: (b, 0, 0))],
        scratch_shapes=[pltpu.VMEM((2, n_pages, IDX_DIM, PAGE_SIZE), F32), pltpu.SemaphoreType.DMA((2,))],
    )
    return pl.pallas_call(
        functools.partial(_score_sample_kernel, chunk=chunk),
        grid_spec=grid_spec,
        out_shape=[jax.ShapeDtypeStruct((n, t, n_pages * PAGE_SIZE), F32),
                   jax.ShapeDtypeStruct((n, t, LANES), F32)],
        compiler_params=_cparams(1),
        name="score_sample",
    )(page_table, qi_hm, wi_hm, ki_new, cache_idx_k)


def _sel_sample_kernel(past_ref, new_ref, mpast_ref, mnew_ref, key_ref, mask_ref, *, n_sel, t, tile):
    rows = past_ref.shape[0]
    n_past = past_ref.shape[1] // tile
    for j in range(n_past):
        key_ref[j] = past_ref[:, j * tile:(j + 1) * tile]
    pad = jnp.full((rows, tile - LANES), -jnp.inf, F32)
    key_ref[n_past] = jnp.concatenate([new_ref[...], pad], axis=1)
    qpos = lax.rem(lax.broadcasted_iota(I32, (rows, tile), 0), t)

    def valid(j, shape):
        kpos = lax.broadcasted_iota(I32, shape, 1)
        return kpos <= qpos + jnp.minimum(n_past - j, 1) * tile

    def emit(j, sel):
        mask_ref[j] = sel

    idx_bits = int((n_past + 1) * tile - 1).bit_length()
    _select_mask(key_ref, 1, n_past + 1, n_sel, 0, idx_bits, valid, emit)
    for j in range(n_past):
        mpast_ref[:, j * tile:(j + 1) * tile] = mask_ref[j]
    mnew_ref[...] = mask_ref[n_past][:, :LANES]


def _sel_sample(key_past, key_new, *, n_sel, t, rows=64, tile=2048):
    r, n_keys = key_past.shape
    n_t = n_keys // tile + 1
    return pl.pallas_call(
        functools.partial(_sel_sample_kernel, n_sel=n_sel, t=t, tile=tile),
        grid=(r // rows,),
        in_specs=[pl.BlockSpec((rows, n_keys), lambda i: (i, 0)), pl.BlockSpec((rows, LANES), lambda i: (i, 0))],
        out_specs=[pl.BlockSpec((rows, n_keys), lambda i: (i, 0)), pl.BlockSpec((rows, LANES), lambda i: (i, 0))],
        out_shape=[jax.ShapeDtypeStruct((r, n_keys), F32), jax.ShapeDtypeStruct((r, LANES), F32)],
        scratch_shapes=[pltpu.VMEM((n_t, rows, tile), F32), pltpu.VMEM((n_t, rows, tile), F32)],
        compiler_params=_cparams(1),
        name="sel_sample",
    )(key_past, key_new)


def _attn_sample_kernel(pt_ref, q_ref, kn_ref, vn_ref, mpast_ref, mnew_ref, *rest, pages):
    k_refs = rest[:pages]
    v_refs = rest[pages:2 * pages]
    o_ref, qbd_ref, m_ref, l_ref, acc_ref = rest[2 * pages:]
    t, da = q_ref.shape[1], q_ref.shape[2]
    ht = N_HEADS * t
    j = pl.program_id(1)

    @pl.when(j == 0)
    def _():
        q_rep = jnp.concatenate([q_ref[0].astype(F32)] * N_HEADS, axis=0)
        head_of_row = lax.broadcasted_iota(I32, (ht, da), 0) // t
        head_of_col = lax.broadcasted_iota(I32, (ht, da), 1) // HEAD_DIM
        qbd_ref[...] = jnp.where(head_of_row == head_of_col, q_rep, 0.0).astype(BF16)
        m_ref[...] = jnp.full(m_ref.shape, NEG, F32)
        l_ref[...] = jnp.zeros(l_ref.shape, F32)
        acc_ref[...] = jnp.zeros(acc_ref.shape, F32)

    def attend(k_pages, v_pages, mask_t, feature_major):
        qbd = qbd_ref[...]
        qk = [(_dot(qbd, kp) if feature_major else _dot_nt(qbd, kp)) for kp in k_pages]
        sel = jnp.concatenate([mask_t] * N_HEADS, axis=0) > 0.0
        s = jnp.where(sel, qk[0] if len(qk) == 1 else jnp.concatenate(qk, axis=1), NEG)
        m_old = m_ref[...]
        m_new = jnp.maximum(m_old, jnp.max(s, axis=-1, keepdims=True))
        alpha = jnp.exp(m_old - m_new)
        p = jnp.where(sel, jnp.exp(s - m_new), 0.0)
        l_ref[...] = alpha * l_ref[...] + jnp.sum(p, axis=-1, keepdims=True)
        pb = p.astype(BF16)
        pv = None
        for i, vp in enumerate(v_pages):
            p_i = pb[:, i * PAGE_SIZE:(i + 1) * PAGE_SIZE]
            term = _dot_nt(p_i, vp) if feature_major else _dot(p_i, vp)
            pv = term if pv is None else pv + term
        acc_ref[...] = alpha * acc_ref[...] + pv
        m_ref[...] = m_new

    attend([r[0].astype(BF16) for r in k_refs], [r[0].astype(BF16) for r in v_refs], mpast_ref[0], True)

    @pl.when(j == pl.num_programs(1) - 1)
    def _():
        zeros = jnp.zeros((PAGE_SIZE - t, da), F32)
        attend([jnp.concatenate([kn_ref[0], zeros], axis=0).astype(BF16)],
               [jnp.concatenate([vn_ref[0], zeros], axis=0).astype(BF16)], mnew_ref[0], False)
        out = acc_ref[...] / l_ref[...]
        o_ref[0] = jnp.concatenate(
            [out[h * t:(h + 1) * t, h * HEAD_DIM:(h + 1) * HEAD_DIM] for h in range(N_HEADS)], axis=1).astype(BF16)


def _attn_sample(page_table, q, k_new, v_new, mask_past, mask_new, cache_k, cache_v, *, pages):
    n, n_pages = page_table.shape
    t, da = q.shape[1], q.shape[2]
    steps = n_pages // pages
    ht = N_HEADS * t

    def page_spec(i):
        return pl.BlockSpec((1, da, PAGE_SIZE), lambda b, j, pt: (pt[b, j * pages + i], 0, 0))

    per_seq = lambda shape: pl.BlockSpec((1,) + shape, lambda b, j, pt: (b, 0, 0))
    grid_spec = pltpu.PrefetchScalarGridSpec(
        num_scalar_prefetch=1,
        grid=(n, steps),
        in_specs=[per_seq((t, da)), per_seq((t, da)), per_seq((t, da)),
                  pl.BlockSpec((1, t, pages * PAGE_SIZE), lambda b, j, pt: (b, 0, j)), per_seq((t, LANES))]
                 + [page_spec(i) for i in range(pages)] + [page_spec(i) for i in range(pages)],
        out_specs=per_seq((t, da)),
        scratch_shapes=[pltpu.VMEM((ht, da), BF16), pltpu.VMEM((ht, 1), F32), pltpu.VMEM((ht, 1), F32),
                        pltpu.VMEM((ht, da), F32)],
    )
    return pl.pallas_call(
        functools.partial(_attn_sample_kernel, pages=pages),
        grid_spec=grid_spec,
        out_shape=jax.ShapeDtypeStruct((n, t, da), BF16),
        compiler_params=_cparams(2),
        name="attn_sample",
    )(page_table, q, k_new, v_new, mask_past, mask_new, *([cache_k] * pages), *([cache_v] * pages))


def _merge_kernel(x_ref, h_ref, c_ref, a_ref, m_ref, wg_ref, wc_ref, wa_ref, wm_ref, wo_ref, o_ref):
    d = x_ref.shape[1]
    gates = jax.nn.sigmoid(_dot(h_ref[...], wg_ref[...]))
    mix = (gates[:, :d] * _dot(c_ref[...], wc_ref[...])
           + gates[:, d:2 * d] * _dot(a_ref[...], wa_ref[...])
           + gates[:, 2 * d:] * _dot(m_ref[...], wm_ref[...]))
    o_ref[...] = x_ref[...] + _dot(mix.astype(BF16), wo_ref[...])


def _merge(x, h, c, a, mo, wg, wc, wa, wm, wo, *, tm):
    m, d = x.shape
    row = lambda n: pl.BlockSpec((tm, n), lambda i: (i, 0))
    return pl.pallas_call(
        _merge_kernel,
        grid=(m // tm,),
        in_specs=[row(d), row(d), row(c.shape[1]), row(a.shape[1]), row(mo.shape[1]),
                  _const_spec(wg.shape), _const_spec(wc.shape), _const_spec(wa.shape), _const_spec(wm.shape),
                  _const_spec(wo.shape)],
        out_specs=row(d),
        out_shape=jax.ShapeDtypeStruct((m, d), F32),
        compiler_params=_cparams(1),
        name="merge",
    )(x, h, c, a, mo, wg, wc, wa, wm, wo)


def _ffn_kernel(x_ref, g_ref, wi_ref, wo_ref, gf_ref, o_ref):
    x = x_ref[...]
    dff = wo_ref.shape[0]
    hn = _rmsnorm(x, g_ref[...]).astype(BF16)
    a = _dot(hn, wi_ref[:, :dff])
    b = _dot(hn, wi_ref[:, dff:])
    y = x + _dot((a * jax.nn.sigmoid(a) * b).astype(BF16), wo_ref[...])
    o_ref[...] = _rmsnorm(y, gf_ref[...])


def _ffn(x, g, w_in, w_out, g_final, *, tm):
    m, d = x.shape
    return pl.pallas_call(
        _ffn_kernel,
        grid=(m // tm,),
        in_specs=[pl.BlockSpec((tm, d), lambda i: (i, 0)), _const_spec((1, d)), _const_spec(w_in.shape),
                  _const_spec(w_out.shape), _const_spec((1, d))],
        out_specs=pl.BlockSpec((tm, d), lambda i: (i, 0)),
        out_shape=jax.ShapeDtypeStruct((m, d), F32),
        compiler_params=_cparams(1),
        name="ffn",
    )(x, g, w_in, w_out, g_final)


def kernel(x_prompt, x_sample, mem_prompt, cache_conv, cache_k, cache_v, cache_idx_k, cache_mem_k, cache_mem_v,
           page_table, g_mix, w_in, w_conv_dw, b_conv_dw, g_conv_ln, b_conv_ln, w_conv_out, w_att_out, g_mem,
           w_mem_kv, w_mem_out, w_out, g_ffn, w_ffn_in, w_ffn_out, g_final):
    depth = g_mix.shape[0]
    assert depth == 1
    bsz, seq, d = x_prompt.shape
    n_dec, t_dec, _ = x_sample.shape
    n_mem = mem_prompt.shape[1]
    d_conv = w_conv_dw.shape[2]
    d_att = N_HEADS * HEAD_DIM
    d_idx = N_IDX_HEADS * IDX_DIM
    d_mem = MEM_HEADS * MEM_HEAD_DIM
    n_phys = cache_k.shape[1]
    past = page_table.shape[1] * PAGE_SIZE
    l = 0

    splits = [2 * d_conv, d_att, d_att, d_att, d_idx, IDX_DIM, N_IDX_HEADS, d_mem, 3 * d]
    offs = np.concatenate([[0], np.cumsum(splits)])
    w = w_in[l]
    col = lambda i: w[:, offs[i]:offs[i + 1]]
    pad_cols = lambda a, n: jnp.pad(a, ((0, 0), (0, n - a.shape[1])))
    glu, wq, wk, wv, wqi, wki, wwi, wqm, wgates = (col(i) for i in range(9))
    shared = [glu, wqm, pad_cols(wki, LANES)]
    w_prompt = jnp.concatenate(shared, axis=1).astype(BF16)
    w_prompt_t = jnp.concatenate([wk, wv, wki, pad_cols(wwi, WI_ROWS), wqi, wq], axis=1).T.astype(BF16)
    w_sample = jnp.concatenate(shared + [wq, wqi, pad_cols(wwi, LANES), wk, wv], axis=1).astype(BF16)
    w_gates = wgates.astype(BF16)
    bf = lambda a: a[l].astype(BF16)
    row2 = lambda a: a.reshape(1, -1)
    wdw, bdw, gln, bln = w_conv_dw[l], row2(b_conv_dw[l]), row2(g_conv_ln[l]), row2(b_conv_ln[l])
    dims = dict(d_conv=d_conv, d_att=d_att, d_idx=d_idx, d_mem=d_mem)
    merge = functools.partial(_merge, wg=w_gates, wc=bf(w_conv_out), wa=bf(w_att_out), wm=bf(w_mem_out),
                              wo=bf(w_out))
    ffn = functools.partial(_ffn, g=row2(g_ffn[l]), w_in=bf(w_ffn_in), w_out=bf(w_ffn_out),
                            g_final=row2(g_final))

    mp = bsz * seq
    xp = x_prompt.reshape(mp, d)
    u, qm, h, kib, kb, kt, vt, vtb, kit, qitb, wit, qtb = _inproj_prompt(x_prompt, row2(g_mix[l]), w_prompt,
                                                                         w_prompt_t, tm=512, **dims)
    per_seq = lambda a: a.reshape(bsz, seq, a.shape[-1])
    u3 = per_seq(u)
    c = _conv_prompt(u3, wdw, bdw, gln, bln)
    n_sel = min(TOPK_MAX, seq // 4)
    tq = 256
    mask = _sel_prompt(per_seq(kib), qitb, wit, tq=tq, n_sel=n_sel)
    oa = _attn_prompt(qtb, per_seq(kb), vtb, mask, tq=tq)
    mk, mv = _memkv(mem_prompt.reshape(bsz * n_mem, d), row2(g_mem[l]), bf(w_mem_kv))
    om = _memattn(per_seq(qm), mk.reshape(bsz, n_mem, d_mem), mv.reshape(bsz, n_mem, d_mem), tq=512)
    x1 = merge(xp, h, c.reshape(mp, d_conv), oa.reshape(mp, d_att), om.reshape(mp, d_mem), tm=512)
    y_prompt = ffn(x1, tm=256).reshape(bsz, seq, d)
    conv_state_prompt = u3[:, seq - (CONV_WIDTH - 1):][None]
    k_prompt = kt.reshape(bsz, N_HEADS, HEAD_DIM, seq).transpose(0, 3, 1, 2)[None]
    v_prompt = vt.reshape(bsz, N_HEADS, HEAD_DIM, seq).transpose(0, 3, 1, 2)[None]
    idx_k_prompt = kit.transpose(0, 2, 1)[None]
    mem_k_prompt = mk.reshape(1, bsz, n_mem, MEM_HEADS, MEM_HEAD_DIM)
    mem_v_prompt = mv.reshape(1, bsz, n_mem, MEM_HEADS, MEM_HEAD_DIM)

    ms = n_dec * t_dec
    xs = x_sample.reshape(ms, d)
    u, q, qm, h, qi, wi, k, v, ki = _inproj_sample(xs, row2(g_mix[l]), w_sample, **dims)
    per_seq = lambda a: a.reshape(n_dec, t_dec, a.shape[-1])
    u_ext = jnp.concatenate([cache_conv[l], per_seq(u)], axis=1)
    c = _conv_sample(u_ext, wdw, bdw, gln, bln)
    n_sel = min(TOPK_MAX, (past + t_dec) // 4)
    qi_hm = per_seq(qi).reshape(n_dec, t_dec, N_IDX_HEADS, IDX_DIM).transpose(0, 2, 1, 3)
    qi_hm = qi_hm.reshape(n_dec, N_IDX_HEADS * t_dec, IDX_DIM)
    wi_hm = per_seq(wi).transpose(0, 2, 1).reshape(n_dec, N_IDX_HEADS * t_dec, 1)
    page_major = lambda a: jnp.moveaxis(a[l], 1, -1).reshape(n_phys, -1, PAGE_SIZE)
    key_past, key_new = _score_sample(page_table, qi_hm, wi_hm, per_seq(ki), page_major(cache_idx_k))
    mask_past, mask_new = _sel_sample(key_past.reshape(ms, past), key_new.reshape(ms, LANES), n_sel=n_sel, t=t_dec)
    oa = _attn_sample(page_table, per_seq(q), per_seq(k), per_seq(v), mask_past.reshape(n_dec, t_dec, past),
                      mask_new.reshape(n_dec, t_dec, LANES), page_major(cache_k), page_major(cache_v), pages=32)
    om = _memattn_sample(per_seq(qm), cache_mem_k[l].reshape(n_dec, n_mem * MEM_HEADS, MEM_HEAD_DIM),
                         cache_mem_v[l].reshape(n_dec, n_mem * MEM_HEADS, MEM_HEAD_DIM))
    x1 = merge(xs, h, c.reshape(ms, d_conv), oa.reshape(ms, d_att), om.reshape(ms, d_mem), tm=ms)
    y_sample = ffn(x1, tm=ms).reshape(n_dec, t_dec, d)
    conv_state_sample = u_ext[:, t_dec:][None]
    k_sample = k.reshape(1, n_dec, t_dec, N_HEADS, HEAD_DIM)
    v_sample = v.reshape(1, n_dec, t_dec, N_HEADS, HEAD_DIM)
    idx_k_sample = ki.reshape(1, n_dec, t_dec, IDX_DIM)

    return (y_prompt, y_sample, conv_state_prompt, k_prompt, v_prompt, idx_k_prompt, mem_k_prompt, mem_v_prompt,
            conv_state_sample, k_sample, v_sample, idx_k_sample)
```

```python
import functools

import jax
import jax.numpy as jnp
import numpy as np
from jax import lax
from jax.experimental import pallas as pl
from jax.experimental.pallas import tpu as pltpu

EPS = 1e-6
NEG = -1e30
CONV_WIDTH = 31
N_HEADS = 8
HEAD_DIM = 64
N_IDX_HEADS = 8
IDX_DIM = 64
MEM_HEADS = 4
MEM_HEAD_DIM = 128
TOPK_MAX = 256
PAGE_SIZE = 128
LANES = 128
SUBLANES = 8
VMEM_LIMIT = 56 * 1024 * 1024
INT_MIN = -(2 ** 31)
LOG2E = 1.4426950408889634
TILES_PER_STEP = 2

BF16 = jnp.bfloat16
F32 = jnp.float32
I32 = jnp.int32


def _cparams(n_axes):
    return pltpu.CompilerParams(dimension_semantics=("arbitrary",) * n_axes, vmem_limit_bytes=VMEM_LIMIT)


def _const_spec(shape):
    zeros = (0,) * len(shape)
    return pl.BlockSpec(shape, lambda *_: zeros)


def _rmsnorm(x, g):
    return x * lax.rsqrt(jnp.mean(x * x, axis=-1, keepdims=True) + EPS) * g


def _dot(a, b):
    return jnp.dot(a, b, preferred_element_type=F32)


def _dot_nt(a, b):
    return lax.dot_general(a, b, (((1,), (1,)), ((), ())), preferred_element_type=F32)


WI_ROWS = 16


def _glu_qm(h, w_ref, u_ref, qm_ref, d_conv, d_mem):
    o = 0
    glu = _dot(h, w_ref[:, o:o + 2 * d_conv]); o += 2 * d_conv
    u_ref[...] = glu[:, :d_conv] * jax.nn.sigmoid(glu[:, d_conv:])
    qm_ref[...] = _dot(h, w_ref[:, o:o + d_mem]).astype(BF16); o += d_mem
    return o


def _inproj_prompt_kernel(x_ref, g_ref, w_ref, wt_ref, u_ref, qm_ref, h_ref, kib_ref, kb_ref,
                          kt_ref, vt_ref, vtb_ref, kit_ref, qit_ref, wit_ref, qt_ref, *, d_conv, d_att, d_idx, d_mem):
    h = _rmsnorm(x_ref[0], g_ref[...]).astype(BF16)
    h_ref[...] = h
    o = _glu_qm(h, w_ref, u_ref, qm_ref, d_conv, d_mem)
    kib_ref[...] = _dot(h, w_ref[:, o:o + LANES])[:, :IDX_DIM].astype(BF16)
    r = 0
    kt = _dot_nt(wt_ref[r:r + d_att, :], h); r += d_att
    kt_ref[0] = kt
    kb_ref[...] = kt.T.astype(BF16)
    vt = _dot_nt(wt_ref[r:r + d_att, :], h); r += d_att
    vt_ref[0] = vt
    vtb_ref[0] = vt.astype(BF16)
    kit_ref[0] = _dot_nt(wt_ref[r:r + IDX_DIM, :], h); r += IDX_DIM
    wit_ref[0] = _dot_nt(wt_ref[r:r + WI_ROWS, :], h)[:N_IDX_HEADS] * (N_IDX_HEADS ** -0.5); r += WI_ROWS
    qit_ref[0] = (_dot_nt(wt_ref[r:r + d_idx, :], h) * (IDX_DIM ** -0.5)).astype(BF16); r += d_idx
    qt_ref[0] = (_dot_nt(wt_ref[r:r + d_att, :], h) * (HEAD_DIM ** -0.5)).astype(BF16)


def _inproj_prompt(x, g, w, wt, *, tm, d_conv, d_att, d_idx, d_mem):
    bsz, s, d = x.shape
    nt = s // tm
    row = lambda n: pl.BlockSpec((tm, n), lambda b, i: (b * nt + i, 0))
    col = lambda n: pl.BlockSpec((1, n, tm), lambda b, i: (b, 0, i))
    outs = [(d_conv, F32), (d_mem, BF16), (d, BF16), (IDX_DIM, BF16), (d_att, BF16)]
    outs_t = [(d_att, F32), (d_att, F32), (d_att, BF16), (IDX_DIM, F32), (d_idx, BF16), (N_IDX_HEADS, F32),
              (d_att, BF16)]
    return pl.pallas_call(
        functools.partial(_inproj_prompt_kernel, d_conv=d_conv, d_att=d_att, d_idx=d_idx, d_mem=d_mem),
        grid=(bsz, nt),
        in_specs=[pl.BlockSpec((1, tm, d), lambda b, i: (b, i, 0)), _const_spec((1, d)), _const_spec(w.shape),
                  _const_spec(wt.shape)],
        out_specs=[row(n) for n, _ in outs] + [col(n) for n, _ in outs_t],
        out_shape=[jax.ShapeDtypeStruct((bsz * s, n), dt) for n, dt in outs]
                  + [jax.ShapeDtypeStruct((bsz, n, s), dt) for n, dt in outs_t],
        compiler_params=_cparams(2),
        name="inproj_prompt",
    )(x, g, w, wt)


def _inproj_sample_kernel(x_ref, g_ref, w_ref, u_ref, q_ref, qm_ref, h_ref, qi_ref, wi_ref, k_ref, v_ref, ki_ref,
                          *, d_conv, d_att, d_idx, d_mem):
    h = _rmsnorm(x_ref[...], g_ref[...]).astype(BF16)
    h_ref[...] = h
    o = _glu_qm(h, w_ref, u_ref, qm_ref, d_conv, d_mem)
    ki_ref[...] = _dot(h, w_ref[:, o:o + LANES])[:, :IDX_DIM]; o += LANES
    q_ref[...] = (_dot(h, w_ref[:, o:o + d_att]) * (HEAD_DIM ** -0.5)).astype(BF16); o += d_att
    qi_ref[...] = (_dot(h, w_ref[:, o:o + d_idx]) * (IDX_DIM ** -0.5)).astype(BF16); o += d_idx
    wi_ref[...] = _dot(h, w_ref[:, o:o + LANES])[:, :N_IDX_HEADS] * (N_IDX_HEADS ** -0.5); o += LANES
    k_ref[...] = _dot(h, w_ref[:, o:o + d_att]); o += d_att
    v_ref[...] = _dot(h, w_ref[:, o:o + d_att])


def _inproj_sample(x, g, w, *, d_conv, d_att, d_idx, d_mem):
    m, d = x.shape
    outs = [(d_conv, F32), (d_att, BF16), (d_mem, BF16), (d, BF16), (d_idx, BF16), (N_IDX_HEADS, F32),
            (d_att, F32), (d_att, F32), (IDX_DIM, F32)]
    return pl.pallas_call(
        functools.partial(_inproj_sample_kernel, d_conv=d_conv, d_att=d_att, d_idx=d_idx, d_mem=d_mem),
        grid=(1,),
        in_specs=[_const_spec((m, d)), _const_spec((1, d)), _const_spec(w.shape)],
        out_specs=[_const_spec((m, n)) for n, _ in outs],
        out_shape=[jax.ShapeDtypeStruct((m, n), dt) for n, dt in outs],
        compiler_params=_cparams(1),
        name="inproj_sample",
    )(x, g, w)


def _conv_taps(ext_ref, start, rows, w_ref, by_phase):
    if not by_phase:
        acc = w_ref[0:1, :] * ext_ref[pl.ds(start, rows), :]
        for j in range(1, CONV_WIDTH):
            acc = acc + w_ref[j:j + 1, :] * ext_ref[pl.ds(start + j, rows), :]
        return acc
    acc = None
    for r in range(SUBLANES):
        y = None
        for j in range(CONV_WIDTH):
            if (start + j) % SUBLANES == r:
                term = w_ref[j:j + 1, :] * ext_ref[pl.ds(start + j - r, rows + (SUBLANES if r else 0)), :]
                y = term if y is None else y + term
        if y is not None:
            y = y[r:r + rows] if r else y
            acc = y if acc is None else acc + y
    return acc


def _conv_rows(ext_ref, start, rows, w_ref, b_ref, g_ref, bl_ref, by_phase=False):
    c = _conv_taps(ext_ref, start, rows, w_ref, by_phase) + b_ref[...]
    mu = jnp.mean(c, axis=-1, keepdims=True)
    xc = c - mu
    y = xc * lax.rsqrt(jnp.mean(xc * xc, axis=-1, keepdims=True) + EPS) * g_ref[...] + bl_ref[...]
    return (y * jax.nn.sigmoid(y)).astype(BF16)


def _conv_prompt_kernel(prev_ref, cur_ref, w_ref, b_ref, g_ref, bl_ref, o_ref, ext_ref, *, halo, chunk):
    t = cur_ref.shape[1]

    @pl.when(pl.program_id(1) == 0)
    def _():
        ext_ref[0:halo, :] = jnp.zeros((halo, ext_ref.shape[1]), F32)

    @pl.when(pl.program_id(1) > 0)
    def _():
        ext_ref[0:halo, :] = prev_ref[0]

    ext_ref[halo:halo + t, :] = cur_ref[0]
    ext_ref[halo + t:, :] = jnp.zeros((SUBLANES, ext_ref.shape[1]), F32)
    first = halo - (CONV_WIDTH - 1)
    for c in range(t // chunk):
        o_ref[0, c * chunk:(c + 1) * chunk, :] = _conv_rows(ext_ref, first + c * chunk, chunk, w_ref, b_ref, g_ref,
                                                            bl_ref, by_phase=True)


def _conv_prompt(u, w, b, g, bl, *, t=512, halo=32, chunk=128):
    bsz, s, dc = u.shape
    assert s % t == 0 and t % halo == 0 and halo >= CONV_WIDTH - 1 and t % chunk == 0
    r = t // halo
    return pl.pallas_call(
        functools.partial(_conv_prompt_kernel, halo=halo, chunk=chunk),
        grid=(bsz, s // t),
        in_specs=[pl.BlockSpec((1, halo, dc), lambda bi, i: (bi, jnp.maximum(i * r - 1, 0), 0)),
                  pl.BlockSpec((1, t, dc), lambda bi, i: (bi, i, 0)),
                  _const_spec(w.shape), _const_spec((1, dc)), _const_spec((1, dc)), _const_spec((1, dc))],
        out_specs=pl.BlockSpec((1, t, dc), lambda bi, i: (bi, i, 0)),
        out_shape=jax.ShapeDtypeStruct((bsz, s, dc), BF16),
        scratch_shapes=[pltpu.VMEM((halo + t + SUBLANES, dc), F32)],
        compiler_params=_cparams(2),
        name="conv_prompt",
    )(u, u, w, b, g, bl)


def _conv_sample_kernel(ext_ref, w_ref, b_ref, g_ref, bl_ref, o_ref):
    t = o_ref.shape[1]
    for s in range(o_ref.shape[0]):
        o_ref[s] = _conv_rows(ext_ref.at[s], 0, t, w_ref, b_ref, g_ref, bl_ref)


def _conv_sample(u_ext, w, b, g, bl, *, seqs=8):
    n, te, dc = u_ext.shape
    t = te - (CONV_WIDTH - 1)
    return pl.pallas_call(
        _conv_sample_kernel,
        grid=(n // seqs,),
        in_specs=[pl.BlockSpec((seqs, te, dc), lambda i: (i, 0, 0)),
                  _const_spec(w.shape), _const_spec((1, dc)), _const_spec((1, dc)), _const_spec((1, dc))],
        out_specs=pl.BlockSpec((seqs, t, dc), lambda i: (i, 0, 0)),
        out_shape=jax.ShapeDtypeStruct((n, t, dc), BF16),
        compiler_params=_cparams(1),
        name="conv_sample",
    )(u_ext, w, b, g, bl)


def _memkv_kernel(x_ref, g_ref, w_ref, k_ref, v_ref):
    h = _rmsnorm(x_ref[...], g_ref[...]).astype(BF16)
    kv = _dot(h, w_ref[...])
    d = k_ref.shape[1]
    k_ref[...] = kv[:, :d]
    v_ref[...] = kv[:, d:]


def _memkv(x, g, w, *, tm=256):
    m, d = x.shape
    dm = w.shape[1] // 2
    return pl.pallas_call(
        _memkv_kernel,
        grid=(m // tm,),
        in_specs=[pl.BlockSpec((tm, d), lambda i: (i, 0)), _const_spec((1, d)), _const_spec(w.shape)],
        out_specs=[pl.BlockSpec((tm, dm), lambda i: (i, 0))] * 2,
        out_shape=[jax.ShapeDtypeStruct((m, dm), F32)] * 2,
        compiler_params=_cparams(1),
        name="memkv",
    )(x, g, w)


def _memattn_kernel(q_ref, k_ref, v_ref, o_ref):
    for h in range(MEM_HEADS):
        sl = slice(h * MEM_HEAD_DIM, (h + 1) * MEM_HEAD_DIM)
        s = _dot_nt(q_ref[0, :, sl], k_ref[0, :, sl].astype(BF16)) * (MEM_HEAD_DIM ** -0.5)
        e = jnp.exp(s - jnp.max(s, axis=-1, keepdims=True))
        p = e / jnp.sum(e, axis=-1, keepdims=True)
        o_ref[0, :, sl] = _dot(p.astype(BF16), v_ref[0, :, sl].astype(BF16)).astype(BF16)


def _memattn(q, k, v, *, tq):
    n, t, dm = q.shape
    nm = k.shape[1]
    return pl.pallas_call(
        _memattn_kernel,
        grid=(n, t // tq),
        in_specs=[pl.BlockSpec((1, tq, dm), lambda i, j: (i, j, 0)),
                  pl.BlockSpec((1, nm, dm), lambda i, j: (i, 0, 0)),
                  pl.BlockSpec((1, nm, dm), lambda i, j: (i, 0, 0))],
        out_specs=pl.BlockSpec((1, tq, dm), lambda i, j: (i, j, 0)),
        out_shape=jax.ShapeDtypeStruct((n, t, dm), BF16),
        compiler_params=_cparams(2),
        name="memattn",
    )(q, k, v)


def _memattn_sample_kernel(q_ref, k_ref, v_ref, o_ref):
    n_mem = k_ref.shape[1] // MEM_HEADS
    for g in range(q_ref.shape[0]):
        for h in range(MEM_HEADS):
            sl = slice(h * MEM_HEAD_DIM, (h + 1) * MEM_HEAD_DIM)
            head_rows = pl.ds(h, n_mem, stride=MEM_HEADS)
            s = _dot_nt(q_ref[g, :, sl], k_ref[g, head_rows, :].astype(BF16)) * (MEM_HEAD_DIM ** -0.5)
            e = jnp.exp(s - jnp.max(s, axis=-1, keepdims=True))
            p = e / jnp.sum(e, axis=-1, keepdims=True)
            o_ref[g, :, sl] = _dot(p.astype(BF16), v_ref[g, head_rows, :].astype(BF16)).astype(BF16)


def _memattn_sample(q, k, v, *, seqs=4):
    n, t, dm = q.shape
    rows = k.shape[1]
    return pl.pallas_call(
        _memattn_sample_kernel,
        grid=(n // seqs,),
        in_specs=[pl.BlockSpec((seqs, t, dm), lambda i: (i, 0, 0)),
                  pl.BlockSpec((seqs, rows, MEM_HEAD_DIM), lambda i: (i, 0, 0)),
                  pl.BlockSpec((seqs, rows, MEM_HEAD_DIM), lambda i: (i, 0, 0))],
        out_specs=pl.BlockSpec((seqs, t, dm), lambda i: (i, 0, 0)),
        out_shape=jax.ShapeDtypeStruct((n, t, dm), BF16),
        compiler_params=_cparams(1),
        name="memattn_sample",
    )(q, k, v)


def _ordinal_to_float(o):
    return pltpu.bitcast(jnp.where(o >= 0, o, (-o) | INT_MIN), F32)


def _float_to_ordinal(x):
    b = pltpu.bitcast(x, I32)
    return jnp.where(b < 0, -(b & 0x7FFFFFFF), b)


ORD_NEG_INF = -0x7F800000
ORD_MIN_NORMAL = 0x00800000
FIRST_CHECK = 18
CHECK_EVERY = 2
VALUE_PROBES = 48
MAX_PROBES = VALUE_PROBES + 34


def _select_mask(score_ref, key_axis, n_tiles, n_sel, outside, idx_bits, valid_fn, emit_fn, extreme_parts=None):
    _, ta, tb = score_ref.shape
    tile = (ta, tb)
    unit = LANES if key_axis == 1 else 4 * SUBLANES
    tk = tile[key_axis]
    fold = tk // unit
    shp = (ta, LANES) if key_axis == 1 else (unit, tb)
    k_f = jnp.float32(n_sel)
    out_f = jnp.float32(1.0) * outside

    def key_fold(x, op=jnp.add):
        if key_axis == 0:
            x = x.reshape(fold, unit, tb)
            acc = x[0]
            for i in range(1, fold):
                acc = op(acc, x[i])
            return acc
        acc = x[:, :LANES]
        for i in range(1, fold):
            acc = op(acc, x[:, i * LANES:(i + 1) * LANES])
        return acc

    def rep(x):
        return x if fold == 1 else jnp.concatenate([x] * fold, axis=key_axis)

    def row_total(part):
        return jnp.broadcast_to(jnp.sum(part, axis=key_axis, keepdims=True), shp)

    def key_index(j):
        return j * tk + lax.broadcasted_iota(I32, tile, key_axis)

    def count_ge(thr):
        thr_t = rep(thr)

        def body(j, part):
            return part + key_fold(jnp.where(score_ref[j] >= thr_t, 1.0, 0.0))

        part = lax.fori_loop(0, n_tiles, body, jnp.zeros(shp, F32))
        return row_total(part) + jnp.where(thr <= NEG, out_f, 0.0)

    def extremes(j, carry):
        mx, mn = carry
        sc = score_ref[j]
        return (jnp.maximum(mx, key_fold(sc, jnp.maximum)),
                jnp.minimum(mn, key_fold(jnp.where(sc > NEG, sc, jnp.inf), jnp.minimum)))

    if extreme_parts is None:
        extreme_parts = lax.fori_loop(0, n_tiles, extremes,
                                      (jnp.full(shp, -jnp.inf, F32), jnp.full(shp, jnp.inf, F32)))
    mx, mn = extreme_parts
    mx = jnp.broadcast_to(jnp.max(mx, axis=key_axis, keepdims=True), shp)
    mx = jnp.maximum(mx, jnp.where(out_f > 0.0, NEG, -jnp.inf))
    mn_o = _float_to_ordinal(jnp.broadcast_to(jnp.min(mn, axis=key_axis, keepdims=True), shp))

    def is_settled(lo_o, hi_o, c_lo):
        return ((c_lo == k_f) | (hi_o == lo_o + 1) | ((lo_o >= 0) & (hi_o <= ORD_MIN_NORMAL))
                | ((hi_o <= 0) & (lo_o >= -ORD_MIN_NORMAL)))

    def probe(it, state):
        lo_o, hi_o, c_lo = state
        settled = is_settled(lo_o, hi_o, c_lo)
        omid = (lo_o >> 1) + (hi_o >> 1) + (lo_o & hi_o & 1)
        vmid = _float_to_ordinal(0.5 * _ordinal_to_float(lo_o) + 0.5 * _ordinal_to_float(hi_o))
        one_sign = (lo_o >= 0) | (hi_o <= 0)
        cand = jnp.where(one_sign, vmid, omid)
        cand = jnp.where(it == 0, mn_o, jnp.where(it == 1, 0, jnp.where(it == 2, ORD_MIN_NORMAL, cand)))
        cand = jnp.where(it < VALUE_PROBES, cand, omid)
        mid = jnp.where((cand > lo_o) & (cand < hi_o), cand, omid)
        cnt = count_ge(_ordinal_to_float(mid))
        up = (cnt >= k_f) & ~settled
        down = (cnt < k_f) & ~settled
        return jnp.where(up, mid, lo_o), jnp.where(down, mid, hi_o), jnp.where(up, cnt, c_lo)

    def unsettled(state):
        return jnp.max(jnp.where(is_settled(*state), 0.0, 1.0)).astype(I32)

    def probe_block(carry):
        it, state, _ = carry
        state = lax.fori_loop(it, it + CHECK_EVERY, probe, state)
        return it + CHECK_EVERY, state, unsettled(state)

    total = jnp.float32(1.0) * (n_tiles * tk) + out_f
    state = (jnp.full(shp, ORD_NEG_INF, I32), _float_to_ordinal(mx) + 1, jnp.broadcast_to(total, shp))
    state = lax.fori_loop(0, FIRST_CHECK, probe, state)
    _, (lo_o, _, cnt_ge), _ = lax.while_loop(lambda c: (c[2] > 0) & (c[0] < MAX_PROBES), probe_block,
                                             (jnp.int32(FIRST_CHECK), state, unsettled(state)))
    thr = _ordinal_to_float(lo_o)
    thr_t = rep(thr)

    surplus = jnp.max(jnp.where(cnt_ge > k_f, 1.0, 0.0)) > 0.0

    def tie_search():
        def count_gt_body(j, part):
            return part + key_fold(jnp.where(score_ref[j] > thr_t, 1.0, 0.0))

        cnt_gt = row_total(lax.fori_loop(0, n_tiles, count_gt_body, jnp.zeros(shp, F32)))
        cnt_gt = cnt_gt + jnp.where(thr < NEG, out_f, 0.0)
        in_tiles = n_tiles * tk

        def idx_step(it, v):
            bit = lax.shift_left(jnp.int32(1), idx_bits - 1 - it)
            cand = v | bit
            cand_t = rep(cand)

            def body(j, part):
                hit = jnp.where(score_ref[j] == thr_t, jnp.where(key_index(j) < cand_t, 1.0, 0.0), 0.0)
                return part + key_fold(hit)

            ties_below = row_total(lax.fori_loop(0, n_tiles, body, jnp.zeros(shp, F32)))
            out_below = jnp.clip(cand - in_tiles, 0, outside).astype(F32)
            ties_below = ties_below + jnp.where(thr == NEG, out_below, 0.0)
            return jnp.where(cnt_gt + ties_below < k_f, cand, v)

        return lax.fori_loop(0, idx_bits, idx_step, jnp.zeros(shp, I32))

    jmax = lax.cond(surplus, tie_search, lambda: jnp.full(shp, 2 ** idx_bits - 1, I32))
    jmax_t = rep(jmax)

    def emit(j, _):
        sc = score_ref[j]
        sel = jnp.where(sc > thr_t, 1.0, jnp.where(sc == thr_t, jnp.where(key_index(j) <= jmax_t, 1.0, 0.0), 0.0))
        emit_fn(j, jnp.where(valid_fn(j, tile), sel, 0.0))
        return 0

    lax.fori_loop(0, n_tiles, emit, 0)


def _sel_prompt_kernel(kib_ref, qit_ref, wit_ref, mask_ref, sc_ref, *, n_sel):
    tq = qit_ref.shape[2]
    s = kib_ref.shape[1]
    n_all = s // tq
    qb = pl.program_id(1)
    n_tiles = qb + 1
    qpos = qb * tq + lax.broadcasted_iota(I32, (tq, tq), 1)

    def score_tile(j):
        ki_t = kib_ref[0, pl.ds(pl.multiple_of(j * tq, tq), tq), :]
        acc = jnp.zeros((tq, tq), F32)
        for h in range(N_IDX_HEADS):
            d = _dot(ki_t, qit_ref[0, h * IDX_DIM:(h + 1) * IDX_DIM, :])
            acc = acc + wit_ref[0, h:h + 1, :] * jnp.maximum(d, 0.0)
        return acc

    part = 4 * SUBLANES

    def extremes(sc, mx, mn):
        sc = sc.reshape(tq // part, part, tq)
        return (jnp.maximum(mx, jnp.max(sc, axis=0)),
                jnp.minimum(mn, jnp.min(jnp.where(sc > NEG, sc, jnp.inf), axis=0)))

    def below(j, carry):
        sc = score_tile(j)
        sc_ref[j] = sc
        return extremes(sc, *carry)

    mx, mn = lax.fori_loop(0, qb, below, (jnp.full((part, tq), -jnp.inf, F32), jnp.full((part, tq), jnp.inf, F32)))
    kpos = qb * tq + lax.broadcasted_iota(I32, (tq, tq), 0)
    diag = jnp.where(kpos <= qpos, score_tile(qb), NEG)
    sc_ref[qb] = diag
    mx, mn = extremes(diag, mx, mn)

    def valid(j, shape):
        return j * tq + lax.broadcasted_iota(I32, shape, 0) <= qpos

    def emit(j, sel):
        mask_ref[0, 0, j] = sel.astype(BF16)

    idx_bits = max(1, int(s - 1).bit_length())
    _select_mask(sc_ref, 0, n_tiles, n_sel, (n_all - n_tiles) * tq, idx_bits, valid, emit, (mx, mn))

    def clear(j, _):
        mask_ref[0, 0, j] = jnp.zeros((tq, tq), BF16)
        return 0

    lax.fori_loop(n_tiles, n_all, clear, 0)


def _sel_prompt(kib, qitb, wit, *, tq, n_sel):
    bsz, s, _ = kib.shape
    nq = s // tq
    return pl.pallas_call(
        functools.partial(_sel_prompt_kernel, n_sel=n_sel),
        grid=(bsz, nq),
        in_specs=[pl.BlockSpec((1, s, kib.shape[2]), lambda b, i: (b, 0, 0)),
                  pl.BlockSpec((1, qitb.shape[1], tq), lambda b, i: (b, 0, i)),
                  pl.BlockSpec((1, wit.shape[1], tq), lambda b, i: (b, 0, i))],
        out_specs=pl.BlockSpec((1, 1, nq, tq, tq), lambda b, i: (b, i, 0, 0, 0)),
        out_shape=jax.ShapeDtypeStruct((bsz, nq, nq, tq, tq), BF16),
        scratch_shapes=[pltpu.VMEM((nq, tq, tq), F32)],
        compiler_params=_cparams(2),
        name="sel_prompt",
    )(kib, qitb, wit)


def _attn_prompt_kernel(qt_ref, k_ref, vt_ref, mask_ref, o_ref, q2_ref, s_ref, mx_ref, l_ref, out_ref):
    tq = qt_ref.shape[2]
    n_tiles = pl.program_id(1) + 1
    pair = 2 * HEAD_DIM
    part = 4 * SUBLANES
    fold = tq // part
    in_pair = lax.broadcasted_iota(I32, (pair, tq), 0)
    for hp in range(N_HEADS // 2):
        qp = qt_ref[0, hp * pair:(hp + 1) * pair, :]
        q2_ref[2 * hp] = jnp.where(in_pair < HEAD_DIM, qp, jnp.zeros_like(qp))
        q2_ref[2 * hp + 1] = jnp.where(in_pair >= HEAD_DIM, qp, jnp.zeros_like(qp))

    def key_rows(j):
        return pl.ds(pl.multiple_of(j * tq, tq), tq)

    group = s_ref.shape[1]
    n_groups = N_HEADS // group

    def score_tile(j, sel, slot, h0):
        for g in range(group):
            h = h0 + g
            k_pair = k_ref[0, key_rows(j), (h // 2) * pair:(h // 2 + 1) * pair]
            s = jnp.where(sel, _dot(k_pair, q2_ref[h]) * LOG2E, NEG)
            s_ref[slot, g, j] = s
            mx_ref[slot, g] = jnp.maximum(mx_ref[slot, g], jnp.max(s.reshape(fold, part, tq), axis=0))

    ones_rows = jnp.ones((2 * SUBLANES, tq), BF16)

    def weigh_tile(j, slot, h0, m):
        for g in range(group):
            rows = slice((h0 + g) * HEAD_DIM, (h0 + g + 1) * HEAD_DIM)
            p = jnp.exp2(s_ref[slot, g, j] - m[g])
            v_ones = jnp.concatenate([vt_ref[0, rows, key_rows(j)], ones_rows], axis=0)
            pv = _dot(v_ones, p.astype(BF16))
            out_ref[rows, :] = out_ref[rows, :] + pv[:HEAD_DIM]
            l_ref[g] = l_ref[g] + pv[HEAD_DIM:HEAD_DIM + SUBLANES]

    for gi in range(n_groups + 1):
        slot, prev = gi % 2, (gi - 1) % 2
        h0, h_prev = gi * group, (gi - 1) * group
        m_prev = None
        if gi > 0:
            m_prev = [jnp.max(mx_ref[prev, g], axis=0, keepdims=True) for g in range(group)]
            l_ref[...] = jnp.zeros(l_ref.shape, F32)
            out_ref[h_prev * HEAD_DIM:h0 * HEAD_DIM, :] = jnp.zeros((group * HEAD_DIM, tq), F32)
        if gi < n_groups:
            mx_ref[slot] = jnp.full(mx_ref.shape[1:], NEG, F32)

        def sweep(jj, _, gi=gi, slot=slot, prev=prev, h0=h0, h_prev=h_prev, m_prev=m_prev):
            for i in range(TILES_PER_STEP):
                j = jj * TILES_PER_STEP + i
                if gi < n_groups:
                    score_tile(j, mask_ref[0, 0, j].astype(F32) > 0.0, slot, h0)
                if gi > 0:
                    weigh_tile(j, prev, h_prev, m_prev)
            return 0

        lax.fori_loop(0, pl.cdiv(n_tiles, TILES_PER_STEP), sweep, 0)
        if gi > 0:
            for g in range(group):
                rows = slice((h_prev + g) * HEAD_DIM, (h_prev + g + 1) * HEAD_DIM)
                out_ref[rows, :] = out_ref[rows, :] / l_ref[g, 0:1, :]
    o_ref[0] = out_ref[...].T.astype(BF16)


def _attn_prompt(qtb, kb, vtb, mask, *, tq, group=2):
    bsz, da, s = qtb.shape
    nq = s // tq
    part = 4 * SUBLANES
    return pl.pallas_call(
        _attn_prompt_kernel,
        grid=(bsz, nq),
        in_specs=[pl.BlockSpec((1, da, tq), lambda b, i: (b, 0, i)),
                  pl.BlockSpec((1, s, da), lambda b, i: (b, 0, 0)),
                  pl.BlockSpec((1, da, s), lambda b, i: (b, 0, 0)),
                  pl.BlockSpec((1, 1, nq, tq, tq), lambda b, i: (b, i, 0, 0, 0))],
        out_specs=pl.BlockSpec((1, tq, da), lambda b, i: (b, i, 0)),
        out_shape=jax.ShapeDtypeStruct((bsz, s, da), BF16),
        scratch_shapes=[pltpu.VMEM((N_HEADS, 2 * HEAD_DIM, tq), BF16), pltpu.VMEM((2, group, nq, tq, tq), F32),
                        pltpu.VMEM((2, group, part, tq), F32), pltpu.VMEM((group, SUBLANES, tq), F32),
                        pltpu.VMEM((da, tq), F32)],
        compiler_params=_cparams(2),
        name="attn_prompt",
    )(qtb, kb, vtb, mask)


def _score_sample_kernel(pt_ref, qi_ref, wi_ref, kin_ref, cache_ref, past_ref, new_ref, buf_ref, sem_ref, *, chunk):
    seq = pl.program_id(0)
    n_seq = pl.num_programs(0)
    n_pages = buf_ref.shape[1]
    t = new_ref.shape[1]
    slot = lax.rem(seq, 2)

    def page_copy(s, i, buf_slot):
        return pltpu.make_async_copy(cache_ref.at[pt_ref[s, i]], buf_ref.at[buf_slot, i], sem_ref.at[buf_slot])

    def start_pages(s, buf_slot):
        def body(i, _):
            page_copy(s, i, buf_slot).start()
            return 0
        lax.fori_loop(0, n_pages, body, 0)

    @pl.when(seq == 0)
    def _():
        start_pages(seq, slot)

    @pl.when(seq + 1 < n_seq)
    def _():
        start_pages(seq + 1, 1 - slot)

    def wait_body(i, _):
        page_copy(seq, i, slot).wait()
        return 0

    lax.fori_loop(0, n_pages, wait_body, 0)

    qi = qi_ref[0]
    wi = wi_ref[0]

    def scores(dots):
        w = wi * jnp.maximum(dots, 0.0)
        acc = w[0:t]
        for h in range(1, N_IDX_HEADS):
            acc = acc + w[h * t:(h + 1) * t]
        return acc

    for c in range(n_pages // chunk):
        keys_t = jnp.concatenate([buf_ref[slot, c * chunk + i].astype(BF16) for i in range(chunk)], axis=1)
        past_ref[0, :, c * chunk * PAGE_SIZE:(c + 1) * chunk * PAGE_SIZE] = scores(_dot(qi, keys_t))

    ext = jnp.concatenate([kin_ref[0], jnp.zeros((LANES - t, IDX_DIM), F32)], axis=0).astype(BF16)
    sc = scores(_dot_nt(qi, ext))
    qpos = lax.broadcasted_iota(I32, (t, LANES), 0)
    kpos = lax.broadcasted_iota(I32, (t, LANES), 1)
    new_ref[0] = jnp.where(kpos < t, jnp.where(kpos <= qpos, sc, NEG), -jnp.inf)


def _score_sample(page_table, qi_hm, wi_hm, ki_new, cache_idx_k, *, chunk=16):
    n, n_pages = page_table.shape
    t = ki_new.shape[1]
    ht = qi_hm.shape[1]
    grid_spec = pltpu.PrefetchScalarGridSpec(
        num_scalar_prefetch=1,
        grid=(n,),
        in_specs=[pl.BlockSpec((1, ht, IDX_DIM), lambda b, pt: (b, 0, 0)),
                  pl.BlockSpec((1, ht, 1), lambda b, pt: (b, 0, 0)),
                  pl.BlockSpec((1, t, IDX_DIM), lambda b, pt: (b, 0, 0)),
                  pl.BlockSpec(memory_space=pl.ANY)],
        out_specs=[pl.BlockSpec((1, t, n_pages * PAGE_SIZE), lambda b, pt: (b, 0, 0)),
                   pl.BlockSpec((1, t, LANES), lambda b, pt: (b, 0, 0))],
        scratch_shapes=[pltpu.VMEM((2, n_pages, IDX_DIM, PAGE_SIZE), F32), pltpu.SemaphoreType.DMA((2,))],
    )
    return pl.pallas_call(
        functools.partial(_score_sample_kernel, chunk=chunk),
        grid_spec=grid_spec,
        out_shape=[jax.ShapeDtypeStruct((n, t, n_pages * PAGE_SIZE), F32),
                   jax.ShapeDtypeStruct((n, t, LANES), F32)],
        compiler_params=_cparams(1),
        name="score_sample",
    )(page_table, qi_hm, wi_hm, ki_new, cache_idx_k)


def _sel_sample_kernel(past_ref, new_ref, mpast_ref, mnew_ref, key_ref, mask_ref, *, n_sel, t, tile):
    rows = past_ref.shape[0]
    n_past = past_ref.shape[1] // tile
    for j in range(n_past):
        key_ref[j] = past_ref[:, j * tile:(j + 1) * tile]
    pad = jnp.full((rows, tile - LANES), -jnp.inf, F32)
    key_ref[n_past] = jnp.concatenate([new_ref[...], pad], axis=1)
    qpos = lax.rem(lax.broadcasted_iota(I32, (rows, tile), 0), t)

    def valid(j, shape):
        kpos = lax.broadcasted_iota(I32, shape, 1)
        return kpos <= qpos + jnp.minimum(n_past - j, 1) * tile

    def emit(j, sel):
        mask_ref[j] = sel

    idx_bits = int((n_past + 1) * tile - 1).bit_length()
    _select_mask(key_ref, 1, n_past + 1, n_sel, 0, idx_bits, valid, emit)
    for j in range(n_past):
        mpast_ref[:, j * tile:(j + 1) * tile] = mask_ref[j]
    mnew_ref[...] = mask_ref[n_past][:, :LANES]


def _sel_sample(key_past, key_new, *, n_sel, t, rows=64, tile=2048):
    r, n_keys = key_past.shape
    n_t = n_keys // tile + 1
    return pl.pallas_call(
        functools.partial(_sel_sample_kernel, n_sel=n_sel, t=t, tile=tile),
        grid=(r // rows,),
        in_specs=[pl.BlockSpec((rows, n_keys), lambda i: (i, 0)), pl.BlockSpec((rows, LANES), lambda i: (i, 0))],
        out_specs=[pl.BlockSpec((rows, n_keys), lambda i: (i, 0)), pl.BlockSpec((rows, LANES), lambda i: (i, 0))],
        out_shape=[jax.ShapeDtypeStruct((r, n_keys), F32), jax.ShapeDtypeStruct((r, LANES), F32)],
        scratch_shapes=[pltpu.VMEM((n_t, rows, tile), F32), pltpu.VMEM((n_t, rows, tile), F32)],
        compiler_params=_cparams(1),
        name="sel_sample",
    )(key_past, key_new)


def _attn_sample_kernel(pt_ref, q_ref, kn_ref, vn_ref, mpast_ref, mnew_ref, *rest, pages):
    k_refs = rest[:pages]
    v_refs = rest[pages:2 * pages]
    o_ref, qbd_ref, m_ref, l_ref, acc_ref = rest[2 * pages:]
    t, da = q_ref.shape[1], q_ref.shape[2]
    ht = N_HEADS * t
    j = pl.program_id(1)

    @pl.when(j == 0)
    def _():
        q_rep = jnp.concatenate([q_ref[0].astype(F32)] * N_HEADS, axis=0)
        head_of_row = lax.broadcasted_iota(I32, (ht, da), 0) // t
        head_of_col = lax.broadcasted_iota(I32, (ht, da), 1) // HEAD_DIM
        qbd_ref[...] = jnp.where(head_of_row == head_of_col, q_rep, 0.0).astype(BF16)
        m_ref[...] = jnp.full(m_ref.shape, NEG, F32)
        l_ref[...] = jnp.zeros(l_ref.shape, F32)
        acc_ref[...] = jnp.zeros(acc_ref.shape, F32)

    def attend(k_pages, v_pages, mask_t, feature_major):
        qbd = qbd_ref[...]
        qk = [(_dot(qbd, kp) if feature_major else _dot_nt(qbd, kp)) for kp in k_pages]
        sel = jnp.concatenate([mask_t] * N_HEADS, axis=0) > 0.0
        s = jnp.where(sel, qk[0] if len(qk) == 1 else jnp.concatenate(qk, axis=1), NEG)
        m_old = m_ref[...]
        m_new = jnp.maximum(m_old, jnp.max(s, axis=-1, keepdims=True))
        alpha = jnp.exp(m_old - m_new)
        p = jnp.where(sel, jnp.exp(s - m_new), 0.0)
        l_ref[...] = alpha * l_ref[...] + jnp.sum(p, axis=-1, keepdims=True)
        pb = p.astype(BF16)
        pv = None
        for i, vp in enumerate(v_pages):
            p_i = pb[:, i * PAGE_SIZE:(i + 1) * PAGE_SIZE]
            term = _dot_nt(p_i, vp) if feature_major else _dot(p_i, vp)
            pv = term if pv is None else pv + term
        acc_ref[...] = alpha * acc_ref[...] + pv
        m_ref[...] = m_new

    attend([r[0].astype(BF16) for r in k_refs], [r[0].astype(BF16) for r in v_refs], mpast_ref[0], True)

    @pl.when(j == pl.num_programs(1) - 1)
    def _():
        zeros = jnp.zeros((PAGE_SIZE - t, da), F32)
        attend([jnp.concatenate([kn_ref[0], zeros], axis=0).astype(BF16)],
               [jnp.concatenate([vn_ref[0], zeros], axis=0).astype(BF16)], mnew_ref[0], False)
        out = acc_ref[...] / l_ref[...]
        o_ref[0] = jnp.concatenate(
            [out[h * t:(h + 1) * t, h * HEAD_DIM:(h + 1) * HEAD_DIM] for h in range(N_HEADS)], axis=1).astype(BF16)


def _attn_sample(page_table, q, k_new, v_new, mask_past, mask_new, cache_k, cache_v, *, pages):
    n, n_pages = page_table.shape
    t, da = q.shape[1], q.shape[2]
    steps = n_pages // pages
    ht = N_HEADS * t

    def page_spec(i):
        return pl.BlockSpec((1, da, PAGE_SIZE), lambda b, j, pt: (pt[b, j * pages + i], 0, 0))

    per_seq = lambda shape: pl.BlockSpec((1,) + shape, lambda b, j, pt: (b, 0, 0))
    grid_spec = pltpu.PrefetchScalarGridSpec(
        num_scalar_prefetch=1,
        grid=(n, steps),
        in_specs=[per_seq((t, da)), per_seq((t, da)), per_seq((t, da)),
                  pl.BlockSpec((1, t, pages * PAGE_SIZE), lambda b, j, pt: (b, 0, j)), per_seq((t, LANES))]
                 + [page_spec(i) for i in range(pages)] + [page_spec(i) for i in range(pages)],
        out_specs=per_seq((t, da)),
        scratch_shapes=[pltpu.VMEM((ht, da), BF16), pltpu.VMEM((ht, 1), F32), pltpu.VMEM((ht, 1), F32),
                        pltpu.VMEM((ht, da), F32)],
    )
    return pl.pallas_call(
        functools.partial(_attn_sample_kernel, pages=pages),
        grid_spec=grid_spec,
        out_shape=jax.ShapeDtypeStruct((n, t, da), BF16),
        compiler_params=_cparams(2),
        name="attn_sample",
    )(page_table, q, k_new, v_new, mask_past, mask_new, *([cache_k] * pages), *([cache_v] * pages))


def _merge_kernel(x_ref, h_ref, c_ref, a_ref, m_ref, wg_ref, wc_ref, wa_ref, wm_ref, wo_ref, o_ref):
    d = x_ref.shape[1]
    gates = jax.nn.sigmoid(_dot(h_ref[...], wg_ref[...]))
    mix = (gates[:, :d] * _dot(c_ref[...], wc_ref[...])
           + gates[:, d:2 * d] * _dot(a_ref[...], wa_ref[...])
           + gates[:, 2 * d:] * _dot(m_ref[...], wm_ref[...]))
    o_ref[...] = x_ref[...] + _dot(mix.astype(BF16), wo_ref[...])


def _merge(x, h, c, a, mo, wg, wc, wa, wm, wo, *, tm):
    m, d = x.shape
    row = lambda n: pl.BlockSpec((tm, n), lambda i: (i, 0))
    return pl.pallas_call(
        _merge_kernel,
        grid=(m // tm,),
        in_specs=[row(d), row(d), row(c.shape[1]), row(a.shape[1]), row(mo.shape[1]),
                  _const_spec(wg.shape), _const_spec(wc.shape), _const_spec(wa.shape), _const_spec(wm.shape),
                  _const_spec(wo.shape)],
        out_specs=row(d),
        out_shape=jax.ShapeDtypeStruct((m, d), F32),
        compiler_params=_cparams(1),
        name="merge",
    )(x, h, c, a, mo, wg, wc, wa, wm, wo)


def _ffn_kernel(x_ref, g_ref, wi_ref, wo_ref, gf_ref, o_ref):
    x = x_ref[...]
    dff = wo_ref.shape[0]
    hn = _rmsnorm(x, g_ref[...]).astype(BF16)
    a = _dot(hn, wi_ref[:, :dff])
    b = _dot(hn, wi_ref[:, dff:])
    y = x + _dot((a * jax.nn.sigmoid(a) * b).astype(BF16), wo_ref[...])
    o_ref[...] = _rmsnorm(y, gf_ref[...])


def _ffn(x, g, w_in, w_out, g_final, *, tm):
    m, d = x.shape
    return pl.pallas_call(
        _ffn_kernel,
        grid=(m // tm,),
        in_specs=[pl.BlockSpec((tm, d), lambda i: (i, 0)), _const_spec((1, d)), _const_spec(w_in.shape),
                  _const_spec(w_out.shape), _const_spec((1, d))],
        out_specs=pl.BlockSpec((tm, d), lambda i: (i, 0)),
        out_shape=jax.ShapeDtypeStruct((m, d), F32),
        compiler_params=_cparams(1),
        name="ffn",
    )(x, g, w_in, w_out, g_final)


def kernel(x_prompt, x_sample, mem_prompt, cache_conv, cache_k, cache_v, cache_idx_k, cache_mem_k, cache_mem_v,
           page_table, g_mix, w_in, w_conv_dw, b_conv_dw, g_conv_ln, b_conv_ln, w_conv_out, w_att_out, g_mem,
           w_mem_kv, w_mem_out, w_out, g_ffn, w_ffn_in, w_ffn_out, g_final):
    depth = g_mix.shape[0]
    assert depth == 1
    bsz, seq, d = x_prompt.shape
    n_dec, t_dec, _ = x_sample.shape
    n_mem = mem_prompt.shape[1]
    d_conv = w_conv_dw.shape[2]
    d_att = N_HEADS * HEAD_DIM
    d_idx = N_IDX_HEADS * IDX_DIM
    d_mem = MEM_HEADS * MEM_HEAD_DIM
    n_phys = cache_k.shape[1]
    past = page_table.shape[1] * PAGE_SIZE
    l = 0

    splits = [2 * d_conv, d_att, d_att, d_att, d_idx, IDX_DIM, N_IDX_HEADS, d_mem, 3 * d]
    offs = np.concatenate([[0], np.cumsum(splits)])
    w = w_in[l]
    col = lambda i: w[:, offs[i]:offs[i + 1]]
    pad_cols = lambda a, n: jnp.pad(a, ((0, 0), (0, n - a.shape[1])))
    glu, wq, wk, wv, wqi, wki, wwi, wqm, wgates = (col(i) for i in range(9))
    shared = [glu, wqm, pad_cols(wki, LANES)]
    w_prompt = jnp.concatenate(shared, axis=1).astype(BF16)
    w_prompt_t = jnp.concatenate([wk, wv, wki, pad_cols(wwi, WI_ROWS), wqi, wq], axis=1).T.astype(BF16)
    w_sample = jnp.concatenate(shared + [wq, wqi, pad_cols(wwi, LANES), wk, wv], axis=1).astype(BF16)
    w_gates = wgates.astype(BF16)
    bf = lambda a: a[l].astype(BF16)
    row2 = lambda a: a.reshape(1, -1)
    wdw, bdw, gln, bln = w_conv_dw[l], row2(b_conv_dw[l]), row2(g_conv_ln[l]), row2(b_conv_ln[l])
    dims = dict(d_conv=d_conv, d_att=d_att, d_idx=d_idx, d_mem=d_mem)
    merge = functools.partial(_merge, wg=w_gates, wc=bf(w_conv_out), wa=bf(w_att_out), wm=bf(w_mem_out),
                              wo=bf(w_out))
    ffn = functools.partial(_ffn, g=row2(g_ffn[l]), w_in=bf(w_ffn_in), w_out=bf(w_ffn_out),
                            g_final=row2(g_final))

    mp = bsz * seq
    xp = x_prompt.reshape(mp, d)
    u, qm, h, kib, kb, kt, vt, vtb, kit, qitb, wit, qtb = _inproj_prompt(x_prompt, row2(g_mix[l]), w_prompt,
                                                                         w_prompt_t, tm=512, **dims)
    per_seq = lambda a: a.reshape(bsz, seq, a.shape[-1])
    u3 = per_seq(u)
    c = _conv_prompt(u3, wdw, bdw, gln, bln)
    n_sel = min(TOPK_MAX, seq // 4)
    tq = 256
    mask = _sel_prompt(per_seq(kib), qitb, wit, tq=tq, n_sel=n_sel)
    oa = _attn_prompt(qtb, per_seq(kb), vtb, mask, tq=tq)
    mk, mv = _memkv(mem_prompt.reshape(bsz * n_mem, d), row2(g_mem[l]), bf(w_mem_kv))
    om = _memattn(per_seq(qm), mk.reshape(bsz, n_mem, d_mem), mv.reshape(bsz, n_mem, d_mem), tq=512)
    x1 = merge(xp, h, c.reshape(mp, d_conv), oa.reshape(mp, d_att), om.reshape(mp, d_mem), tm=512)
    y_prompt = ffn(x1, tm=256).reshape(bsz, seq, d)
    conv_state_prompt = u3[:, seq - (CONV_WIDTH - 1):][None]
    k_prompt = kt.reshape(bsz, N_HEADS, HEAD_DIM, seq).transpose(0, 3, 1, 2)[None]
    v_prompt = vt.reshape(bsz, N_HEADS, HEAD_DIM, seq).transpose(0, 3, 1, 2)[None]
    idx_k_prompt = kit.transpose(0, 2, 1)[None]
    mem_k_prompt = mk.reshape(1, bsz, n_mem, MEM_HEADS, MEM_HEAD_DIM)
    mem_v_prompt = mv.reshape(1, bsz, n_mem, MEM_HEADS, MEM_HEAD_DIM)

    ms = n_dec * t_dec
    xs = x_sample.reshape(ms, d)
    u, q, qm, h, qi, wi, k, v, ki = _inproj_sample(xs, row2(g_mix[l]), w_sample, **dims)
    per_seq = lambda a: a.reshape(n_dec, t_dec, a.shape[-1])
    u_ext = jnp.concatenate([cache_conv[l], per_seq(u)], axis=1)
    c = _conv_sample(u_ext, wdw, bdw, gln, bln)
    n_sel = min(TOPK_MAX, (past + t_dec) // 4)
    qi_hm = per_seq(qi).reshape(n_dec, t_dec, N_IDX_HEADS, IDX_DIM).transpose(0, 2, 1, 3)
    qi_hm = qi_hm.reshape(n_dec, N_IDX_HEADS * t_dec, IDX_DIM)
    wi_hm = per_seq(wi).transpose(0, 2, 1).reshape(n_dec, N_IDX_HEADS * t_dec, 1)
    page_major = lambda a: jnp.moveaxis(a[l], 1, -1).reshape(n_phys, -1, PAGE_SIZE)
    key_past, key_new = _score_sample(page_table, qi_hm, wi_hm, per_seq(ki), page_major(cache_idx_k))
    mask_past, mask_new = _sel_sample(key_past.reshape(ms, past), key_new.reshape(ms, LANES), n_sel=n_sel, t=t_dec)
    oa = _attn_sample(page_table, per_seq(q), per_seq(k), per_seq(v), mask_past.reshape(n_dec, t_dec, past),
                      mask_new.reshape(n_dec, t_dec, LANES), page_major(cache_k), page_major(cache_v), pages=32)
    om = _memattn_sample(per_seq(qm), cache_mem_k[l].reshape(n_dec, n_mem * MEM_HEADS, MEM_HEAD_DIM),
                         cache_mem_v[l].reshape(n_dec, n_mem * MEM_HEADS, MEM_HEAD_DIM))
    x1 = merge(xs, h, c.reshape(ms, d_conv), oa.reshape(ms, d_att), om.reshape(ms, d_mem), tm=ms)
    y_sample = ffn(x1, tm=ms).reshape(n_dec, t_dec, d)
    conv_state_sample = u_ext[:, t_dec:][None]
    k_sample = k.reshape(1, n_dec, t_dec, N_HEADS, HEAD_DIM)
    v_sample = v.reshape(1, n_dec, t_dec, N_HEADS, HEAD_DIM)
    idx_k_sample = ki.reshape(1, n_dec, t_dec, IDX_DIM)

    return (y_prompt, y_sample, conv_state_prompt, k_prompt, v_prompt, idx_k_prompt, mem_k_prompt, mem_v_prompt,
            conv_state_sample, k_sample, v_sample, idx_k_sample)
```

```python
import functools

import jax
import jax.numpy as jnp
import numpy as np
from jax import lax
from jax.experimental import pallas as pl
from jax.experimental.pallas import tpu as pltpu

EPS = 1e-6
NEG = -1e30
CONV_WIDTH = 31
N_HEADS = 8
HEAD_DIM = 64
N_IDX_HEADS = 8
IDX_DIM = 64
MEM_HEADS = 4
MEM_HEAD_DIM = 128
TOPK_MAX = 256
PAGE_SIZE = 128
LANES = 128
SUBLANES = 8
VMEM_LIMIT = 56 * 1024 * 1024
INT_MIN = -(2 ** 31)
LOG2E = 1.4426950408889634
TILES_PER_STEP = 2

BF16 = jnp.bfloat16
F32 = jnp.float32
I32 = jnp.int32


def _cparams(n_axes):
    return pltpu.CompilerParams(dimension_semantics=("arbitrary",) * n_axes, vmem_limit_bytes=VMEM_LIMIT)


def _const_spec(shape):
    zeros = (0,) * len(shape)
    return pl.BlockSpec(shape, lambda *_: zeros)


def _rmsnorm(x, g):
    return x * lax.rsqrt(jnp.mean(x * x, axis=-1, keepdims=True) + EPS) * g


def _dot(a, b):
    return jnp.dot(a, b, preferred_element_type=F32)


def _dot_nt(a, b):
    return lax.dot_general(a, b, (((1,), (1,)), ((), ())), preferred_element_type=F32)


WI_ROWS = 16


def _glu_qm(h, w_ref, u_ref, qm_ref, d_conv, d_mem):
    o = 0
    glu = _dot(h, w_ref[:, o:o + 2 * d_conv]); o += 2 * d_conv
    u_ref[...] = glu[:, :d_conv] * jax.nn.sigmoid(glu[:, d_conv:])
    qm_ref[...] = _dot(h, w_ref[:, o:o + d_mem]).astype(BF16); o += d_mem
    return o


def _inproj_prompt_kernel(x_ref, g_ref, w_ref, wt_ref, u_ref, qm_ref, h_ref, kib_ref, kb_ref,
                          kt_ref, vt_ref, vtb_ref, kit_ref, qit_ref, wit_ref, qt_ref, *, d_conv, d_att, d_idx, d_mem):
    h = _rmsnorm(x_ref[0], g_ref[...]).astype(BF16)
    h_ref[...] = h
    o = _glu_qm(h, w_ref, u_ref, qm_ref, d_conv, d_mem)
    kib_ref[...] = _dot(h, w_ref[:, o:o + LANES])[:, :IDX_DIM].astype(BF16)
    r = 0
    kt = _dot_nt(wt_ref[r:r + d_att, :], h); r += d_att
    kt_ref[0] = kt
    kb_ref[...] = kt.T.astype(BF16)
    vt = _dot_nt(wt_ref[r:r + d_att, :], h); r += d_att
    vt_ref[0] = vt
    vtb_ref[0] = vt.astype(BF16)
    kit_ref[0] = _dot_nt(wt_ref[r:r + IDX_DIM, :], h); r += IDX_DIM
    wit_ref[0] = _dot_nt(wt_ref[r:r + WI_ROWS, :], h)[:N_IDX_HEADS] * (N_IDX_HEADS ** -0.5); r += WI_ROWS
    qit_ref[0] = (_dot_nt(wt_ref[r:r + d_idx, :], h) * (IDX_DIM ** -0.5)).astype(BF16); r += d_idx
    qt_ref[0] = (_dot_nt(wt_ref[r:r + d_att, :], h) * (HEAD_DIM ** -0.5)).astype(BF16)


def _inproj_prompt(x, g, w, wt, *, tm, d_conv, d_att, d_idx, d_mem):
    bsz, s, d = x.shape
    nt = s // tm
    row = lambda n: pl.BlockSpec((tm, n), lambda b, i: (b * nt + i, 0))
    col = lambda n: pl.BlockSpec((1, n, tm), lambda b, i: (b, 0, i))
    outs = [(d_conv, F32), (d_mem, BF16), (d, BF16), (IDX_DIM, BF16), (d_att, BF16)]
    outs_t = [(d_att, F32), (d_att, F32), (d_att, BF16), (IDX_DIM, F32), (d_idx, BF16), (N_IDX_HEADS, F32),
              (d_att, BF16)]
    return pl.pallas_call(
        functools.partial(_inproj_prompt_kernel, d_conv=d_conv, d_att=d_att, d_idx=d_idx, d_mem=d_mem),
        grid=(bsz, nt),
        in_specs=[pl.BlockSpec((1, tm, d), lambda b, i: (b, i, 0)), _const_spec((1, d)), _const_spec(w.shape),
                  _const_spec(wt.shape)],
        out_specs=[row(n) for n, _ in outs] + [col(n) for n, _ in outs_t],
        out_shape=[jax.ShapeDtypeStruct((bsz * s, n), dt) for n, dt in outs]
                  + [jax.ShapeDtypeStruct((bsz, n, s), dt) for n, dt in outs_t],
        compiler_params=_cparams(2),
        name="inproj_prompt",
    )(x, g, w, wt)


def _inproj_sample_kernel(x_ref, g_ref, w_ref, u_ref, q_ref, qm_ref, h_ref, qi_ref, wi_ref, k_ref, v_ref, ki_ref,
                          *, d_conv, d_att, d_idx, d_mem):
    h = _rmsnorm(x_ref[...], g_ref[...]).astype(BF16)
    h_ref[...] = h
    o = _glu_qm(h, w_ref, u_ref, qm_ref, d_conv, d_mem)
    ki_ref[...] = _dot(h, w_ref[:, o:o + LANES])[:, :IDX_DIM]; o += LANES
    q_ref[...] = (_dot(h, w_ref[:, o:o + d_att]) * (HEAD_DIM ** -0.5)).astype(BF16); o += d_att
    qi_ref[...] = (_dot(h, w_ref[:, o:o + d_idx]) * (IDX_DIM ** -0.5)).astype(BF16); o += d_idx
    wi_ref[...] = _dot(h, w_ref[:, o:o + LANES])[:, :N_IDX_HEADS] * (N_IDX_HEADS ** -0.5); o += LANES
    k_ref[...] = _dot(h, w_ref[:, o:o + d_att]); o += d_att
    v_ref[...] = _dot(h, w_ref[:, o:o + d_att])


def _inproj_sample(x, g, w, *, d_conv, d_att, d_idx, d_mem):
    m, d = x.shape
    outs = [(d_conv, F32), (d_att, BF16), (d_mem, BF16), (d, BF16), (d_idx, BF16), (N_IDX_HEADS, F32),
            (d_att, F32), (d_att, F32), (IDX_DIM, F32)]
    return pl.pallas_call(
        functools.partial(_inproj_sample_kernel, d_conv=d_conv, d_att=d_att, d_idx=d_idx, d_mem=d_mem),
        grid=(1,),
        in_specs=[_const_spec((m, d)), _const_spec((1, d)), _const_spec(w.shape)],
        out_specs=[_const_spec((m, n)) for n, _ in outs],
        out_shape=[jax.ShapeDtypeStruct((m, n), dt) for n, dt in outs],
        compiler_params=_cparams(1),
        name="inproj_sample",
    )(x, g, w)


def _conv_taps(ext_ref, start, rows, w_ref, by_phase):
    if not by_phase:
        acc = w_ref[0:1, :] * ext_ref[pl.ds(start, rows), :]
        for j in range(1, CONV_WIDTH):
            acc = acc + w_ref[j:j + 1, :] * ext_ref[pl.ds(start + j, rows), :]
        return acc
    acc = None
    for r in range(SUBLANES):
        y = None
        for j in range(CONV_WIDTH):
            if (start + j) % SUBLANES == r:
                term = w_ref[j:j + 1, :] * ext_ref[pl.ds(start + j - r, rows + (SUBLANES if r else 0)), :]
                y = term if y is None else y + term
        if y is not None:
            y = y[r:r + rows] if r else y
            acc = y if acc is None else acc + y
    return acc


def _conv_rows(ext_ref, start, rows, w_ref, b_ref, g_ref, bl_ref, by_phase=False):
    c = _conv_taps(ext_ref, start, rows, w_ref, by_phase) + b_ref[...]
    mu = jnp.mean(c, axis=-1, keepdims=True)
    xc = c - mu
    y = xc * lax.rsqrt(jnp.mean(xc * xc, axis=-1, keepdims=True) + EPS) * g_ref[...] + bl_ref[...]
    return (y * jax.nn.sigmoid(y)).astype(BF16)


def _conv_prompt_kernel(prev_ref, cur_ref, w_ref, b_ref, g_ref, bl_ref, o_ref, ext_ref, *, halo, chunk):
    t = cur_ref.shape[1]

    @pl.when(pl.program_id(1) == 0)
    def _():
        ext_ref[0:halo, :] = jnp.zeros((halo, ext_ref.shape[1]), F32)

    @pl.when(pl.program_id(1) > 0)
    def _():
        ext_ref[0:halo, :] = prev_ref[0]

    ext_ref[halo:halo + t, :] = cur_ref[0]
    ext_ref[halo + t:, :] = jnp.zeros((SUBLANES, ext_ref.shape[1]), F32)
    first = halo - (CONV_WIDTH - 1)
    for c in range(t // chunk):
        o_ref[0, c * chunk:(c + 1) * chunk, :] = _conv_rows(ext_ref, first + c * chunk, chunk, w_ref, b_ref, g_ref,
                                                            bl_ref, by_phase=True)


def _conv_prompt(u, w, b, g, bl, *, t=512, halo=32, chunk=128):
    bsz, s, dc = u.shape
    assert s % t == 0 and t % halo == 0 and halo >= CONV_WIDTH - 1 and t % chunk == 0
    r = t // halo
    return pl.pallas_call(
        functools.partial(_conv_prompt_kernel, halo=halo, chunk=chunk),
        grid=(bsz, s // t),
        in_specs=[pl.BlockSpec((1, halo, dc), lambda bi, i: (bi, jnp.maximum(i * r - 1, 0), 0)),
                  pl.BlockSpec((1, t, dc), lambda bi, i: (bi, i, 0)),
                  _const_spec(w.shape), _const_spec((1, dc)), _const_spec((1, dc)), _const_spec((1, dc))],
        out_specs=pl.BlockSpec((1, t, dc), lambda bi, i: (bi, i, 0)),
        out_shape=jax.ShapeDtypeStruct((bsz, s, dc), BF16),
        scratch_shapes=[pltpu.VMEM((halo + t + SUBLANES, dc), F32)],
        compiler_params=_cparams(2),
        name="conv_prompt",
    )(u, u, w, b, g, bl)


def _conv_sample_kernel(ext_ref, w_ref, b_ref, g_ref, bl_ref, o_ref):
    t = o_ref.shape[1]
    for s in range(o_ref.shape[0]):
        o_ref[s] = _conv_rows(ext_ref.at[s], 0, t, w_ref, b_ref, g_ref, bl_ref)


def _conv_sample(u_ext, w, b, g, bl, *, seqs=8):
    n, te, dc = u_ext.shape
    t = te - (CONV_WIDTH - 1)
    return pl.pallas_call(
        _conv_sample_kernel,
        grid=(n // seqs,),
        in_specs=[pl.BlockSpec((seqs, te, dc), lambda i: (i, 0, 0)),
                  _const_spec(w.shape), _const_spec((1, dc)), _const_spec((1, dc)), _const_spec((1, dc))],
        out_specs=pl.BlockSpec((seqs, t, dc), lambda i: (i, 0, 0)),
        out_shape=jax.ShapeDtypeStruct((n, t, dc), BF16),
        compiler_params=_cparams(1),
        name="conv_sample",
    )(u_ext, w, b, g, bl)


def _memkv_kernel(x_ref, g_ref, w_ref, k_ref, v_ref):
    h = _rmsnorm(x_ref[...], g_ref[...]).astype(BF16)
    kv = _dot(h, w_ref[...])
    d = k_ref.shape[1]
    k_ref[...] = kv[:, :d]
    v_ref[...] = kv[:, d:]


def _memkv(x, g, w, *, tm=256):
    m, d = x.shape
    dm = w.shape[1] // 2
    return pl.pallas_call(
        _memkv_kernel,
        grid=(m // tm,),
        in_specs=[pl.BlockSpec((tm, d), lambda i: (i, 0)), _const_spec((1, d)), _const_spec(w.shape)],
        out_specs=[pl.BlockSpec((tm, dm), lambda i: (i, 0))] * 2,
        out_shape=[jax.ShapeDtypeStruct((m, dm), F32)] * 2,
        compiler_params=_cparams(1),
        name="memkv",
    )(x, g, w)


def _memattn_kernel(q_ref, k_ref, v_ref, o_ref):
    for h in range(MEM_HEADS):
        sl = slice(h * MEM_HEAD_DIM, (h + 1) * MEM_HEAD_DIM)
        s = _dot_nt(q_ref[0, :, sl], k_ref[0, :, sl].astype(BF16)) * (MEM_HEAD_DIM ** -0.5)
        e = jnp.exp(s - jnp.max(s, axis=-1, keepdims=True))
        p = e / jnp.sum(e, axis=-1, keepdims=True)
        o_ref[0, :, sl] = _dot(p.astype(BF16), v_ref[0, :, sl].astype(BF16)).astype(BF16)


def _memattn(q, k, v, *, tq):
    n, t, dm = q.shape
    nm = k.shape[1]
    return pl.pallas_call(
        _memattn_kernel,
        grid=(n, t // tq),
        in_specs=[pl.BlockSpec((1, tq, dm), lambda i, j: (i, j, 0)),
                  pl.BlockSpec((1, nm, dm), lambda i, j: (i, 0, 0)),
                  pl.BlockSpec((1, nm, dm), lambda i, j: (i, 0, 0))],
        out_specs=pl.BlockSpec((1, tq, dm), lambda i, j: (i, j, 0)),
        out_shape=jax.ShapeDtypeStruct((n, t, dm), BF16),
        compiler_params=_cparams(2),
        name="memattn",
    )(q, k, v)


def _memattn_sample_kernel(q_ref, k_ref, v_ref, o_ref):
    n_mem = k_ref.shape[1] // MEM_HEADS
    for g in range(q_ref.shape[0]):
        for h in range(MEM_HEADS):
            sl = slice(h * MEM_HEAD_DIM, (h + 1) * MEM_HEAD_DIM)
            head_rows = pl.ds(h, n_mem, stride=MEM_HEADS)
            s = _dot_nt(q_ref[g, :, sl], k_ref[g, head_rows, :].astype(BF16)) * (MEM_HEAD_DIM ** -0.5)
            e = jnp.exp(s - jnp.max(s, axis=-1, keepdims=True))
            p = e / jnp.sum(e, axis=-1, keepdims=True)
            o_ref[g, :, sl] = _dot(p.astype(BF16), v_ref[g, head_rows, :].astype(BF16)).astype(BF16)


def _memattn_sample(q, k, v, *, seqs=4):
    n, t, dm = q.shape
    rows = k.shape[1]
    return pl.pallas_call(
        _memattn_sample_kernel,
        grid=(n // seqs,),
        in_specs=[pl.BlockSpec((seqs, t, dm), lambda i: (i, 0, 0)),
                  pl.BlockSpec((seqs, rows, MEM_HEAD_DIM), lambda i: (i, 0, 0)),
                  pl.BlockSpec((seqs, rows, MEM_HEAD_DIM), lambda i: (i, 0, 0))],
        out_specs=pl.BlockSpec((seqs, t, dm), lambda i: (i, 0, 0)),
        out_shape=jax.ShapeDtypeStruct((n, t, dm), BF16),
        compiler_params=_cparams(1),
        name="memattn_sample",
    )(q, k, v)


def _ordinal_to_float(o):
    return pltpu.bitcast(jnp.where(o >= 0, o, (-o) | INT_MIN), F32)


def _float_to_ordinal(x):
    b = pltpu.bitcast(x, I32)
    return jnp.where(b < 0, -(b & 0x7FFFFFFF), b)


ORD_NEG_INF = -0x7F800000
ORD_MIN_NORMAL = 0x00800000
FIRST_CHECK = 18
CHECK_EVERY = 2
VALUE_PROBES = 48
MAX_PROBES = VALUE_PROBES + 34


def _select_mask(score_ref, key_axis, n_tiles, n_sel, outside, idx_bits, valid_fn, emit_fn, extreme_parts=None):
    _, ta, tb = score_ref.shape
    tile = (ta, tb)
    unit = LANES if key_axis == 1 else 4 * SUBLANES
    tk = tile[key_axis]
    fold = tk // unit
    shp = (ta, LANES) if key_axis == 1 else (unit, tb)
    k_f = jnp.float32(n_sel)
    out_f = jnp.float32(1.0) * outside

    def key_fold(x, op=jnp.add):
        if key_axis == 0:
            x = x.reshape(fold, unit, tb)
            acc = x[0]
            for i in range(1, fold):
                acc = op(acc, x[i])
            return acc
        acc = x[:, :LANES]
        for i in range(1, fold):
            acc = op(acc, x[:, i * LANES:(i + 1) * LANES])
        return acc

    def rep(x):
        return x if fold == 1 else jnp.concatenate([x] * fold, axis=key_axis)

    def row_total(part):
        return jnp.broadcast_to(jnp.sum(part, axis=key_axis, keepdims=True), shp)

    def key_index(j):
        return j * tk + lax.broadcasted_iota(I32, tile, key_axis)

    def count_ge(thr):
        thr_t = rep(thr)

        def body(j, part):
            return part + key_fold(jnp.where(score_ref[j] >= thr_t, 1.0, 0.0))

        part = lax.fori_loop(0, n_tiles, body, jnp.zeros(shp, F32))
        return row_total(part) + jnp.where(thr <= NEG, out_f, 0.0)

    def extremes(j, carry):
        mx, mn = carry
        sc = score_ref[j]
        return (jnp.maximum(mx, key_fold(sc, jnp.maximum)),
                jnp.minimum(mn, key_fold(jnp.where(sc > NEG, sc, jnp.inf), jnp.minimum)))

    if extreme_parts is None:
        extreme_parts = lax.fori_loop(0, n_tiles, extremes,
                                      (jnp.full(shp, -jnp.inf, F32), jnp.full(shp, jnp.inf, F32)))
    mx, mn = extreme_parts
    mx = jnp.broadcast_to(jnp.max(mx, axis=key_axis, keepdims=True), shp)
    mx = jnp.maximum(mx, jnp.where(out_f > 0.0, NEG, -jnp.inf))
    mn_o = _float_to_ordinal(jnp.broadcast_to(jnp.min(mn, axis=key_axis, keepdims=True), shp))

    def is_settled(lo_o, hi_o, c_lo):
        return ((c_lo == k_f) | (hi_o == lo_o + 1) | ((lo_o >= 0) & (hi_o <= ORD_MIN_NORMAL))
                | ((hi_o <= 0) & (lo_o >= -ORD_MIN_NORMAL)))

    def probe(it, state):
        lo_o, hi_o, c_lo = state
        settled = is_settled(lo_o, hi_o, c_lo)
        omid = (lo_o >> 1) + (hi_o >> 1) + (lo_o & hi_o & 1)
        vmid = _float_to_ordinal(0.5 * _ordinal_to_float(lo_o) + 0.5 * _ordinal_to_float(hi_o))
        one_sign = (lo_o >= 0) | (hi_o <= 0)
        cand = jnp.where(one_sign, vmid, omid)
        cand = jnp.where(it == 0, mn_o, jnp.where(it == 1, 0, jnp.where(it == 2, ORD_MIN_NORMAL, cand)))
        cand = jnp.where(it < VALUE_PROBES, cand, omid)
        mid = jnp.where((cand > lo_o) & (cand < hi_o), cand, omid)
        cnt = count_ge(_ordinal_to_float(mid))
        up = (cnt >= k_f) & ~settled
        down = (cnt < k_f) & ~settled
        return jnp.where(up, mid, lo_o), jnp.where(down, mid, hi_o), jnp.where(up, cnt, c_lo)

    def unsettled(state):
        return jnp.max(jnp.where(is_settled(*state), 0.0, 1.0)).astype(I32)

    def probe_block(carry):
        it, state, _ = carry
        state = lax.fori_loop(it, it + CHECK_EVERY, probe, state)
        return it + CHECK_EVERY, state, unsettled(state)

    total = jnp.float32(1.0) * (n_tiles * tk) + out_f
    state = (jnp.full(shp, ORD_NEG_INF, I32), _float_to_ordinal(mx) + 1, jnp.broadcast_to(total, shp))
    state = lax.fori_loop(0, FIRST_CHECK, probe, state)
    _, (lo_o, _, cnt_ge), _ = lax.while_loop(lambda c: (c[2] > 0) & (c[0] < MAX_PROBES), probe_block,
                                             (jnp.int32(FIRST_CHECK), state, unsettled(state)))
    thr = _ordinal_to_float(lo_o)
    thr_t = rep(thr)

    surplus = jnp.max(jnp.where(cnt_ge > k_f, 1.0, 0.0)) > 0.0

    def tie_search():
        def count_gt_body(j, part):
            return part + key_fold(jnp.where(score_ref[j] > thr_t, 1.0, 0.0))

        cnt_gt = row_total(lax.fori_loop(0, n_tiles, count_gt_body, jnp.zeros(shp, F32)))
        cnt_gt = cnt_gt + jnp.where(thr < NEG, out_f, 0.0)
        in_tiles = n_tiles * tk

        def idx_step(it, v):
            bit = lax.shift_left(jnp.int32(1), idx_bits - 1 - it)
            cand = v | bit
            cand_t = rep(cand)

            def body(j, part):
                hit = jnp.where(score_ref[j] == thr_t, jnp.where(key_index(j) < cand_t, 1.0, 0.0), 0.0)
                return part + key_fold(hit)

            ties_below = row_total(lax.fori_loop(0, n_tiles, body, jnp.zeros(shp, F32)))
            out_below = jnp.clip(cand - in_tiles, 0, outside).astype(F32)
            ties_below = ties_below + jnp.where(thr == NEG, out_below, 0.0)
            return jnp.where(cnt_gt + ties_below < k_f, cand, v)

        return lax.fori_loop(0, idx_bits, idx_step, jnp.zeros(shp, I32))

    last = n_tiles - 1

    def emit_with_ties():
        jmax_t = rep(tie_search())

        def emit(j, _):
            sc = score_ref[j]
            tied = jnp.where(sc == thr_t, jnp.where(key_index(j) <= jmax_t, 1.0, 0.0), 0.0)
            emit_fn(j, jnp.where(valid_fn(j, tile), jnp.where(sc > thr_t, 1.0, tied), 0.0))
            return 0

        lax.fori_loop(0, n_tiles, emit, 0)

    def emit_plain():
        def emit(j, _):
            emit_fn(j, jnp.where(score_ref[j] >= thr_t, 1.0, 0.0))
            return 0

        lax.fori_loop(0, last, emit, 0)
        emit_fn(last, jnp.where(valid_fn(last, tile), jnp.where(score_ref[last] >= thr_t, 1.0, 0.0), 0.0))

    lax.cond(surplus, emit_with_ties, emit_plain)


def _sel_prompt_kernel(kib_ref, qit_ref, wit_ref, mask_ref, sc_ref, *, n_sel):
    tq = qit_ref.shape[2]
    s = kib_ref.shape[1]
    n_all = s // tq
    qb = pl.program_id(1)
    n_tiles = qb + 1
    qpos = qb * tq + lax.broadcasted_iota(I32, (tq, tq), 1)

    def score_tile(j):
        ki_t = kib_ref[0, pl.ds(pl.multiple_of(j * tq, tq), tq), :]
        acc = jnp.zeros((tq, tq), F32)
        for h in range(N_IDX_HEADS):
            d = _dot(ki_t, qit_ref[0, h * IDX_DIM:(h + 1) * IDX_DIM, :])
            acc = acc + wit_ref[0, h:h + 1, :] * jnp.maximum(d, 0.0)
        return acc

    part = 4 * SUBLANES

    def extremes(sc, mx, mn):
        sc = sc.reshape(tq // part, part, tq)
        return (jnp.maximum(mx, jnp.max(sc, axis=0)),
                jnp.minimum(mn, jnp.min(jnp.where(sc > NEG, sc, jnp.inf), axis=0)))

    def below(j, carry):
        sc = score_tile(j)
        sc_ref[j] = sc
        return extremes(sc, *carry)

    def below_pair(jj, carry):
        return below(2 * jj + 1, below(2 * jj, carry))

    pairs = lax.shift_right_logical(qb, 1)
    carry = lax.fori_loop(0, pairs, below_pair,
                          (jnp.full((part, tq), -jnp.inf, F32), jnp.full((part, tq), jnp.inf, F32)))
    mx, mn = lax.fori_loop(2 * pairs, qb, below, carry)
    kpos = qb * tq + lax.broadcasted_iota(I32, (tq, tq), 0)
    diag = jnp.where(kpos <= qpos, score_tile(qb), NEG)
    sc_ref[qb] = diag
    mx, mn = extremes(diag, mx, mn)

    def valid(j, shape):
        return j * tq + lax.broadcasted_iota(I32, shape, 0) <= qpos

    def emit(j, sel):
        mask_ref[0, 0, j] = sel.astype(BF16)

    idx_bits = max(1, int(s - 1).bit_length())
    _select_mask(sc_ref, 0, n_tiles, n_sel, (n_all - n_tiles) * tq, idx_bits, valid, emit, (mx, mn))

    def clear(j, _):
        mask_ref[0, 0, j] = jnp.zeros((tq, tq), BF16)
        return 0

    lax.fori_loop(n_tiles, n_all, clear, 0)


def _sel_prompt(kib, qitb, wit, *, tq, n_sel):
    bsz, s, _ = kib.shape
    nq = s // tq
    return pl.pallas_call(
        functools.partial(_sel_prompt_kernel, n_sel=n_sel),
        grid=(bsz, nq),
        in_specs=[pl.BlockSpec((1, s, kib.shape[2]), lambda b, i: (b, 0, 0)),
                  pl.BlockSpec((1, qitb.shape[1], tq), lambda b, i: (b, 0, i)),
                  pl.BlockSpec((1, wit.shape[1], tq), lambda b, i: (b, 0, i))],
        out_specs=pl.BlockSpec((1, 1, nq, tq, tq), lambda b, i: (b, i, 0, 0, 0)),
        out_shape=jax.ShapeDtypeStruct((bsz, nq, nq, tq, tq), BF16),
        scratch_shapes=[pltpu.VMEM((nq, tq, tq), F32)],
        compiler_params=_cparams(2),
        name="sel_prompt",
    )(kib, qitb, wit)


def _attn_prompt_kernel(qt_ref, k_ref, vt_ref, mask_ref, o_ref, q2_ref, s_ref, mx_ref, l_ref, out_ref):
    tq = qt_ref.shape[2]
    n_tiles = pl.program_id(1) + 1
    pair = 2 * HEAD_DIM
    part = 4 * SUBLANES
    fold = tq // part
    in_pair = lax.broadcasted_iota(I32, (pair, tq), 0)
    for hp in range(N_HEADS // 2):
        qp = qt_ref[0, hp * pair:(hp + 1) * pair, :]
        q2_ref[2 * hp] = jnp.where(in_pair < HEAD_DIM, qp, jnp.zeros_like(qp))
        q2_ref[2 * hp + 1] = jnp.where(in_pair >= HEAD_DIM, qp, jnp.zeros_like(qp))

    def key_rows(j):
        return pl.ds(pl.multiple_of(j * tq, tq), tq)

    group = s_ref.shape[1]
    n_groups = N_HEADS // group

    def score_tile(j, sel, slot, h0):
        for g in range(group):
            h = h0 + g
            k_pair = k_ref[0, key_rows(j), (h // 2) * pair:(h // 2 + 1) * pair]
            s = jnp.where(sel, _dot(k_pair, q2_ref[h]) * LOG2E, NEG)
            s_ref[slot, g, j] = s
            mx_ref[slot, g] = jnp.maximum(mx_ref[slot, g], jnp.max(s.reshape(fold, part, tq), axis=0))

    ones_rows = jnp.ones((2 * SUBLANES, tq), BF16)

    def weigh_tile(j, slot, h0, m):
        for g in range(group):
            rows = slice((h0 + g) * HEAD_DIM, (h0 + g + 1) * HEAD_DIM)
            p = jnp.exp2(s_ref[slot, g, j] - m[g])
            v_ones = jnp.concatenate([vt_ref[0, rows, key_rows(j)], ones_rows], axis=0)
            pv = _dot(v_ones, p.astype(BF16))
            out_ref[rows, :] = out_ref[rows, :] + pv[:HEAD_DIM]
            l_ref[g] = l_ref[g] + pv[HEAD_DIM:HEAD_DIM + SUBLANES]

    for gi in range(n_groups + 1):
        slot, prev = gi % 2, (gi - 1) % 2
        h0, h_prev = gi * group, (gi - 1) * group
        m_prev = None
        if gi > 0:
            m_prev = [jnp.max(mx_ref[prev, g], axis=0, keepdims=True) for g in range(group)]
            l_ref[...] = jnp.zeros(l_ref.shape, F32)
            out_ref[h_prev * HEAD_DIM:h0 * HEAD_DIM, :] = jnp.zeros((group * HEAD_DIM, tq), F32)
        if gi < n_groups:
            mx_ref[slot] = jnp.full(mx_ref.shape[1:], NEG, F32)

        def sweep(jj, _, gi=gi, slot=slot, prev=prev, h0=h0, h_prev=h_prev, m_prev=m_prev):
            for i in range(TILES_PER_STEP):
                j = jj * TILES_PER_STEP + i
                if gi < n_groups:
                    score_tile(j, mask_ref[0, 0, j].astype(F32) > 0.0, slot, h0)
                if gi > 0:
                    weigh_tile(j, prev, h_prev, m_prev)
            return 0

        lax.fori_loop(0, pl.cdiv(n_tiles, TILES_PER_STEP), sweep, 0)
        if gi > 0:
            for g in range(group):
                rows = slice((h_prev + g) * HEAD_DIM, (h_prev + g + 1) * HEAD_DIM)
                out_ref[rows, :] = out_ref[rows, :] / l_ref[g, 0:1, :]
    o_ref[0] = out_ref[...].T.astype(BF16)


def _attn_prompt(qtb, kb, vtb, mask, *, tq, group=2):
    bsz, da, s = qtb.shape
    nq = s // tq
    part = 4 * SUBLANES
    return pl.pallas_call(
        _attn_prompt_kernel,
        grid=(bsz, nq),
        in_specs=[pl.BlockSpec((1, da, tq), lambda b, i: (b, 0, i)),
                  pl.BlockSpec((1, s, da), lambda b, i: (b, 0, 0)),
                  pl.BlockSpec((1, da, s), lambda b, i: (b, 0, 0)),
                  pl.BlockSpec((1, 1, nq, tq, tq), lambda b, i: (b, i, 0, 0, 0))],
        out_specs=pl.BlockSpec((1, tq, da), lambda b, i: (b, i, 0)),
        out_shape=jax.ShapeDtypeStruct((bsz, s, da), BF16),
        scratch_shapes=[pltpu.VMEM((N_HEADS, 2 * HEAD_DIM, tq), BF16), pltpu.VMEM((2, group, nq, tq, tq), F32),
                        pltpu.VMEM((2, group, part, tq), F32), pltpu.VMEM((group, SUBLANES, tq), F32),
                        pltpu.VMEM((da, tq), F32)],
        compiler_params=_cparams(2),
        name="attn_prompt",
    )(qtb, kb, vtb, mask)


def _score_sample_kernel(pt_ref, qi_ref, wi_ref, kin_ref, cache_ref, past_ref, new_ref, buf_ref, sem_ref, *, chunk):
    seq = pl.program_id(0)
    n_seq = pl.num_programs(0)
    n_pages = buf_ref.shape[1]
    t = new_ref.shape[1]
    slot = lax.rem(seq, 2)

    def page_copy(s, i, buf_slot):
        return pltpu.make_async_copy(cache_ref.at[pt_ref[s, i]], buf_ref.at[buf_slot, i], sem_ref.at[buf_slot])

    def start_pages(s, buf_slot):
        def body(i, _):
            page_copy(s, i, buf_slot).start()
            return 0
        lax.fori_loop(0, n_pages, body, 0)

    @pl.when(seq == 0)
    def _():
        start_pages(seq, slot)

    @pl.when(seq + 1 < n_seq)
    def _():
        start_pages(seq + 1, 1 - slot)

    def wait_body(i, _):
        page_copy(seq, i, slot).wait()
        return 0

    lax.fori_loop(0, n_pages, wait_body, 0)

    qi = qi_ref[0]
    wi = wi_ref[0]

    def scores(dots):
        w = wi * jnp.maximum(dots, 0.0)
        acc = w[0:t]
        for h in range(1, N_IDX_HEADS):
            acc = acc + w[h * t:(h + 1) * t]
        return acc

    for c in range(n_pages // chunk):
        keys_t = jnp.concatenate([buf_ref[slot, c * chunk + i].astype(BF16) for i in range(chunk)], axis=1)
        past_ref[0, :, c * chunk * PAGE_SIZE:(c + 1) * chunk * PAGE_SIZE] = scores(_dot(qi, keys_t))

    ext = jnp.concatenate([kin_ref[0], jnp.zeros((LANES - t, IDX_DIM), F32)], axis=0).astype(BF16)
    sc = scores(_dot_nt(qi, ext))
    qpos = lax.broadcasted_iota(I32, (t, LANES), 0)
    kpos = lax.broadcasted_iota(I32, (t, LANES), 1)
    new_ref[0] = jnp.where(kpos < t, jnp.where(kpos <= qpos, sc, NEG), -jnp.inf)


def _score_sample(page_table, qi_hm, wi_hm, ki_new, cache_idx_k, *, chunk=16):
    n, n_pages = page_table.shape
    t = ki_new.shape[1]
    ht = qi_hm.shape[1]
    grid_spec = pltpu.PrefetchScalarGridSpec(
        num_scalar_prefetch=1,
        grid=(n,),
        in_specs=[pl.BlockSpec((1, ht, IDX_DIM), lambda b, pt: (b, 0, 0)),
                  pl.BlockSpec((1, ht, 1), lambda b, pt: (b, 0, 0)),
                  pl.BlockSpec((1, t, IDX_DIM), lambda b, pt: (b, 0, 0)),
                  pl.BlockSpec(memory_space=pl.ANY)],
        out_specs=[pl.BlockSpec((1, t, n_pages * PAGE_SIZE), lambda b, pt: (b, 0, 0)),
                   pl.BlockSpec((1, t, LANES), lambda b, pt: (b, 0, 0))],
        scratch_shapes=[pltpu.VMEM((2, n_pages, IDX_DIM, PAGE_SIZE), F32), pltpu.SemaphoreType.DMA((2,))],
    )
    return pl.pallas_call(
        functools.partial(_score_sample_kernel, chunk=chunk),
        grid_spec=grid_spec,
        out_shape=[jax.ShapeDtypeStruct((n, t, n_pages * PAGE_SIZE), F32),
                   jax.ShapeDtypeStruct((n, t, LANES), F32)],
        compiler_params=_cparams(1),
        name="score_sample",
    )(page_table, qi_hm, wi_hm, ki_new, cache_idx_k)


def _sel_sample_kernel(past_ref, new_ref, mpast_ref, mnew_ref, key_ref, mask_ref, *, n_sel, t, tile):
    rows = past_ref.shape[0]
    n_past = past_ref.shape[1] // tile
    for j in range(n_past):
        key_ref[j] = past_ref[:, j * tile:(j + 1) * tile]
    pad = jnp.full((rows, tile - LANES), -jnp.inf, F32)
    key_ref[n_past] = jnp.concatenate([new_ref[...], pad], axis=1)
    qpos = lax.rem(lax.broadcasted_iota(I32, (rows, tile), 0), t)

    def valid(j, shape):
        kpos = lax.broadcasted_iota(I32, shape, 1)
        return kpos <= qpos + jnp.minimum(n_past - j, 1) * tile

    def emit(j, sel):
        mask_ref[j] = sel

    idx_bits = int((n_past + 1) * tile - 1).bit_length()
    _select_mask(key_ref, 1, n_past + 1, n_sel, 0, idx_bits, valid, emit)
    for j in range(n_past):
        mpast_ref[:, j * tile:(j + 1) * tile] = mask_ref[j]
    mnew_ref[...] = mask_ref[n_past][:, :LANES]


def _sel_sample(key_past, key_new, *, n_sel, t, rows=64, tile=2048):
    r, n_keys = key_past.shape
    n_t = n_keys // tile + 1
    return pl.pallas_call(
        functools.partial(_sel_sample_kernel, n_sel=n_sel, t=t, tile=tile),
        grid=(r // rows,),
        in_specs=[pl.BlockSpec((rows, n_keys), lambda i: (i, 0)), pl.BlockSpec((rows, LANES), lambda i: (i, 0))],
        out_specs=[pl.BlockSpec((rows, n_keys), lambda i: (i, 0)), pl.BlockSpec((rows, LANES), lambda i: (i, 0))],
        out_shape=[jax.ShapeDtypeStruct((r, n_keys), F32), jax.ShapeDtypeStruct((r, LANES), F32)],
        scratch_shapes=[pltpu.VMEM((n_t, rows, tile), F32), pltpu.VMEM((n_t, rows, tile), F32)],
        compiler_params=_cparams(1),
        name="sel_sample",
    )(key_past, key_new)


def _attn_sample_kernel(pt_ref, q_ref, kn_ref, vn_ref, mpast_ref, mnew_ref, *rest, pages):
    k_refs = rest[:pages]
    v_refs = rest[pages:2 * pages]
    o_ref, qbd_ref, m_ref, l_ref, acc_ref = rest[2 * pages:]
    t, da = q_ref.shape[1], q_ref.shape[2]
    ht = N_HEADS * t
    j = pl.program_id(1)

    @pl.when(j == 0)
    def _():
        q_rep = jnp.concatenate([q_ref[0].astype(F32)] * N_HEADS, axis=0)
        head_of_row = lax.broadcasted_iota(I32, (ht, da), 0) // t
        head_of_col = lax.broadcasted_iota(I32, (ht, da), 1) // HEAD_DIM
        qbd_ref[...] = jnp.where(head_of_row == head_of_col, q_rep, 0.0).astype(BF16)
        m_ref[...] = jnp.full(m_ref.shape, NEG, F32)
        l_ref[...] = jnp.zeros(l_ref.shape, F32)
        acc_ref[...] = jnp.zeros(acc_ref.shape, F32)

    def attend(k_pages, v_pages, mask_t, feature_major):
        qbd = qbd_ref[...]
        qk = [(_dot(qbd, kp) if feature_major else _dot_nt(qbd, kp)) for kp in k_pages]
        sel = jnp.concatenate([mask_t] * N_HEADS, axis=0) > 0.0
        s = jnp.where(sel, qk[0] if len(qk) == 1 else jnp.concatenate(qk, axis=1), NEG)
        m_old = m_ref[...]
        m_new = jnp.maximum(m_old, jnp.max(s, axis=-1, keepdims=True))
        alpha = jnp.exp(m_old - m_new)
        p = jnp.where(sel, jnp.exp(s - m_new), 0.0)
        l_ref[...] = alpha * l_ref[...] + jnp.sum(p, axis=-1, keepdims=True)
        pb = p.astype(BF16)
        pv = None
        for i, vp in enumerate(v_pages):
            p_i = pb[:, i * PAGE_SIZE:(i + 1) * PAGE_SIZE]
            term = _dot_nt(p_i, vp) if feature_major else _dot(p_i, vp)
            pv = term if pv is None else pv + term
        acc_ref[...] = alpha * acc_ref[...] + pv
        m_ref[...] = m_new

    attend([r[0].astype(BF16) for r in k_refs], [r[0].astype(BF16) for r in v_refs], mpast_ref[0], True)

    @pl.when(j == pl.num_programs(1) - 1)
    def _():
        zeros = jnp.zeros((PAGE_SIZE - t, da), F32)
        attend([jnp.concatenate([kn_ref[0], zeros], axis=0).astype(BF16)],
               [jnp.concatenate([vn_ref[0], zeros], axis=0).astype(BF16)], mnew_ref[0], False)
        out = acc_ref[...] / l_ref[...]
        o_ref[0] = jnp.concatenate(
            [out[h * t:(h + 1) * t, h * HEAD_DIM:(h + 1) * HEAD_DIM] for h in range(N_HEADS)], axis=1).astype(BF16)


def _attn_sample(page_table, q, k_new, v_new, mask_past, mask_new, cache_k, cache_v, *, pages):
    n, n_pages = page_table.shape
    t, da = q.shape[1], q.shape[2]
    steps = n_pages // pages
    ht = N_HEADS * t

    def page_spec(i):
        return pl.BlockSpec((1, da, PAGE_SIZE), lambda b, j, pt: (pt[b, j * pages + i], 0, 0))

    per_seq = lambda shape: pl.BlockSpec((1,) + shape, lambda b, j, pt: (b, 0, 0))
    grid_spec = pltpu.PrefetchScalarGridSpec(
        num_scalar_prefetch=1,
        grid=(n, steps),
        in_specs=[per_seq((t, da)), per_seq((t, da)), per_seq((t, da)),
                  pl.BlockSpec((1, t, pages * PAGE_SIZE), lambda b, j, pt: (b, 0, j)), per_seq((t, LANES))]
                 + [page_spec(i) for i in range(pages)] + [page_spec(i) for i in range(pages)],
        out_specs=per_seq((t, da)),
        scratch_shapes=[pltpu.VMEM((ht, da), BF16), pltpu.VMEM((ht, 1), F32), pltpu.VMEM((ht, 1), F32),
                        pltpu.VMEM((ht, da), F32)],
    )
    return pl.pallas_call(
        functools.partial(_attn_sample_kernel, pages=pages),
        grid_spec=grid_spec,
        out_shape=jax.ShapeDtypeStruct((n, t, da), BF16),
        compiler_params=_cparams(2),
        name="attn_sample",
    )(page_table, q, k_new, v_new, mask_past, mask_new, *([cache_k] * pages), *([cache_v] * pages))


def _merge_kernel(x_ref, h_ref, c_ref, a_ref, m_ref, wg_ref, wc_ref, wa_ref, wm_ref, wo_ref, o_ref):
    d = x_ref.shape[1]
    gates = jax.nn.sigmoid(_dot(h_ref[...], wg_ref[...]))
    mix = (gates[:, :d] * _dot(c_ref[...], wc_ref[...])
           + gates[:, d:2 * d] * _dot(a_ref[...], wa_ref[...])
           + gates[:, 2 * d:] * _dot(m_ref[...], wm_ref[...]))
    o_ref[...] = x_ref[...] + _dot(mix.astype(BF16), wo_ref[...])


def _merge(x, h, c, a, mo, wg, wc, wa, wm, wo, *, tm):
    m, d = x.shape
    row = lambda n: pl.BlockSpec((tm, n), lambda i: (i, 0))
    return pl.pallas_call(
        _merge_kernel,
        grid=(m // tm,),
        in_specs=[row(d), row(d), row(c.shape[1]), row(a.shape[1]), row(mo.shape[1]),
                  _const_spec(wg.shape), _const_spec(wc.shape), _const_spec(wa.shape), _const_spec(wm.shape),
                  _const_spec(wo.shape)],
        out_specs=row(d),
        out_shape=jax.ShapeDtypeStruct((m, d), F32),
        compiler_params=_cparams(1),
        name="merge",
    )(x, h, c, a, mo, wg, wc, wa, wm, wo)


def _ffn_kernel(x_ref, g_ref, wi_ref, wo_ref, gf_ref, o_ref):
    x = x_ref[...]
    dff = wo_ref.shape[0]
    hn = _rmsnorm(x, g_ref[...]).astype(BF16)
    a = _dot(hn, wi_ref[:, :dff])
    b = _dot(hn, wi_ref[:, dff:])
    y = x + _dot((a * jax.nn.sigmoid(a) * b).astype(BF16), wo_ref[...])
    o_ref[...] = _rmsnorm(y, gf_ref[...])


def _ffn(x, g, w_in, w_out, g_final, *, tm):
    m, d = x.shape
    return pl.pallas_call(
        _ffn_kernel,
        grid=(m // tm,),
        in_specs=[pl.BlockSpec((tm, d), lambda i: (i, 0)), _const_spec((1, d)), _const_spec(w_in.shape),
                  _const_spec(w_out.shape), _const_spec((1, d))],
        out_specs=pl.BlockSpec((tm, d), lambda i: (i, 0)),
        out_shape=jax.ShapeDtypeStruct((m, d), F32),
        compiler_params=_cparams(1),
        name="ffn",
    )(x, g, w_in, w_out, g_final)


def kernel(x_prompt, x_sample, mem_prompt, cache_conv, cache_k, cache_v, cache_idx_k, cache_mem_k, cache_mem_v,
           page_table, g_mix, w_in, w_conv_dw, b_conv_dw, g_conv_ln, b_conv_ln, w_conv_out, w_att_out, g_mem,
           w_mem_kv, w_mem_out, w_out, g_ffn, w_ffn_in, w_ffn_out, g_final):
    depth = g_mix.shape[0]
    assert depth == 1
    bsz, seq, d = x_prompt.shape
    n_dec, t_dec, _ = x_sample.shape
    n_mem = mem_prompt.shape[1]
    d_conv = w_conv_dw.shape[2]
    d_att = N_HEADS * HEAD_DIM
    d_idx = N_IDX_HEADS * IDX_DIM
    d_mem = MEM_HEADS * MEM_HEAD_DIM
    n_phys = cache_k.shape[1]
    past = page_table.shape[1] * PAGE_SIZE
    l = 0

    splits = [2 * d_conv, d_att, d_att, d_att, d_idx, IDX_DIM, N_IDX_HEADS, d_mem, 3 * d]
    offs = np.concatenate([[0], np.cumsum(splits)])
    w = w_in[l]
    col = lambda i: w[:, offs[i]:offs[i + 1]]
    pad_cols = lambda a, n: jnp.pad(a, ((0, 0), (0, n - a.shape[1])))
    glu, wq, wk, wv, wqi, wki, wwi, wqm, wgates = (col(i) for i in range(9))
    shared = [glu, wqm, pad_cols(wki, LANES)]
    w_prompt = jnp.concatenate(shared, axis=1).astype(BF16)
    w_prompt_t = jnp.concatenate([wk, wv, wki, pad_cols(wwi, WI_ROWS), wqi, wq], axis=1).T.astype(BF16)
    w_sample = jnp.concatenate(shared + [wq, wqi, pad_cols(wwi, LANES), wk, wv], axis=1).astype(BF16)
    w_gates = wgates.astype(BF16)
    bf = lambda a: a[l].astype(BF16)
    row2 = lambda a: a.reshape(1, -1)
    wdw, bdw, gln, bln = w_conv_dw[l], row2(b_conv_dw[l]), row2(g_conv_ln[l]), row2(b_conv_ln[l])
    dims = dict(d_conv=d_conv, d_att=d_att, d_idx=d_idx, d_mem=d_mem)
    merge = functools.partial(_merge, wg=w_gates, wc=bf(w_conv_out), wa=bf(w_att_out), wm=bf(w_mem_out),
                              wo=bf(w_out))
    ffn = functools.partial(_ffn, g=row2(g_ffn[l]), w_in=bf(w_ffn_in), w_out=bf(w_ffn_out),
                            g_final=row2(g_final))

    mp = bsz * seq
    xp = x_prompt.reshape(mp, d)
    u, qm, h, kib, kb, kt, vt, vtb, kit, qitb, wit, qtb = _inproj_prompt(x_prompt, row2(g_mix[l]), w_prompt,
                                                                         w_prompt_t, tm=512, **dims)
    per_seq = lambda a: a.reshape(bsz, seq, a.shape[-1])
    u3 = per_seq(u)
    c = _conv_prompt(u3, wdw, bdw, gln, bln)
    n_sel = min(TOPK_MAX, seq // 4)
    tq = 256
    mask = _sel_prompt(per_seq(kib), qitb, wit, tq=tq, n_sel=n_sel)
    oa = _attn_prompt(qtb, per_seq(kb), vtb, mask, tq=tq)
    mk, mv = _memkv(mem_prompt.reshape(bsz * n_mem, d), row2(g_mem[l]), bf(w_mem_kv))
    om = _memattn(per_seq(qm), mk.reshape(bsz, n_mem, d_mem), mv.reshape(bsz, n_mem, d_mem), tq=512)
    x1 = merge(xp, h, c.reshape(mp, d_conv), oa.reshape(mp, d_att), om.reshape(mp, d_mem), tm=512)
    y_prompt = ffn(x1, tm=256).reshape(bsz, seq, d)
    conv_state_prompt = u3[:, seq - (CONV_WIDTH - 1):][None]
    k_prompt = kt.reshape(bsz, N_HEADS, HEAD_DIM, seq).transpose(0, 3, 1, 2)[None]
    v_prompt = vt.reshape(bsz, N_HEADS, HEAD_DIM, seq).transpose(0, 3, 1, 2)[None]
    idx_k_prompt = kit.transpose(0, 2, 1)[None]
    mem_k_prompt = mk.reshape(1, bsz, n_mem, MEM_HEADS, MEM_HEAD_DIM)
    mem_v_prompt = mv.reshape(1, bsz, n_mem, MEM_HEADS, MEM_HEAD_DIM)

    ms = n_dec * t_dec
    xs = x_sample.reshape(ms, d)
    u, q, qm, h, qi, wi, k, v, ki = _inproj_sample(xs, row2(g_mix[l]), w_sample, **dims)
    per_seq = lambda a: a.reshape(n_dec, t_dec, a.shape[-1])
    u_ext = jnp.concatenate([cache_conv[l], per_seq(u)], axis=1)
    c = _conv_sample(u_ext, wdw, bdw, gln, bln)
    n_sel = min(TOPK_MAX, (past + t_dec) // 4)
    qi_hm = per_seq(qi).reshape(n_dec, t_dec, N_IDX_HEADS, IDX_DIM).transpose(0, 2, 1, 3)
    qi_hm = qi_hm.reshape(n_dec, N_IDX_HEADS * t_dec, IDX_DIM)
    wi_hm = per_seq(wi).transpose(0, 2, 1).reshape(n_dec, N_IDX_HEADS * t_dec, 1)
    page_major = lambda a: jnp.moveaxis(a[l], 1, -1).reshape(n_phys, -1, PAGE_SIZE)
    key_past, key_new = _score_sample(page_table, qi_hm, wi_hm, per_seq(ki), page_major(cache_idx_k))
    mask_past, mask_new = _sel_sample(key_past.reshape(ms, past), key_new.reshape(ms, LANES), n_sel=n_sel, t=t_dec)
    oa = _attn_sample(page_table, per_seq(q), per_seq(k), per_seq(v), mask_past.reshape(n_dec, t_dec, past),
                      mask_new.reshape(n_dec, t_dec, LANES), page_major(cache_k), page_major(cache_v), pages=32)
    om = _memattn_sample(per_seq(qm), cache_mem_k[l].reshape(n_dec, n_mem * MEM_HEADS, MEM_HEAD_DIM),
                         cache_mem_v[l].reshape(n_dec, n_mem * MEM_HEADS, MEM_HEAD_DIM))
    x1 = merge(xs, h, c.reshape(ms, d_conv), oa.reshape(ms, d_att), om.reshape(ms, d_mem), tm=ms)
    y_sample = ffn(x1, tm=ms).reshape(n_dec, t_dec, d)
    conv_state_sample = u_ext[:, t_dec:][None]
    k_sample = k.reshape(1, n_dec, t_dec, N_HEADS, HEAD_DIM)
    v_sample = v.reshape(1, n_dec, t_dec, N_HEADS, HEAD_DIM)
    idx_k_sample = ki.reshape(1, n_dec, t_dec, IDX_DIM)

    return (y_prompt, y_sample, conv_state_prompt, k_prompt, v_prompt, idx_k_prompt, mem_k_prompt, mem_v_prompt,
            conv_state_sample, k_sample, v_sample, idx_k_sample)
```

```python
import functools

import jax
import jax.numpy as jnp
import numpy as np
from jax import lax
from jax.experimental import pallas as pl
from jax.experimental.pallas import tpu as pltpu

EPS = 1e-6
NEG = -1e30
CONV_WIDTH = 31
N_HEADS = 8
HEAD_DIM = 64
N_IDX_HEADS = 8
IDX_DIM = 64
MEM_HEADS = 4
MEM_HEAD_DIM = 128
TOPK_MAX = 256
PAGE_SIZE = 128
LANES = 128
SUBLANES = 8
VMEM_LIMIT = 56 * 1024 * 1024
INT_MIN = -(2 ** 31)
LOG2E = 1.4426950408889634
TILES_PER_STEP = 2

BF16 = jnp.bfloat16
F32 = jnp.float32
I32 = jnp.int32


def _cparams(n_axes):
    return pltpu.CompilerParams(dimension_semantics=("arbitrary",) * n_axes, vmem_limit_bytes=VMEM_LIMIT)


def _const_spec(shape, single=False):
    zeros = (0,) * len(shape)
    if single:
        return pl.BlockSpec(shape, lambda *_: zeros, pipeline_mode=pl.Buffered(1))
    return pl.BlockSpec(shape, lambda *_: zeros)


def _rmsnorm(x, g):
    return x * lax.rsqrt(jnp.mean(x * x, axis=-1, keepdims=True) + EPS) * g


def _dot(a, b):
    return jnp.dot(a, b, preferred_element_type=F32)


def _dot_nt(a, b):
    return lax.dot_general(a, b, (((1,), (1,)), ((), ())), preferred_element_type=F32)


WI_ROWS = 16


def _glu_qm(h, w_ref, u_ref, qm_ref, d_conv, d_mem):
    o = 0
    glu = _dot(h, w_ref[:, o:o + 2 * d_conv]); o += 2 * d_conv
    u_ref[...] = glu[:, :d_conv] * jax.nn.sigmoid(glu[:, d_conv:])
    qm_ref[...] = _dot(h, w_ref[:, o:o + d_mem]).astype(BF16); o += d_mem
    return o


def _inproj_prompt_kernel(x_ref, g_ref, w_ref, wt_ref, u_ref, qm_ref, h_ref, kib_ref, kb_ref,
                          kt_ref, vt_ref, vtb_ref, kit_ref, qit_ref, wit_ref, qt_ref, *, d_conv, d_att, d_idx, d_mem):
    h = _rmsnorm(x_ref[0], g_ref[...]).astype(BF16)
    h_ref[...] = h
    o = _glu_qm(h, w_ref, u_ref, qm_ref, d_conv, d_mem)
    kib_ref[...] = _dot(h, w_ref[:, o:o + LANES])[:, :IDX_DIM].astype(BF16)
    r = 0
    kt = _dot_nt(wt_ref[r:r + d_att, :], h); r += d_att
    kt_ref[0] = kt
    kb_ref[...] = kt.T.astype(BF16)
    vt = _dot_nt(wt_ref[r:r + d_att, :], h); r += d_att
    vt_ref[0] = vt
    vtb_ref[0] = vt.astype(BF16)
    kit_ref[0] = _dot_nt(wt_ref[r:r + IDX_DIM, :], h); r += IDX_DIM
    wit_ref[0] = _dot_nt(wt_ref[r:r + WI_ROWS, :], h)[:N_IDX_HEADS] * (N_IDX_HEADS ** -0.5); r += WI_ROWS
    qit_ref[0] = (_dot_nt(wt_ref[r:r + d_idx, :], h) * (IDX_DIM ** -0.5)).astype(BF16); r += d_idx
    qt_ref[0] = (_dot_nt(wt_ref[r:r + d_att, :], h) * (HEAD_DIM ** -0.5)).astype(BF16)


def _inproj_prompt(x, g, w, wt, *, tm, d_conv, d_att, d_idx, d_mem):
    bsz, s, d = x.shape
    nt = s // tm
    row = lambda n: pl.BlockSpec((tm, n), lambda b, i: (b * nt + i, 0))
    col = lambda n: pl.BlockSpec((1, n, tm), lambda b, i: (b, 0, i))
    outs = [(d_conv, F32), (d_mem, BF16), (d, BF16), (IDX_DIM, BF16), (d_att, BF16)]
    outs_t = [(d_att, F32), (d_att, F32), (d_att, BF16), (IDX_DIM, F32), (d_idx, BF16), (N_IDX_HEADS, F32),
              (d_att, BF16)]
    return pl.pallas_call(
        functools.partial(_inproj_prompt_kernel, d_conv=d_conv, d_att=d_att, d_idx=d_idx, d_mem=d_mem),
        grid=(bsz, nt),
        in_specs=[pl.BlockSpec((1, tm, d), lambda b, i: (b, i, 0)), _const_spec((1, d)), _const_spec(w.shape),
                  _const_spec(wt.shape)],
        out_specs=[row(n) for n, _ in outs] + [col(n) for n, _ in outs_t],
        out_shape=[jax.ShapeDtypeStruct((bsz * s, n), dt) for n, dt in outs]
                  + [jax.ShapeDtypeStruct((bsz, n, s), dt) for n, dt in outs_t],
        compiler_params=_cparams(2),
        name="inproj_prompt",
    )(x, g, w, wt)


def _inproj_sample_kernel(x_ref, g_ref, w_ref, u_ref, q_ref, qm_ref, h_ref, qi_ref, wi_ref, k_ref, v_ref, ki_ref,
                          *, d_conv, d_att, d_idx, d_mem):
    h = _rmsnorm(x_ref[...], g_ref[...]).astype(BF16)
    h_ref[...] = h
    o = _glu_qm(h, w_ref, u_ref, qm_ref, d_conv, d_mem)
    ki_ref[...] = _dot(h, w_ref[:, o:o + LANES])[:, :IDX_DIM]; o += LANES
    q_ref[...] = (_dot(h, w_ref[:, o:o + d_att]) * (HEAD_DIM ** -0.5)).astype(BF16); o += d_att
    qi_ref[...] = (_dot(h, w_ref[:, o:o + d_idx]) * (IDX_DIM ** -0.5)).astype(BF16); o += d_idx
    wi_ref[...] = _dot(h, w_ref[:, o:o + LANES])[:, :N_IDX_HEADS] * (N_IDX_HEADS ** -0.5); o += LANES
    k_ref[...] = _dot(h, w_ref[:, o:o + d_att]); o += d_att
    v_ref[...] = _dot(h, w_ref[:, o:o + d_att])


def _inproj_sample(x, g, w, *, d_conv, d_att, d_idx, d_mem):
    m, d = x.shape
    outs = [(d_conv, F32), (d_att, BF16), (d_mem, BF16), (d, BF16), (d_idx, BF16), (N_IDX_HEADS, F32),
            (d_att, F32), (d_att, F32), (IDX_DIM, F32)]
    return pl.pallas_call(
        functools.partial(_inproj_sample_kernel, d_conv=d_conv, d_att=d_att, d_idx=d_idx, d_mem=d_mem),
        grid=(1,),
        in_specs=[_const_spec((m, d)), _const_spec((1, d)), _const_spec(w.shape)],
        out_specs=[_const_spec((m, n)) for n, _ in outs],
        out_shape=[jax.ShapeDtypeStruct((m, n), dt) for n, dt in outs],
        compiler_params=_cparams(1),
        name="inproj_sample",
    )(x, g, w)


def _conv_taps(ext_ref, start, rows, w_ref, by_phase):
    if not by_phase:
        acc = w_ref[0:1, :] * ext_ref[pl.ds(start, rows), :]
        for j in range(1, CONV_WIDTH):
            acc = acc + w_ref[j:j + 1, :] * ext_ref[pl.ds(start + j, rows), :]
        return acc
    acc = None
    for r in range(SUBLANES):
        y = None
        for j in range(CONV_WIDTH):
            if (start + j) % SUBLANES == r:
                term = w_ref[j:j + 1, :] * ext_ref[pl.ds(start + j - r, rows + (SUBLANES if r else 0)), :]
                y = term if y is None else y + term
        if y is not None:
            y = y[r:r + rows] if r else y
            acc = y if acc is None else acc + y
    return acc


def _conv_rows(ext_ref, start, rows, w_ref, b_ref, g_ref, bl_ref, by_phase=False):
    c = _conv_taps(ext_ref, start, rows, w_ref, by_phase) + b_ref[...]
    mu = jnp.mean(c, axis=-1, keepdims=True)
    xc = c - mu
    y = xc * lax.rsqrt(jnp.mean(xc * xc, axis=-1, keepdims=True) + EPS) * g_ref[...] + bl_ref[...]
    return (y * jax.nn.sigmoid(y)).astype(BF16)


def _conv_prompt_kernel(prev_ref, cur_ref, w_ref, b_ref, g_ref, bl_ref, o_ref, ext_ref, *, halo, chunk):
    t = cur_ref.shape[1]

    @pl.when(pl.program_id(1) == 0)
    def _():
        ext_ref[0:halo, :] = jnp.zeros((halo, ext_ref.shape[1]), F32)

    @pl.when(pl.program_id(1) > 0)
    def _():
        ext_ref[0:halo, :] = prev_ref[0]

    ext_ref[halo:halo + t, :] = cur_ref[0]
    ext_ref[halo + t:, :] = jnp.zeros((SUBLANES, ext_ref.shape[1]), F32)
    first = halo - (CONV_WIDTH - 1)
    for c in range(t // chunk):
        o_ref[0, c * chunk:(c + 1) * chunk, :] = _conv_rows(ext_ref, first + c * chunk, chunk, w_ref, b_ref, g_ref,
                                                            bl_ref, by_phase=True)


def _conv_prompt(u, w, b, g, bl, *, t=512, halo=32, chunk=128):
    bsz, s, dc = u.shape
    assert s % t == 0 and t % halo == 0 and halo >= CONV_WIDTH - 1 and t % chunk == 0
    r = t // halo
    return pl.pallas_call(
        functools.partial(_conv_prompt_kernel, halo=halo, chunk=chunk),
        grid=(bsz, s // t),
        in_specs=[pl.BlockSpec((1, halo, dc), lambda bi, i: (bi, jnp.maximum(i * r - 1, 0), 0)),
                  pl.BlockSpec((1, t, dc), lambda bi, i: (bi, i, 0)),
                  _const_spec(w.shape), _const_spec((1, dc)), _const_spec((1, dc)), _const_spec((1, dc))],
        out_specs=pl.BlockSpec((1, t, dc), lambda bi, i: (bi, i, 0)),
        out_shape=jax.ShapeDtypeStruct((bsz, s, dc), BF16),
        scratch_shapes=[pltpu.VMEM((halo + t + SUBLANES, dc), F32)],
        compiler_params=_cparams(2),
        name="conv_prompt",
    )(u, u, w, b, g, bl)


def _conv_sample_kernel(ext_ref, w_ref, b_ref, g_ref, bl_ref, o_ref):
    t = o_ref.shape[1]
    for s in range(o_ref.shape[0]):
        o_ref[s] = _conv_rows(ext_ref.at[s], 0, t, w_ref, b_ref, g_ref, bl_ref)


def _conv_sample(u_ext, w, b, g, bl, *, seqs=8):
    n, te, dc = u_ext.shape
    t = te - (CONV_WIDTH - 1)
    return pl.pallas_call(
        _conv_sample_kernel,
        grid=(n // seqs,),
        in_specs=[pl.BlockSpec((seqs, te, dc), lambda i: (i, 0, 0)),
                  _const_spec(w.shape), _const_spec((1, dc)), _const_spec((1, dc)), _const_spec((1, dc))],
        out_specs=pl.BlockSpec((seqs, t, dc), lambda i: (i, 0, 0)),
        out_shape=jax.ShapeDtypeStruct((n, t, dc), BF16),
        compiler_params=_cparams(1),
        name="conv_sample",
    )(u_ext, w, b, g, bl)


def _memkv_kernel(x_ref, g_ref, w_ref, k_ref, v_ref):
    h = _rmsnorm(x_ref[...], g_ref[...]).astype(BF16)
    kv = _dot(h, w_ref[...])
    d = k_ref.shape[1]
    k_ref[...] = kv[:, :d]
    v_ref[...] = kv[:, d:]


def _memkv(x, g, w, *, tm=256):
    m, d = x.shape
    dm = w.shape[1] // 2
    return pl.pallas_call(
        _memkv_kernel,
        grid=(m // tm,),
        in_specs=[pl.BlockSpec((tm, d), lambda i: (i, 0)), _const_spec((1, d)), _const_spec(w.shape)],
        out_specs=[pl.BlockSpec((tm, dm), lambda i: (i, 0))] * 2,
        out_shape=[jax.ShapeDtypeStruct((m, dm), F32)] * 2,
        compiler_params=_cparams(1),
        name="memkv",
    )(x, g, w)


def _memattn_kernel(q_ref, k_ref, v_ref, o_ref):
    for h in range(MEM_HEADS):
        sl = slice(h * MEM_HEAD_DIM, (h + 1) * MEM_HEAD_DIM)
        s = _dot_nt(q_ref[0, :, sl], k_ref[0, :, sl].astype(BF16)) * (MEM_HEAD_DIM ** -0.5)
        e = jnp.exp(s - jnp.max(s, axis=-1, keepdims=True))
        p = e / jnp.sum(e, axis=-1, keepdims=True)
        o_ref[0, :, sl] = _dot(p.astype(BF16), v_ref[0, :, sl].astype(BF16)).astype(BF16)


def _memattn(q, k, v, *, tq):
    n, t, dm = q.shape
    nm = k.shape[1]
    return pl.pallas_call(
        _memattn_kernel,
        grid=(n, t // tq),
        in_specs=[pl.BlockSpec((1, tq, dm), lambda i, j: (i, j, 0)),
                  pl.BlockSpec((1, nm, dm), lambda i, j: (i, 0, 0)),
                  pl.BlockSpec((1, nm, dm), lambda i, j: (i, 0, 0))],
        out_specs=pl.BlockSpec((1, tq, dm), lambda i, j: (i, j, 0)),
        out_shape=jax.ShapeDtypeStruct((n, t, dm), BF16),
        compiler_params=_cparams(2),
        name="memattn",
    )(q, k, v)


def _memattn_sample_kernel(q_ref, k_ref, v_ref, o_ref):
    t = q_ref.shape[1]
    rows = k_ref.shape[1]
    row_head = lax.broadcasted_iota(I32, (MEM_HEADS * t, rows), 0) // t
    col_head = lax.rem(lax.broadcasted_iota(I32, (MEM_HEADS * t, rows), 1), MEM_HEADS)
    same_head = row_head == col_head
    for g in range(q_ref.shape[0]):
        q_hm = jnp.concatenate([q_ref[g, :, h * MEM_HEAD_DIM:(h + 1) * MEM_HEAD_DIM].astype(F32)
                                for h in range(MEM_HEADS)], axis=0).astype(BF16)
        s = _dot_nt(q_hm, k_ref[g].astype(BF16)) * (MEM_HEAD_DIM ** -0.5)
        s = jnp.where(same_head, s, NEG)
        e = jnp.exp(s - jnp.max(s, axis=-1, keepdims=True))
        p = e / jnp.sum(e, axis=-1, keepdims=True)
        out = _dot(p.astype(BF16), v_ref[g].astype(BF16))
        for h in range(MEM_HEADS):
            o_ref[g, :, h * MEM_HEAD_DIM:(h + 1) * MEM_HEAD_DIM] = out[h * t:(h + 1) * t].astype(BF16)


def _memattn_sample(q, k, v, *, seqs=4):
    n, t, dm = q.shape
    rows = k.shape[1]
    return pl.pallas_call(
        _memattn_sample_kernel,
        grid=(n // seqs,),
        in_specs=[pl.BlockSpec((seqs, t, dm), lambda i: (i, 0, 0)),
                  pl.BlockSpec((seqs, rows, MEM_HEAD_DIM), lambda i: (i, 0, 0)),
                  pl.BlockSpec((seqs, rows, MEM_HEAD_DIM), lambda i: (i, 0, 0))],
        out_specs=pl.BlockSpec((seqs, t, dm), lambda i: (i, 0, 0)),
        out_shape=jax.ShapeDtypeStruct((n, t, dm), BF16),
        compiler_params=_cparams(1),
        name="memattn_sample",
    )(q, k, v)


def _ordinal_to_float(o):
    return pltpu.bitcast(jnp.where(o >= 0, o, (-o) | INT_MIN), F32)


def _float_to_ordinal(x):
    b = pltpu.bitcast(x, I32)
    return jnp.where(b < 0, -(b & 0x7FFFFFFF), b)


ORD_NEG_INF = -0x7F800000
ORD_MIN_NORMAL = 0x00800000
FIRST_CHECK = 20
CHECK_EVERY = 2
VALUE_PROBES = 48
MAX_PROBES = VALUE_PROBES + 34


def _select_mask(score_ref, key_axis, n_tiles, n_sel, outside, idx_bits, valid_fn, emit_fn, extreme_parts=None):
    _, ta, tb = score_ref.shape
    tile = (ta, tb)
    unit = LANES if key_axis == 1 else 4 * SUBLANES
    tk = tile[key_axis]
    fold = tk // unit
    shp = (ta, LANES) if key_axis == 1 else (unit, tb)
    st = (ta, LANES) if key_axis == 1 else (SUBLANES, tb)
    k_f = jnp.float32(n_sel)
    out_f = jnp.float32(1.0) * outside

    def key_fold(x, op=jnp.add):
        if key_axis == 0:
            x = x.reshape(fold, unit, tb)
            acc = x[0]
            for i in range(1, fold):
                acc = op(acc, x[i])
            return acc
        acc = x[:, :LANES]
        for i in range(1, fold):
            acc = op(acc, x[:, i * LANES:(i + 1) * LANES])
        return acc

    def rep(x):
        n = tk // st[key_axis]
        return x if n == 1 else jnp.concatenate([x] * n, axis=key_axis)

    def row_total(part):
        return jnp.broadcast_to(jnp.sum(part, axis=key_axis, keepdims=True), st)

    def key_index(j):
        return j * tk + lax.broadcasted_iota(I32, tile, key_axis)

    def count_ge(thr):
        thr_t = rep(thr)

        def body(j, part):
            return part + key_fold(jnp.where(score_ref[j] >= thr_t, 1.0, 0.0))

        part = lax.fori_loop(0, n_tiles, body, jnp.zeros(shp, F32))
        return row_total(part) + jnp.where(thr <= NEG, out_f, 0.0)

    def extremes(j, carry):
        mx, mn = carry
        sc = score_ref[j]
        return (jnp.maximum(mx, key_fold(sc, jnp.maximum)),
                jnp.minimum(mn, key_fold(jnp.where(sc > NEG, sc, jnp.inf), jnp.minimum)))

    if extreme_parts is None:
        extreme_parts = lax.fori_loop(0, n_tiles, extremes,
                                      (jnp.full(shp, -jnp.inf, F32), jnp.full(shp, jnp.inf, F32)))
    mx, mn = extreme_parts
    mx = jnp.broadcast_to(jnp.max(mx, axis=key_axis, keepdims=True), st)
    mx = jnp.maximum(mx, jnp.where(out_f > 0.0, NEG, -jnp.inf))
    mn_o = _float_to_ordinal(jnp.broadcast_to(jnp.min(mn, axis=key_axis, keepdims=True), st))

    def is_settled(lo_o, hi_o, c_lo):
        return ((c_lo == k_f) | (hi_o == lo_o + 1) | ((lo_o >= 0) & (hi_o <= ORD_MIN_NORMAL))
                | ((hi_o <= 0) & (lo_o >= -ORD_MIN_NORMAL)))

    def probe(it, state):
        lo_o, hi_o, c_lo = state
        settled = is_settled(lo_o, hi_o, c_lo)
        omid = (lo_o >> 1) + (hi_o >> 1) + (lo_o & hi_o & 1)
        vmid = _float_to_ordinal(0.5 * _ordinal_to_float(lo_o) + 0.5 * _ordinal_to_float(hi_o))
        one_sign = (lo_o >= 0) | (hi_o <= 0)
        cand = jnp.where(one_sign, vmid, omid)
        cand = jnp.where(it == 0, mn_o, jnp.where(it == 1, 0, jnp.where(it == 2, ORD_MIN_NORMAL, cand)))
        cand = jnp.where(it < VALUE_PROBES, cand, omid)
        mid = jnp.where((cand > lo_o) & (cand < hi_o), cand, omid)
        cnt = count_ge(_ordinal_to_float(mid))
        up = (cnt >= k_f) & ~settled
        down = (cnt < k_f) & ~settled
        return jnp.where(up, mid, lo_o), jnp.where(down, mid, hi_o), jnp.where(up, cnt, c_lo)

    def unsettled(state):
        return jnp.max(jnp.where(is_settled(*state), 0.0, 1.0)).astype(I32)

    def probe_block(carry):
        it, state, _ = carry
        state = lax.fori_loop(it, it + CHECK_EVERY, probe, state)
        return it + CHECK_EVERY, state, unsettled(state)

    total = jnp.float32(1.0) * (n_tiles * tk) + out_f
    state = (jnp.full(st, ORD_NEG_INF, I32), _float_to_ordinal(mx) + 1, jnp.broadcast_to(total, st))
    state = lax.fori_loop(0, FIRST_CHECK, probe, state)
    _, (lo_o, _, cnt_ge), _ = lax.while_loop(lambda c: (c[2] > 0) & (c[0] < MAX_PROBES), probe_block,
                                             (jnp.int32(FIRST_CHECK), state, unsettled(state)))
    thr = _ordinal_to_float(lo_o)
    thr_t = rep(thr)

    surplus = jnp.max(jnp.where(cnt_ge > k_f, 1.0, 0.0)) > 0.0

    def tie_search():
        def count_gt_body(j, part):
            return part + key_fold(jnp.where(score_ref[j] > thr_t, 1.0, 0.0))

        cnt_gt = row_total(lax.fori_loop(0, n_tiles, count_gt_body, jnp.zeros(shp, F32)))
        cnt_gt = cnt_gt + jnp.where(thr < NEG, out_f, 0.0)
        in_tiles = n_tiles * tk

        def idx_step(it, v):
            bit = lax.shift_left(jnp.int32(1), idx_bits - 1 - it)
            cand = v | bit
            cand_t = rep(cand)

            def body(j, part):
                hit = jnp.where(score_ref[j] == thr_t, jnp.where(key_index(j) < cand_t, 1.0, 0.0), 0.0)
                return part + key_fold(hit)

            ties_below = row_total(lax.fori_loop(0, n_tiles, body, jnp.zeros(shp, F32)))
            out_below = jnp.clip(cand - in_tiles, 0, outside).astype(F32)
            ties_below = ties_below + jnp.where(thr == NEG, out_below, 0.0)
            return jnp.where(cnt_gt + ties_below < k_f, cand, v)

        return lax.fori_loop(0, idx_bits, idx_step, jnp.zeros(st, I32))

    last = n_tiles - 1

    def emit_with_ties():
        jmax_t = rep(tie_search())

        def emit(j, _):
            sc = score_ref[j]
            tied = jnp.where(sc == thr_t, jnp.where(key_index(j) <= jmax_t, 1.0, 0.0), 0.0)
            emit_fn(j, jnp.where(valid_fn(j, tile), jnp.where(sc > thr_t, 1.0, tied), 0.0))
            return 0

        lax.fori_loop(0, n_tiles, emit, 0)

    def emit_plain():
        def emit(j, _):
            emit_fn(j, jnp.where(score_ref[j] >= thr_t, 1.0, 0.0))
            return 0

        lax.fori_loop(0, last, emit, 0)
        emit_fn(last, jnp.where(valid_fn(last, tile), jnp.where(score_ref[last] >= thr_t, 1.0, 0.0), 0.0))

    lax.cond(surplus, emit_with_ties, emit_plain)


def _sel_prompt_kernel(kib_ref, qit_ref, wit_ref, mask_ref, sc_ref, *, n_sel):
    tq = qit_ref.shape[2]
    s = kib_ref.shape[1]
    n_all = s // tq
    qb = pl.program_id(1)
    n_tiles = qb + 1
    qpos = qb * tq + lax.broadcasted_iota(I32, (tq, tq), 1)

    def score_tile(j):
        ki_t = kib_ref[0, pl.ds(pl.multiple_of(j * tq, tq), tq), :]
        acc = jnp.zeros((tq, tq), F32)
        for h in range(N_IDX_HEADS):
            d = _dot(ki_t, qit_ref[0, h * IDX_DIM:(h + 1) * IDX_DIM, :])
            acc = acc + wit_ref[0, h:h + 1, :] * jnp.maximum(d, 0.0)
        return acc

    part = 4 * SUBLANES

    def extremes(sc, mx, mn):
        sc = sc.reshape(tq // part, part, tq)
        return (jnp.maximum(mx, jnp.max(sc, axis=0)),
                jnp.minimum(mn, jnp.min(jnp.where(sc > NEG, sc, jnp.inf), axis=0)))

    def below(j, carry):
        sc = score_tile(j)
        sc_ref[j] = sc
        return extremes(sc, *carry)

    def below_pair(jj, carry):
        return below(2 * jj + 1, below(2 * jj, carry))

    pairs = lax.shift_right_logical(qb, 1)
    carry = lax.fori_loop(0, pairs, below_pair,
                          (jnp.full((part, tq), -jnp.inf, F32), jnp.full((part, tq), jnp.inf, F32)))
    mx, mn = lax.fori_loop(2 * pairs, qb, below, carry)
    kpos = qb * tq + lax.broadcasted_iota(I32, (tq, tq), 0)
    diag = jnp.where(kpos <= qpos, score_tile(qb), NEG)
    sc_ref[qb] = diag
    mx, mn = extremes(diag, mx, mn)

    def valid(j, shape):
        return j * tq + lax.broadcasted_iota(I32, shape, 0) <= qpos

    def emit(j, sel):
        mask_ref[0, 0, j] = sel.astype(BF16)

    idx_bits = max(1, int(s - 1).bit_length())
    _select_mask(sc_ref, 0, n_tiles, n_sel, (n_all - n_tiles) * tq, idx_bits, valid, emit, (mx, mn))

    def clear(j, _):
        mask_ref[0, 0, j] = jnp.zeros((tq, tq), BF16)
        return 0

    lax.fori_loop(n_tiles, n_all, clear, 0)


def _sel_prompt(kib, qitb, wit, *, tq, n_sel):
    bsz, s, _ = kib.shape
    nq = s // tq
    return pl.pallas_call(
        functools.partial(_sel_prompt_kernel, n_sel=n_sel),
        grid=(bsz, nq),
        in_specs=[pl.BlockSpec((1, s, kib.shape[2]), lambda b, i: (b, 0, 0)),
                  pl.BlockSpec((1, qitb.shape[1], tq), lambda b, i: (b, 0, i)),
                  pl.BlockSpec((1, wit.shape[1], tq), lambda b, i: (b, 0, i))],
        out_specs=pl.BlockSpec((1, 1, nq, tq, tq), lambda b, i: (b, i, 0, 0, 0)),
        out_shape=jax.ShapeDtypeStruct((bsz, nq, nq, tq, tq), BF16),
        scratch_shapes=[pltpu.VMEM((nq, tq, tq), F32)],
        compiler_params=_cparams(2),
        name="sel_prompt",
    )(kib, qitb, wit)


def _attn_prompt_kernel(qt_ref, k_ref, vt_ref, mask_ref, o_ref, q2_ref, s_ref, mx_ref, l_ref, out_ref):
    tq = qt_ref.shape[2]
    n_tiles = pl.program_id(1) + 1
    pair = 2 * HEAD_DIM
    part = 4 * SUBLANES
    fold = tq // part
    in_pair = lax.broadcasted_iota(I32, (pair, tq), 0)
    for hp in range(N_HEADS // 2):
        qp = qt_ref[0, hp * pair:(hp + 1) * pair, :]
        q2_ref[2 * hp] = jnp.where(in_pair < HEAD_DIM, qp, jnp.zeros_like(qp))
        q2_ref[2 * hp + 1] = jnp.where(in_pair >= HEAD_DIM, qp, jnp.zeros_like(qp))

    def key_rows(j):
        return pl.ds(pl.multiple_of(j * tq, tq), tq)

    group = s_ref.shape[1]
    n_groups = N_HEADS // group

    def score_tile(j, sel, slot, h0):
        for g in range(group):
            h = h0 + g
            k_pair = k_ref[0, key_rows(j), (h // 2) * pair:(h // 2 + 1) * pair]
            s = jnp.where(sel, _dot(k_pair, q2_ref[h]) * LOG2E, NEG)
            s_ref[slot, g, j] = s
            mx_ref[slot, g] = jnp.maximum(mx_ref[slot, g], jnp.max(s.reshape(fold, part, tq), axis=0))

    ones_rows = jnp.ones((2 * SUBLANES, tq), BF16)

    def weigh_tile(j, slot, h0, m):
        for g in range(group):
            rows = slice((h0 + g) * HEAD_DIM, (h0 + g + 1) * HEAD_DIM)
            p = jnp.exp2(s_ref[slot, g, j] - m[g])
            v_ones = jnp.concatenate([vt_ref[0, rows, key_rows(j)], ones_rows], axis=0)
            pv = _dot(v_ones, p.astype(BF16))
            out_ref[rows, :] = out_ref[rows, :] + pv[:HEAD_DIM]
            l_ref[g] = l_ref[g] + pv[HEAD_DIM:HEAD_DIM + SUBLANES]

    for gi in range(n_groups + 1):
        slot, prev = gi % 2, (gi - 1) % 2
        h0, h_prev = gi * group, (gi - 1) * group
        m_prev = None
        if gi > 0:
            m_prev = [jnp.max(mx_ref[prev, g], axis=0, keepdims=True) for g in range(group)]
            l_ref[...] = jnp.zeros(l_ref.shape, F32)
            out_ref[h_prev * HEAD_DIM:h0 * HEAD_DIM, :] = jnp.zeros((group * HEAD_DIM, tq), F32)
        if gi < n_groups:
            mx_ref[slot] = jnp.full(mx_ref.shape[1:], NEG, F32)

        def sweep(jj, _, gi=gi, slot=slot, prev=prev, h0=h0, h_prev=h_prev, m_prev=m_prev):
            for i in range(TILES_PER_STEP):
                j = jj * TILES_PER_STEP + i
                if gi < n_groups:
                    score_tile(j, mask_ref[0, 0, j].astype(F32) > 0.0, slot, h0)
                if gi > 0:
                    weigh_tile(j, prev, h_prev, m_prev)
            return 0

        lax.fori_loop(0, pl.cdiv(n_tiles, TILES_PER_STEP), sweep, 0)
        if gi > 0:
            for g in range(group):
                rows = slice((h_prev + g) * HEAD_DIM, (h_prev + g + 1) * HEAD_DIM)
                out_ref[rows, :] = out_ref[rows, :] / l_ref[g, 0:1, :]
    o_ref[0] = out_ref[...].T.astype(BF16)


def _attn_prompt(qtb, kb, vtb, mask, *, tq, group=2):
    bsz, da, s = qtb.shape
    nq = s // tq
    part = 4 * SUBLANES
    return pl.pallas_call(
        _attn_prompt_kernel,
        grid=(bsz, nq),
        in_specs=[pl.BlockSpec((1, da, tq), lambda b, i: (b, 0, i)),
                  pl.BlockSpec((1, s, da), lambda b, i: (b, 0, 0)),
                  pl.BlockSpec((1, da, s), lambda b, i: (b, 0, 0)),
                  pl.BlockSpec((1, 1, nq, tq, tq), lambda b, i: (b, i, 0, 0, 0))],
        out_specs=pl.BlockSpec((1, tq, da), lambda b, i: (b, i, 0)),
        out_shape=jax.ShapeDtypeStruct((bsz, s, da), BF16),
        scratch_shapes=[pltpu.VMEM((N_HEADS, 2 * HEAD_DIM, tq), BF16), pltpu.VMEM((2, group, nq, tq, tq), F32),
                        pltpu.VMEM((2, group, part, tq), F32), pltpu.VMEM((group, SUBLANES, tq), F32),
                        pltpu.VMEM((da, tq), F32)],
        compiler_params=_cparams(2),
        name="attn_prompt",
    )(qtb, kb, vtb, mask)


def _score_sample_kernel(pt_ref, qi_ref, wi_ref, kin_ref, cache_ref, past_ref, new_ref, buf_ref, sem_ref, *, chunk):
    seq = pl.program_id(0)
    n_seq = pl.num_programs(0)
    n_pages = buf_ref.shape[1]
    t = new_ref.shape[1]
    slot = lax.rem(seq, 2)

    def page_copy(s, i, buf_slot):
        return pltpu.make_async_copy(cache_ref.at[pt_ref[s, i]], buf_ref.at[buf_slot, i], sem_ref.at[buf_slot])

    def start_pages(s, buf_slot):
        def body(i, _):
            page_copy(s, i, buf_slot).start()
            return 0
        lax.fori_loop(0, n_pages, body, 0)

    @pl.when(seq == 0)
    def _():
        start_pages(seq, slot)

    @pl.when(seq + 1 < n_seq)
    def _():
        start_pages(seq + 1, 1 - slot)

    def wait_body(i, _):
        page_copy(seq, i, slot).wait()
        return 0

    lax.fori_loop(0, n_pages, wait_body, 0)

    qi = qi_ref[0]
    wi = wi_ref[0]

    def scores(dots):
        w = wi * jnp.maximum(dots, 0.0)
        acc = w[0:t]
        for h in range(1, N_IDX_HEADS):
            acc = acc + w[h * t:(h + 1) * t]
        return acc

    for c in range(n_pages // chunk):
        keys_t = jnp.concatenate([buf_ref[slot, c * chunk + i].astype(BF16) for i in range(chunk)], axis=1)
        past_ref[0, :, c * chunk * PAGE_SIZE:(c + 1) * chunk * PAGE_SIZE] = scores(_dot(qi, keys_t))

    ext = jnp.concatenate([kin_ref[0], jnp.zeros((LANES - t, IDX_DIM), F32)], axis=0).astype(BF16)
    sc = scores(_dot_nt(qi, ext))
    qpos = lax.broadcasted_iota(I32, (t, LANES), 0)
    kpos = lax.broadcasted_iota(I32, (t, LANES), 1)
    new_ref[0] = jnp.where(kpos < t, jnp.where(kpos <= qpos, sc, NEG), -jnp.inf)


def _score_sample(page_table, qi_hm, wi_hm, ki_new, cache_idx_k, *, chunk=16):
    n, n_pages = page_table.shape
    t = ki_new.shape[1]
    ht = qi_hm.shape[1]
    grid_spec = pltpu.PrefetchScalarGridSpec(
        num_scalar_prefetch=1,
        grid=(n,),
        in_specs=[pl.BlockSpec((1, ht, IDX_DIM), lambda b, pt: (b, 0, 0)),
                  pl.BlockSpec((1, ht, 1), lambda b, pt: (b, 0, 0)),
                  pl.BlockSpec((1, t, IDX_DIM), lambda b, pt: (b, 0, 0)),
                  pl.BlockSpec(memory_space=pl.ANY)],
        out_specs=[pl.BlockSpec((1, t, n_pages * PAGE_SIZE), lambda b, pt: (b, 0, 0)),
                   pl.BlockSpec((1, t, LANES), lambda b, pt: (b, 0, 0))],
        scratch_shapes=[pltpu.VMEM((2, n_pages, IDX_DIM, PAGE_SIZE), F32), pltpu.SemaphoreType.DMA((2,))],
    )
    return pl.pallas_call(
        functools.partial(_score_sample_kernel, chunk=chunk),
        grid_spec=grid_spec,
        out_shape=[jax.ShapeDtypeStruct((n, t, n_pages * PAGE_SIZE), F32),
                   jax.ShapeDtypeStruct((n, t, LANES), F32)],
        compiler_params=_cparams(1),
        name="score_sample",
    )(page_table, qi_hm, wi_hm, ki_new, cache_idx_k)


def _sel_sample_kernel(past_ref, new_ref, mpast_ref, mnew_ref, key_ref, mask_ref, *, n_sel, t, tile):
    rows = past_ref.shape[0]
    n_past = past_ref.shape[1] // tile
    for j in range(n_past):
        key_ref[j] = past_ref[:, j * tile:(j + 1) * tile]
    pad = jnp.full((rows, tile - LANES), -jnp.inf, F32)
    key_ref[n_past] = jnp.concatenate([new_ref[...], pad], axis=1)
    qpos = lax.rem(lax.broadcasted_iota(I32, (rows, tile), 0), t)

    def valid(j, shape):
        kpos = lax.broadcasted_iota(I32, shape, 1)
        return kpos <= qpos + jnp.minimum(n_past - j, 1) * tile

    def emit(j, sel):
        mask_ref[j] = sel

    idx_bits = int((n_past + 1) * tile - 1).bit_length()
    _select_mask(key_ref, 1, n_past + 1, n_sel, 0, idx_bits, valid, emit)
    for j in range(n_past):
        mpast_ref[:, j * tile:(j + 1) * tile] = mask_ref[j]
    mnew_ref[...] = mask_ref[n_past][:, :LANES]


def _sel_sample(key_past, key_new, *, n_sel, t, rows=64, tile=2048):
    r, n_keys = key_past.shape
    n_t = n_keys // tile + 1
    return pl.pallas_call(
        functools.partial(_sel_sample_kernel, n_sel=n_sel, t=t, tile=tile),
        grid=(r // rows,),
        in_specs=[pl.BlockSpec((rows, n_keys), lambda i: (i, 0)), pl.BlockSpec((rows, LANES), lambda i: (i, 0))],
        out_specs=[pl.BlockSpec((rows, n_keys), lambda i: (i, 0)), pl.BlockSpec((rows, LANES), lambda i: (i, 0))],
        out_shape=[jax.ShapeDtypeStruct((r, n_keys), F32), jax.ShapeDtypeStruct((r, LANES), F32)],
        scratch_shapes=[pltpu.VMEM((n_t, rows, tile), F32), pltpu.VMEM((n_t, rows, tile), F32)],
        compiler_params=_cparams(1),
        name="sel_sample",
    )(key_past, key_new)


def _attn_sample_kernel(pt_ref, q_ref, kn_ref, vn_ref, mpast_ref, mnew_ref, *rest, pages):
    k_refs = rest[:pages]
    v_refs = rest[pages:2 * pages]
    o_ref, qbd_ref, m_ref, l_ref, acc_ref = rest[2 * pages:]
    t, da = q_ref.shape[1], q_ref.shape[2]
    ht = N_HEADS * t
    j = pl.program_id(1)

    @pl.when(j == 0)
    def _():
        q_rep = jnp.concatenate([q_ref[0].astype(F32)] * N_HEADS, axis=0)
        head_of_row = lax.broadcasted_iota(I32, (ht, da), 0) // t
        head_of_col = lax.broadcasted_iota(I32, (ht, da), 1) // HEAD_DIM
        qbd_ref[...] = jnp.where(head_of_row == head_of_col, q_rep, 0.0).astype(BF16)
        m_ref[...] = jnp.full(m_ref.shape, NEG, F32)
        l_ref[...] = jnp.zeros(l_ref.shape, F32)
        acc_ref[...] = jnp.zeros(acc_ref.shape, F32)

    def attend(k_pages, v_pages, mask_t, feature_major):
        qbd = qbd_ref[...]
        qk = [(_dot(qbd, kp) if feature_major else _dot_nt(qbd, kp)) for kp in k_pages]
        sel = jnp.concatenate([mask_t] * N_HEADS, axis=0) > 0.0
        s = jnp.where(sel, qk[0] if len(qk) == 1 else jnp.concatenate(qk, axis=1), NEG)
        m_old = m_ref[...]
        m_new = jnp.maximum(m_old, jnp.max(s, axis=-1, keepdims=True))
        alpha = jnp.exp(m_old - m_new)
        p = jnp.where(sel, jnp.exp(s - m_new), 0.0)
        l_ref[...] = alpha * l_ref[...] + jnp.sum(p, axis=-1, keepdims=True)
        pb = p.astype(BF16)
        pv = None
        for i, vp in enumerate(v_pages):
            p_i = pb[:, i * PAGE_SIZE:(i + 1) * PAGE_SIZE]
            term = _dot_nt(p_i, vp) if feature_major else _dot(p_i, vp)
            pv = term if pv is None else pv + term
        acc_ref[...] = alpha * acc_ref[...] + pv
        m_ref[...] = m_new

    attend([r[0].astype(BF16) for r in k_refs], [r[0].astype(BF16) for r in v_refs], mpast_ref[0], True)

    @pl.when(j == pl.num_programs(1) - 1)
    def _():
        zeros = jnp.zeros((PAGE_SIZE - t, da), F32)
        attend([jnp.concatenate([kn_ref[0], zeros], axis=0).astype(BF16)],
               [jnp.concatenate([vn_ref[0], zeros], axis=0).astype(BF16)], mnew_ref[0], False)
        out = acc_ref[...] / l_ref[...]
        o_ref[0] = jnp.concatenate(
            [out[h * t:(h + 1) * t, h * HEAD_DIM:(h + 1) * HEAD_DIM] for h in range(N_HEADS)], axis=1).astype(BF16)


def _attn_sample(page_table, q, k_new, v_new, mask_past, mask_new, cache_k, cache_v, *, pages):
    n, n_pages = page_table.shape
    t, da = q.shape[1], q.shape[2]
    steps = n_pages // pages
    ht = N_HEADS * t

    def page_spec(i):
        return pl.BlockSpec((1, da, PAGE_SIZE), lambda b, j, pt: (pt[b, j * pages + i], 0, 0))

    per_seq = lambda shape: pl.BlockSpec((1,) + shape, lambda b, j, pt: (b, 0, 0))
    grid_spec = pltpu.PrefetchScalarGridSpec(
        num_scalar_prefetch=1,
        grid=(n, steps),
        in_specs=[per_seq((t, da)), per_seq((t, da)), per_seq((t, da)),
                  pl.BlockSpec((1, t, pages * PAGE_SIZE), lambda b, j, pt: (b, 0, j)), per_seq((t, LANES))]
                 + [page_spec(i) for i in range(pages)] + [page_spec(i) for i in range(pages)],
        out_specs=per_seq((t, da)),
        scratch_shapes=[pltpu.VMEM((ht, da), BF16), pltpu.VMEM((ht, 1), F32), pltpu.VMEM((ht, 1), F32),
                        pltpu.VMEM((ht, da), F32)],
    )
    return pl.pallas_call(
        functools.partial(_attn_sample_kernel, pages=pages),
        grid_spec=grid_spec,
        out_shape=jax.ShapeDtypeStruct((n, t, da), BF16),
        compiler_params=_cparams(2),
        name="attn_sample",
    )(page_table, q, k_new, v_new, mask_past, mask_new, *([cache_k] * pages), *([cache_v] * pages))


def _merge_kernel(x_ref, h_ref, c_ref, a_ref, m_ref, wg_ref, wc_ref, wa_ref, wm_ref, wo_ref, o_ref):
    d = x_ref.shape[1]
    gates = jax.nn.sigmoid(_dot(h_ref[...], wg_ref[...]))
    mix = (gates[:, :d] * _dot(c_ref[...], wc_ref[...])
           + gates[:, d:2 * d] * _dot(a_ref[...], wa_ref[...])
           + gates[:, 2 * d:] * _dot(m_ref[...], wm_ref[...]))
    o_ref[...] = x_ref[...] + _dot(mix.astype(BF16), wo_ref[...])


def _merge(x, h, c, a, mo, wg, wc, wa, wm, wo, *, tm):
    m, d = x.shape
    row = lambda n: pl.BlockSpec((tm, n), lambda i: (i, 0))
    return pl.pallas_call(
        _merge_kernel,
        grid=(m // tm,),
        in_specs=[row(d), row(d), row(c.shape[1]), row(a.shape[1]), row(mo.shape[1]),
                  _const_spec(wg.shape), _const_spec(wc.shape), _const_spec(wa.shape), _const_spec(wm.shape),
                  _const_spec(wo.shape)],
        out_specs=row(d),
        out_shape=jax.ShapeDtypeStruct((m, d), F32),
        compiler_params=_cparams(1),
        name="merge",
    )(x, h, c, a, mo, wg, wc, wa, wm, wo)


def _ffn_kernel(x_ref, g_ref, wi_ref, wo_ref, gf_ref, o_ref):
    x = x_ref[...]
    dff = wo_ref.shape[0]
    hn = _rmsnorm(x, g_ref[...]).astype(BF16)
    a = _dot(hn, wi_ref[:, :dff])
    b = _dot(hn, wi_ref[:, dff:])
    y = x + _dot((a * jax.nn.sigmoid(a) * b).astype(BF16), wo_ref[...])
    o_ref[...] = _rmsnorm(y, gf_ref[...])


def _ffn(x, g, w_in, w_out, g_final, *, tm):
    m, d = x.shape
    return pl.pallas_call(
        _ffn_kernel,
        grid=(m // tm,),
        in_specs=[pl.BlockSpec((tm, d), lambda i: (i, 0)), _const_spec((1, d)), _const_spec(w_in.shape, single=True),
                  _const_spec(w_out.shape, single=True), _const_spec((1, d))],
        out_specs=pl.BlockSpec((tm, d), lambda i: (i, 0)),
        out_shape=jax.ShapeDtypeStruct((m, d), F32),
        compiler_params=_cparams(1),
        name="ffn",
    )(x, g, w_in, w_out, g_final)


def kernel(x_prompt, x_sample, mem_prompt, cache_conv, cache_k, cache_v, cache_idx_k, cache_mem_k, cache_mem_v,
           page_table, g_mix, w_in, w_conv_dw, b_conv_dw, g_conv_ln, b_conv_ln, w_conv_out, w_att_out, g_mem,
           w_mem_kv, w_mem_out, w_out, g_ffn, w_ffn_in, w_ffn_out, g_final):
    depth = g_mix.shape[0]
    assert depth == 1
    bsz, seq, d = x_prompt.shape
    n_dec, t_dec, _ = x_sample.shape
    n_mem = mem_prompt.shape[1]
    d_conv = w_conv_dw.shape[2]
    d_att = N_HEADS * HEAD_DIM
    d_idx = N_IDX_HEADS * IDX_DIM
    d_mem = MEM_HEADS * MEM_HEAD_DIM
    n_phys = cache_k.shape[1]
    past = page_table.shape[1] * PAGE_SIZE
    l = 0

    splits = [2 * d_conv, d_att, d_att, d_att, d_idx, IDX_DIM, N_IDX_HEADS, d_mem, 3 * d]
    offs = np.concatenate([[0], np.cumsum(splits)])
    w = w_in[l]
    col = lambda i: w[:, offs[i]:offs[i + 1]]
    pad_cols = lambda a, n: jnp.pad(a, ((0, 0), (0, n - a.shape[1])))
    glu, wq, wk, wv, wqi, wki, wwi, wqm, wgates = (col(i) for i in range(9))
    shared = [glu, wqm, pad_cols(wki, LANES)]
    w_prompt = jnp.concatenate(shared, axis=1).astype(BF16)
    w_prompt_t = jnp.concatenate([wk, wv, wki, pad_cols(wwi, WI_ROWS), wqi, wq], axis=1).T.astype(BF16)
    w_sample = jnp.concatenate(shared + [wq, wqi, pad_cols(wwi, LANES), wk, wv], axis=1).astype(BF16)
    w_gates = wgates.astype(BF16)
    bf = lambda a: a[l].astype(BF16)
    row2 = lambda a: a.reshape(1, -1)
    wdw, bdw, gln, bln = w_conv_dw[l], row2(b_conv_dw[l]), row2(g_conv_ln[l]), row2(b_conv_ln[l])
    dims = dict(d_conv=d_conv, d_att=d_att, d_idx=d_idx, d_mem=d_mem)
    merge = functools.partial(_merge, wg=w_gates, wc=bf(w_conv_out), wa=bf(w_att_out), wm=bf(w_mem_out),
                              wo=bf(w_out))
    ffn = functools.partial(_ffn, g=row2(g_ffn[l]), w_in=bf(w_ffn_in), w_out=bf(w_ffn_out),
                            g_final=row2(g_final))

    mp = bsz * seq
    xp = x_prompt.reshape(mp, d)
    u, qm, h, kib, kb, kt, vt, vtb, kit, qitb, wit, qtb = _inproj_prompt(x_prompt, row2(g_mix[l]), w_prompt,
                                                                         w_prompt_t, tm=512, **dims)
    per_seq = lambda a: a.reshape(bsz, seq, a.shape[-1])
    u3 = per_seq(u)
    c = _conv_prompt(u3, wdw, bdw, gln, bln)
    n_sel = min(TOPK_MAX, seq // 4)
    tq = 256
    mask = _sel_prompt(per_seq(kib), qitb, wit, tq=tq, n_sel=n_sel)
    oa = _attn_prompt(qtb, per_seq(kb), vtb, mask, tq=tq)
    mk, mv = _memkv(mem_prompt.reshape(bsz * n_mem, d), row2(g_mem[l]), bf(w_mem_kv))
    om = _memattn(per_seq(qm), mk.reshape(bsz, n_mem, d_mem), mv.reshape(bsz, n_mem, d_mem), tq=512)
    x1 = merge(xp, h, c.reshape(mp, d_conv), oa.reshape(mp, d_att), om.reshape(mp, d_mem), tm=512)
    y_prompt = ffn(x1, tm=512).reshape(bsz, seq, d)
    conv_state_prompt = u3[:, seq - (CONV_WIDTH - 1):][None]
    k_prompt = kt.reshape(bsz, N_HEADS, HEAD_DIM, seq).transpose(0, 3, 1, 2)[None]
    v_prompt = vt.reshape(bsz, N_HEADS, HEAD_DIM, seq).transpose(0, 3, 1, 2)[None]
    idx_k_prompt = kit.transpose(0, 2, 1)[None]
    mem_k_prompt = mk.reshape(1, bsz, n_mem, MEM_HEADS, MEM_HEAD_DIM)
    mem_v_prompt = mv.reshape(1, bsz, n_mem, MEM_HEADS, MEM_HEAD_DIM)

    ms = n_dec * t_dec
    xs = x_sample.reshape(ms, d)
    u, q, qm, h, qi, wi, k, v, ki = _inproj_sample(xs, row2(g_mix[l]), w_sample, **dims)
    per_seq = lambda a: a.reshape(n_dec, t_dec, a.shape[-1])
    u_ext = jnp.concatenate([cache_conv[l], per_seq(u)], axis=1)
    c = _conv_sample(u_ext, wdw, bdw, gln, bln)
    n_sel = min(TOPK_MAX, (past + t_dec) // 4)
    qi_hm = per_seq(qi).reshape(n_dec, t_dec, N_IDX_HEADS, IDX_DIM).transpose(0, 2, 1, 3)
    qi_hm = qi_hm.reshape(n_dec, N_IDX_HEADS * t_dec, IDX_DIM)
    wi_hm = per_seq(wi).transpose(0, 2, 1).reshape(n_dec, N_IDX_HEADS * t_dec, 1)
    page_major = lambda a: jnp.moveaxis(a[l], 1, -1).reshape(n_phys, -1, PAGE_SIZE)
    key_past, key_new = _score_sample(page_table, qi_hm, wi_hm, per_seq(ki), page_major(cache_idx_k))
    mask_past, mask_new = _sel_sample(key_past.reshape(ms, past), key_new.reshape(ms, LANES), n_sel=n_sel, t=t_dec)
    oa = _attn_sample(page_table, per_seq(q), per_seq(k), per_seq(v), mask_past.reshape(n_dec, t_dec, past),
                      mask_new.reshape(n_dec, t_dec, LANES), page_major(cache_k), page_major(cache_v), pages=32)
    om = _memattn_sample(per_seq(qm), cache_mem_k[l].reshape(n_dec, n_mem * MEM_HEADS, MEM_HEAD_DIM),
                         cache_mem_v[l].reshape(n_dec, n_mem * MEM_HEADS, MEM_HEAD_DIM))
    x1 = merge(xs, h, c.reshape(ms, d_conv), oa.reshape(ms, d_att), om.reshape(ms, d_mem), tm=ms)
    y_sample = ffn(x1, tm=ms).reshape(n_dec, t_dec, d)
    conv_state_sample = u_ext[:, t_dec:][None]
    k_sample = k.reshape(1, n_dec, t_dec, N_HEADS, HEAD_DIM)
    v_sample = v.reshape(1, n_dec, t_dec, N_HEADS, HEAD_DIM)
    idx_k_sample = ki.reshape(1, n_dec, t_dec, IDX_DIM)

    return (y_prompt, y_sample, conv_state_prompt, k_prompt, v_prompt, idx_k_prompt, mem_k_prompt, mem_v_prompt,
            conv_state_sample, k_sample, v_sample, idx_k_sample)
```
